```python
import jax
import jax.numpy as jnp
from jax import lax
import numpy as np

D_MODEL = 1024
BATCH = 32
SEQ = 2048
DEPTH = 4

CHUNK = 64
Q_BLOCK = 128
ROPE_THETA = 10000.0
NEG_INF = -1e30
LN_EPS = 1e-5
RMS_EPS = 1e-6

H_MLA = 8
D_NOPE = 64
D_ROPE = 32
D_V_MLA = 64
Q_LORA = 384
KV_LORA = 256

H_DSA = 8
D_DSA = 64
H_IDX = 4
D_IDX = 64
TOPK_MAX = 256

H_FOX = 16
D_FOX = 64
FORGET_BIAS = 2.0

N_GROUPS = 4
EXPERTS_PER_GROUP = 4
N_EXPERTS = N_GROUPS * EXPERTS_PER_GROUP
D_EXPERT = 512
TOP_K_INNER = 2

DN_ALPHA = (2 * DEPTH) ** 0.25
DN_BETA = (8 * DEPTH) ** -0.25

N_EVEN = (DEPTH + 1) // 2
N_ODD = DEPTH // 2

EV_SPLITS = (Q_LORA, KV_LORA, D_ROPE, H_DSA * D_DSA, D_DSA, D_DSA, H_IDX * D_IDX, D_IDX, H_IDX)
EV_IN = sum(EV_SPLITS)
EV_OUT = H_MLA * D_V_MLA + H_DSA * D_DSA
OD_SPLITS = (H_FOX * D_FOX, H_FOX * D_FOX, H_FOX * D_FOX, H_FOX)
OD_IN = sum(OD_SPLITS)
OD_OUT = H_FOX * D_FOX

kernel_name = 'hybrid_mla_dsa_fox_hmoe_deepnorm'


def _split(h, sizes):
    offs = np.cumsum(sizes)[:-1].tolist()
    return jnp.split(h, offs, axis=-1)


def _layer_norm(x, g, b):
    xf = x.astype(jnp.float32)
    mu = jnp.mean(xf, axis=-1, keepdims=True)
    var = jnp.mean(jnp.square(xf - mu), axis=-1, keepdims=True)
    y = (xf - mu) * lax.rsqrt(var + LN_EPS)
    return (y * g.astype(jnp.float32) + b.astype(jnp.float32)).astype(x.dtype)


def _rms_norm(x, g):
    xf = x.astype(jnp.float32)
    y = xf * lax.rsqrt(jnp.mean(jnp.square(xf), axis=-1, keepdims=True) + RMS_EPS)
    return (y * g.astype(jnp.float32)).astype(x.dtype)


def _rope(x, pos):
    d = x.shape[-1]
    inv = jnp.power(ROPE_THETA, -jnp.arange(0, d, 2, dtype=jnp.float32) / d)
    ang = pos.astype(jnp.float32)[..., None] * inv
    cos = jnp.cos(ang)[:, :, None, :]
    sin = jnp.sin(ang)[:, :, None, :]
    xf = x.astype(jnp.float32)
    x1, x2 = xf[..., : d // 2], xf[..., d // 2:]
    return jnp.concatenate([x1 * cos - x2 * sin, x2 * cos + x1 * sin], axis=-1).astype(x.dtype)


def _mla_attention(q_nope, q_rope, k_nope, k_rope, v):
    S = q_nope.shape[1]
    scale = (D_NOPE + D_ROPE) ** -0.5
    outs = []
    for i in range(S // Q_BLOCK):
        q0, q1 = i * Q_BLOCK, (i + 1) * Q_BLOCK
        s = (jnp.einsum('bqhd,bkhd->bhqk', q_nope[:, q0:q1], k_nope[:, :q1])
             + jnp.einsum('bqhd,bkd->bhqk', q_rope[:, q0:q1], k_rope[:, :q1]))
        s = s.astype(jnp.float32) * scale
        mask = (jnp.arange(q0, q1) // CHUNK)[:, None] >= (jnp.arange(q1) // CHUNK)[None, :]
        p = jax.nn.softmax(jnp.where(mask, s, NEG_INF), axis=-1).astype(v.dtype)
        outs.append(jnp.einsum('bhqk,bkhd->bqhd', p, v[:, :q1]))
    return jnp.concatenate(outs, axis=1)


def _dsa_attention(q, k, v, q_idx, k_idx, w_idx, k_sel):
    B, S, H, D = q.shape
    nb = S // Q_BLOCK
    scale = D ** -0.5
    key_chunk = jnp.arange(S) // CHUNK
    gather = jax.vmap(lambda a, i: a[i])

    def to_blocks(a):
        return a.reshape((B, nb, Q_BLOCK) + a.shape[2:]).swapaxes(0, 1)

    def block(args):
        qb, qib, wib, q0 = args
        q_chunk = (q0 + jnp.arange(Q_BLOCK)) // CHUNK
        adm = q_chunk[:, None] >= key_chunk[None, :]
        rel = jax.nn.relu(jnp.einsum('bqhd,bkd->bqhk', qib, k_idx).astype(jnp.float32))
        score = jnp.einsum('bqhk,bqh->bqk', rel, wib.astype(jnp.float32))
        score = jnp.where(adm[None], score, NEG_INF)
        _, sel = lax.top_k(score, k_sel)
        valid = (sel // CHUNK) <= q_chunk[None, :, None]
        k_g = gather(k, sel)
        v_g = gather(v, sel)
        s = jnp.einsum('bqhd,bqkd->bqhk', qb, k_g).astype(jnp.float32) * scale
        s = jnp.where(valid[:, :, None, :], s, NEG_INF)
        p = jax.nn.softmax(s, axis=-1).astype(v.dtype)
        return jnp.einsum('bqhk,bqkd->bqhd', p, v_g)

    out = lax.map(block, (to_blocks(q), to_blocks(q_idx), to_blocks(w_idx),
                          jnp.arange(nb, dtype=jnp.int32) * Q_BLOCK))
    return out.swapaxes(0, 1).reshape(B, S, H, D)


def _fox_attention(q, k, v, log_f):
    S = q.shape[1]
    scale = q.shape[-1] ** -0.5
    c = jnp.swapaxes(jnp.cumsum(log_f, axis=1), 1, 2)
    outs = []
    for i in range(S // Q_BLOCK):
        q0, q1 = i * Q_BLOCK, (i + 1) * Q_BLOCK
        s = jnp.einsum('bqhd,bkhd->bhqk', q[:, q0:q1], k[:, :q1]).astype(jnp.float32) * scale
        s = s + c[:, :, q0:q1, None] - c[:, :, None, :q1]
        mask = jnp.arange(q0, q1)[:, None] >= jnp.arange(q1)[None, :]
        p = jax.nn.softmax(jnp.where(mask, s, NEG_INF), axis=-1).astype(v.dtype)
        outs.append(jnp.einsum('bhqk,bkhd->bqhd', p, v[:, :q1]))
    return jnp.concatenate(outs, axis=1)


def _even_mixer(x, pos, w_in, g_q, g_kv, w_uq, w_ukv, w_o, k_sel):
    B, S, _ = x.shape
    h = x @ w_in
    c_q, c_kv, k_r, q_b, k_b, v_b, q_i, k_i, w_i = _split(h, EV_SPLITS)
    q = (_rms_norm(c_q, g_q) @ w_uq).reshape(B, S, H_MLA, D_NOPE + D_ROPE)
    q_nope, q_rope = q[..., :D_NOPE], _rope(q[..., D_NOPE:], pos)
    kv = (_rms_norm(c_kv, g_kv) @ w_ukv).reshape(B, S, H_MLA, D_NOPE + D_V_MLA)
    k_nope, v_a = kv[..., :D_NOPE], kv[..., D_NOPE:]
    k_rope = _rope(k_r[:, :, None, :], pos)[:, :, 0]
    o_a = _mla_attention(q_nope, q_rope, k_nope, k_rope, v_a)
    q_b = _rope(q_b.reshape(B, S, H_DSA, D_DSA), pos)
    k_b = _rope(k_b[:, :, None, :], pos)[:, :, 0]
    q_i = _rope(q_i.reshape(B, S, H_IDX, D_IDX), pos)
    k_i = _rope(k_i[:, :, None, :], pos)[:, :, 0]
    o_b = _dsa_attention(q_b, k_b, v_b, q_i, k_i, w_i, k_sel)
    o = jnp.concatenate([o_a, o_b], axis=2).reshape(B, S, EV_OUT)
    return o @ w_o


def _odd_mixer(x, w_in, b_f, w_o):
    B, S, _ = x.shape
    q, k, v, f = _split(x @ w_in, OD_SPLITS)
    q = q.reshape(B, S, H_FOX, D_FOX)
    k = k.reshape(B, S, H_FOX, D_FOX)
    v = v.reshape(B, S, H_FOX, D_FOX)
    log_f = jax.nn.log_sigmoid((f + b_f).astype(jnp.float32))
    o = _fox_attention(q, k, v, log_f).reshape(B, S, OD_OUT)
    return o @ w_o


def _hier_moe(x, w_grp, b_grp, w_sub, b_sub, w_gate, w_up, w_down):
    B, S, D = x.shape
    t = x.reshape(-1, D)
    lg = (t @ w_grp + b_grp).astype(jnp.float32)
    p_grp = jax.nn.softmax(lg, axis=-1)
    g_star = jnp.argmax(lg, axis=-1)
    p_top = jnp.take_along_axis(p_grp, g_star[:, None], axis=1)
    ls = (t @ w_sub + b_sub).astype(jnp.float32).reshape(-1, N_GROUPS, EXPERTS_PER_GROUP)
    ls = jnp.take_along_axis(ls, g_star[:, None, None], axis=1)[:, 0]
    v2, j2 = lax.top_k(ls, TOP_K_INNER)
    w2 = jax.nn.softmax(v2, axis=-1) * p_top
    eid = g_star[:, None] * EXPERTS_PER_GROUP + j2
    gates = jnp.sum(jax.nn.one_hot(eid, N_EXPERTS, dtype=jnp.float32) * w2[..., None],
                    axis=1).astype(x.dtype)
    y = jnp.zeros_like(t)
    for e in range(N_EXPERTS):
        h = jax.nn.silu(t @ w_gate[e]) * (t @ w_up[e])
        y = y + gates[:, e:e + 1] * (h @ w_down[e])
    return y.reshape(B, S, D)


def setup_inputs(seed: int = 0) -> dict:
    key = jax.random.key(seed)
    ks = jax.random.split(key, 24)
    f32 = jnp.float32

    def nrm(k, shape, scale):
        return jax.random.normal(k, shape, f32) * scale

    x = jax.random.normal(ks[0], (BATCH, SEQ, D_MODEL), f32)
    offsets = jax.random.randint(ks[1], (BATCH, 1), 0, 4096, dtype=jnp.int32)
    positions = offsets + jnp.arange(SEQ, dtype=jnp.int32)[None, :]
    ev_w_in = nrm(ks[2], (N_EVEN, D_MODEL, EV_IN), D_MODEL ** -0.5)
    ev_g_q = 1.0 + nrm(ks[3], (N_EVEN, Q_LORA), 0.02)
    ev_g_kv = 1.0 + nrm(ks[4], (N_EVEN, KV_LORA), 0.02)
    ev_w_uq = nrm(ks[5], (N_EVEN, Q_LORA, H_MLA * (D_NOPE + D_ROPE)), Q_LORA ** -0.5)
    ev_w_ukv = nrm(ks[6], (N_EVEN, KV_LORA, H_MLA * (D_NOPE + D_V_MLA)), KV_LORA ** -0.5)
    ev_w_o = nrm(ks[7], (N_EVEN, EV_OUT, D_MODEL), DN_BETA * EV_OUT ** -0.5)
    od_w_in = nrm(ks[8], (N_ODD, D_MODEL, OD_IN), D_MODEL ** -0.5)
    od_b_f = FORGET_BIAS + nrm(ks[9], (N_ODD, H_FOX), 0.1)
    od_w_o = nrm(ks[10], (N_ODD, OD_OUT, D_MODEL), DN_BETA * OD_OUT ** -0.5)
    moe_w_grp = nrm(ks[11], (DEPTH, D_MODEL, N_GROUPS), D_MODEL ** -0.5)
    moe_b_grp = nrm(ks[12], (DEPTH, N_GROUPS), 0.01)
    moe_w_sub = nrm(ks[13], (DEPTH, D_MODEL, N_EXPERTS), D_MODEL ** -0.5)
    moe_b_sub = nrm(ks[14], (DEPTH, N_EXPERTS), 0.01)
    moe_w_gate = nrm(ks[15], (DEPTH, N_EXPERTS, D_MODEL, D_EXPERT), D_MODEL ** -0.5)
    moe_w_up = nrm(ks[16], (DEPTH, N_EXPERTS, D_MODEL, D_EXPERT), D_MODEL ** -0.5)
    moe_w_down = nrm(ks[17], (DEPTH, N_EXPERTS, D_EXPERT, D_MODEL), DN_BETA * D_EXPERT ** -0.5)
    ln1_g = 1.0 + nrm(ks[18], (DEPTH, D_MODEL), 0.02)
    ln1_b = nrm(ks[19], (DEPTH, D_MODEL), 0.02)
    ln2_g = 1.0 + nrm(ks[20], (DEPTH, D_MODEL), 0.02)
    ln2_b = nrm(ks[21], (DEPTH, D_MODEL), 0.02)
    return {
        'x': x, 'positions': positions,
        'ev_w_in': ev_w_in, 'ev_g_q': ev_g_q, 'ev_g_kv': ev_g_kv,
        'ev_w_uq': ev_w_uq, 'ev_w_ukv': ev_w_ukv, 'ev_w_o': ev_w_o,
        'od_w_in': od_w_in, 'od_b_f': od_b_f, 'od_w_o': od_w_o,
        'moe_w_grp': moe_w_grp, 'moe_b_grp': moe_b_grp,
        'moe_w_sub': moe_w_sub, 'moe_b_sub': moe_b_sub,
        'moe_w_gate': moe_w_gate, 'moe_w_up': moe_w_up, 'moe_w_down': moe_w_down,
        'ln1_g': ln1_g, 'ln1_b': ln1_b, 'ln2_g': ln2_g, 'ln2_b': ln2_b,
    }


def reference(x, positions, ev_w_in, ev_g_q, ev_g_kv, ev_w_uq, ev_w_ukv, ev_w_o,
              od_w_in, od_b_f, od_w_o, moe_w_grp, moe_b_grp, moe_w_sub, moe_b_sub,
              moe_w_gate, moe_w_up, moe_w_down, ln1_g, ln1_b, ln2_g, ln2_b):
    S = x.shape[1]
    k_sel = min(TOPK_MAX, S // 4)
    for layer in range(DEPTH):
        j = layer // 2
        if layer % 2 == 0:
            y = _even_mixer(x, positions, ev_w_in[j], ev_g_q[j], ev_g_kv[j],
                            ev_w_uq[j], ev_w_ukv[j], ev_w_o[j], k_sel)
        else:
            y = _odd_mixer(x, od_w_in[j], od_b_f[j], od_w_o[j])
        x = _layer_norm(DN_ALPHA * x + y, ln1_g[layer], ln1_b[layer])
        y = _hier_moe(x, moe_w_grp[layer], moe_b_grp[layer], moe_w_sub[layer], moe_b_sub[layer],
                      moe_w_gate[layer], moe_w_up[layer], moe_w_down[layer])
        x = _layer_norm(DN_ALPHA * x + y, ln2_g[layer], ln2_b[layer])
    return x
```

```python
import functools

import jax
import jax.numpy as jnp
from jax import lax
from jax.experimental import pallas as pl
from jax.experimental.pallas import tpu as pltpu

D_MODEL = 1024
DEPTH = 4
CHUNK = 64
CHUNK_SHIFT = 6
ROPE_THETA = 10000.0
NEG_INF = -1e30
LN_EPS = 1e-5
RMS_EPS = 1e-6
H_MLA, D_NOPE, D_ROPE, D_V_MLA, Q_LORA, KV_LORA = 8, 64, 32, 64, 384, 256
H_DSA, D_DSA, H_IDX, D_IDX, TOPK_MAX = 8, 64, 4, 64, 256
H_FOX, D_FOX = 16, 64
N_GROUPS, EXPERTS_PER_GROUP, D_EXPERT = 4, 4, 512
N_EXPERTS = N_GROUPS * EXPERTS_PER_GROUP
DN_ALPHA = (2 * DEPTH) ** 0.25
EV_SPLITS = (Q_LORA, KV_LORA, D_ROPE, H_DSA * D_DSA, D_DSA, D_DSA, H_IDX * D_IDX, D_IDX, H_IDX)

LANES = 128
SUBLANES = 8
VMEM_LIMIT_BYTES = 56 * 1024 * 1024

TM_PROJ = 512
TM_MOE = 512
T_ATT = 512
Q_DSA = 128
KC_DSA = 256

F32 = jnp.float32
BF16 = jnp.bfloat16
_NT = (((1,), (1,)), ((), ()))
_TN = (((0,), (0,)), ((), ()))


def _params(*sem):
    return pltpu.CompilerParams(dimension_semantics=sem, vmem_limit_bytes=VMEM_LIMIT_BYTES)


def _lane_iota(shape=(1, LANES)):
    return lax.broadcasted_iota(jnp.int32, shape, len(shape) - 1)


def _full_spec(shape):
    return pl.BlockSpec(shape, lambda *_: (0,) * len(shape))


def _rope_table_kernel(pos_ref, inv32_ref, sg32_ref, inv64_ref, sg64_ref,
                       c32_ref, s32_ref, c64_ref, s64_ref):
    pos = pos_ref[...]
    a32 = pos * inv32_ref[...]
    a64 = pos * inv64_ref[...]
    c32_ref[...] = jnp.cos(a32)
    s32_ref[...] = jnp.sin(a32) * sg32_ref[...]
    c64_ref[...] = jnp.cos(a64)
    s64_ref[...] = jnp.sin(a64) * sg64_ref[...]


def _rope_tables(posf):
    T = posf.shape[0]
    tm = min(T, 2048)
    lane = jnp.arange(LANES)
    inv16 = jnp.power(ROPE_THETA, -jnp.arange(0, D_ROPE, 2, dtype=F32) / D_ROPE)
    inv32 = jnp.power(ROPE_THETA, -jnp.arange(0, D_DSA, 2, dtype=F32) / D_DSA)
    in32 = (lane >= D_NOPE) & (lane < D_NOPE + D_ROPE)
    t32 = jnp.where(in32, inv16[(lane - D_NOPE) % (D_ROPE // 2)], 0.0).astype(F32)[None]
    g32 = jnp.where(in32, jnp.where(lane < D_NOPE + D_ROPE // 2, -1.0, 1.0), 0.0).astype(F32)[None]
    t64 = inv32[lane % (D_DSA // 2)].astype(F32)[None]
    g64 = jnp.where((lane % D_DSA) < D_DSA // 2, -1.0, 1.0).astype(F32)[None]
    row = pl.BlockSpec((tm, LANES), lambda i: (i, 0))
    tab = jax.ShapeDtypeStruct((T, LANES), F32)
    return pl.pallas_call(
        _rope_table_kernel,
        grid=(T // tm,),
        in_specs=[pl.BlockSpec((tm, 1), lambda i: (i, 0))] + [_full_spec((1, LANES))] * 4,
        out_specs=[row] * 4,
        out_shape=[tab] * 4,
        compiler_params=_params("parallel"),
        name="rope_tables",
    )(posf, t32, g32, t64, g64)


def _rope_block(y, c, s, half, second_half):
    partner = jnp.where(second_half, pltpu.roll(y, half, 1), pltpu.roll(y, LANES - half, 1))
    return y * c + partner * s


def _rms(x, g):
    return x * lax.rsqrt(jnp.mean(x * x, axis=-1, keepdims=True) + RMS_EPS) * g


def _layer_norm(z, g, b):
    mu = jnp.mean(z, axis=-1, keepdims=True)
    d = z - mu
    var = jnp.mean(d * d, axis=-1, keepdims=True)
    return d * lax.rsqrt(var + LN_EPS) * g + b


_E_CQ, _E_CKV, _E_KR, _E_QB, _E_KB, _E_VB, _E_QI, _E_KI, _E_WI, _E_END = (
    0, 384, 640, 768, 1280, 1408, 1536, 1792, 1920, 2048)


def _even_proj_kernel(x_ref, w1_ref, gq_ref, gkv_ref, wuq_ref, wukv_ref,
                      c32_ref, s32_ref, c64_ref, s64_ref,
                      qa_ref, kv_ref, kr_ref, qb_ref, kb_ref, va_ref, qi_ref, ki_ref, wi_ref):
    xb = x_ref[...].astype(BF16)
    lane = _lane_iota()
    sec32 = (lane >= D_NOPE + D_ROPE // 2) & (lane < D_NOPE + D_ROPE)
    sec64 = (lane % D_DSA) >= D_DSA // 2
    c32, s32, c64, s64 = c32_ref[...], s32_ref[...], c64_ref[...], s64_ref[...]

    def seg(a, b):
        return jnp.dot(xb, w1_ref[:, a:b], preferred_element_type=F32)

    def rope64_store(h, out_ref):
        for j in range(h.shape[1] // LANES):
            blk = h[:, j * LANES:(j + 1) * LANES]
            out_ref[:, j * LANES:(j + 1) * LANES] = _rope_block(
                blk, c64, s64, D_DSA // 2, sec64).astype(out_ref.dtype)

    cq = _rms(seg(_E_CQ, _E_CKV), gq_ref[...]).astype(BF16)
    q = jnp.dot(cq, wuq_ref[...], preferred_element_type=F32)
    for h in range(H_MLA):
        blk = q[:, h * LANES:(h + 1) * LANES]
        qa_ref[:, h * LANES:(h + 1) * LANES] = _rope_block(
            blk, c32, s32, D_ROPE // 2, sec32).astype(BF16)
    ckv = _rms(seg(_E_CKV, _E_KR), gkv_ref[...]).astype(BF16)
    kv_ref[...] = jnp.dot(ckv, wukv_ref[...], preferred_element_type=F32).astype(BF16)
    kr_ref[...] = _rope_block(seg(_E_KR, _E_QB), c32, s32, D_ROPE // 2, sec32).astype(BF16)
    rope64_store(seg(_E_QB, _E_KB), qb_ref)
    rope64_store(seg(_E_KB, _E_VB), kb_ref)
    va_ref[...] = jnp.where(lane < D_DSA, seg(_E_VB, _E_QI), 1.0).astype(BF16)
    rope64_store(seg(_E_QI, _E_KI), qi_ref)
    rope64_store(seg(_E_KI, _E_WI), ki_ref)
    wi_ref[...] = seg(_E_WI, _E_END)


def _even_weights(w_in, w_uq, w_ukv):
    o = [0]
    for s_ in EV_SPLITS:
        o.append(o[-1] + s_)
    cq, ckv, kr, qb, kb, vb, qi, ki, wi = (w_in[:, o[i]:o[i + 1]] for i in range(9))
    z = lambda n: jnp.zeros((D_MODEL, n), w_in.dtype)
    w1 = jnp.concatenate([
        cq, ckv,
        z(D_NOPE), kr, z(LANES - D_NOPE - D_ROPE),
        qb,
        kb, kb,
        vb, z(LANES - D_DSA),
        qi,
        ki, ki,
        wi, z(LANES - H_IDX)], axis=1).astype(BF16)
    wq = w_uq.reshape(Q_LORA, H_MLA, D_NOPE + D_ROPE)
    wq = jnp.pad(wq, ((0, 0), (0, 0), (0, LANES - D_NOPE - D_ROPE))).reshape(Q_LORA, H_MLA * LANES)
    return w1, wq.astype(BF16), w_ukv.astype(BF16)


def _even_proj(x2, w1, gq, gkv, wuq, wukv, tabs):
    T = x2.shape[0]
    tm = min(TM_PROJ, T)
    row = lambda n: pl.BlockSpec((tm, n), lambda i: (i, 0))
    outs = [(H_MLA * LANES, BF16), (H_MLA * LANES, BF16), (LANES, BF16), (H_DSA * D_DSA, BF16),
            (LANES, BF16), (LANES, BF16), (H_IDX * D_IDX, BF16), (LANES, BF16), (LANES, F32)]
    return pl.pallas_call(
        _even_proj_kernel,
        grid=(T // tm,),
        in_specs=[row(D_MODEL), _full_spec(w1.shape), _full_spec(gq.shape), _full_spec(gkv.shape),
                  _full_spec(wuq.shape), _full_spec(wukv.shape)] + [row(LANES)] * 4,
        out_specs=[row(n) for n, _ in outs],
        out_shape=[jax.ShapeDtypeStruct((T, n), d) for n, d in outs],
        compiler_params=_params("parallel"),
        name="even_proj",
    )(x2, w1, gq, gkv, wuq, wukv, *tabs)


def _odd_proj_kernel(x_ref, w_ref, bf_ref, q_ref, k_ref, v_ref, lf_ref):
    xb = x_ref[...].astype(BF16)
    n = H_FOX * D_FOX
    for j, out in enumerate((q_ref, k_ref, v_ref)):
        out[...] = jnp.dot(xb, w_ref[:, j * n:(j + 1) * n], preferred_element_type=F32).astype(BF16)
    z = jnp.dot(xb, w_ref[:, 3 * n:], preferred_element_type=F32) + bf_ref[...]
    lf_ref[...] = jnp.minimum(z, 0.0) - jnp.log1p(jnp.exp(-jnp.abs(z)))


def _odd_proj(x2, w, bf):
    T = x2.shape[0]
    tm = min(TM_PROJ, T)
    n = H_FOX * D_FOX
    row = lambda c: pl.BlockSpec((tm, c), lambda i: (i, 0))
    return pl.pallas_call(
        _odd_proj_kernel,
        grid=(T // tm,),
        in_specs=[row(D_MODEL), _full_spec(w.shape), _full_spec(bf.shape)],
        out_specs=[row(n), row(n), row(n), row(LANES)],
        out_shape=[jax.ShapeDtypeStruct((T, n), BF16)] * 3 + [jax.ShapeDtypeStruct((T, LANES), F32)],
        compiler_params=_params("parallel"),
        name="odd_proj",
    )(x2, w, bf)


def _cumsum_kernel(lf_ref, c_ref):
    a = lf_ref[...]
    rows = lax.broadcasted_iota(jnp.int32, a.shape, 0)
    k = 1
    while k < a.shape[0]:
        a = a + jnp.where(rows >= k, pltpu.roll(a, k, 0), 0.0)
        k *= 2
    c_ref[...] = a


def _cumsum_seq(lf, S):
    T = lf.shape[0]
    blk = pl.BlockSpec((S, LANES), lambda b: (b, 0))
    return pl.pallas_call(
        _cumsum_kernel, grid=(T // S,), in_specs=[blk], out_specs=blk,
        out_shape=jax.ShapeDtypeStruct((T, LANES), F32),
        compiler_params=_params("parallel"), name="forget_cumsum",
    )(lf)


def _flash_kernel(*refs, fox, scale, t):
    if fox:
        q_ref, k_ref, v_ref, cc_ref, cr_ref, o_ref, m_ref, l_ref, acc_ref = refs
    else:
        q_ref, kv_ref, kr_ref, o_ref, m_ref, l_ref, acc_ref = refs
    hp, qi, ki = pl.program_id(1), pl.program_id(2), pl.program_id(3)
    lane = _lane_iota()
    lo = lane < D_FOX

    @pl.when(ki == 0)
    def _():
        m_ref[...] = jnp.full(m_ref.shape, -jnp.inf, F32)
        l_ref[...] = jnp.zeros(l_ref.shape, F32)
        acc_ref[...] = jnp.zeros(acc_ref.shape, F32)

    @pl.when(ki <= qi)
    def _():
        q_idx = qi * t + lax.broadcasted_iota(jnp.int32, (t, 1), 0)
        k_idx = ki * t + lax.broadcasted_iota(jnp.int32, (1, t), 1)
        if fox:
            mask = q_idx >= k_idx
            q2, k2, v2 = q_ref[...], k_ref[...], v_ref[...]
            cc = cc_ref[...]
        else:
            mask = (q_idx >> CHUNK_SHIFT) >= (k_idx >> CHUNK_SHIFT)
            kr = kr_ref[...]
        for hh in range(2):
            if fox:
                qh = jnp.where(lo if hh == 0 else lane >= D_FOX, q2, jnp.zeros_like(q2))
                kh, vh = k2, v2
            else:
                qh = q_ref[:, hh * LANES:(hh + 1) * LANES]
                vh = kv_ref[:, hh * LANES:(hh + 1) * LANES]
                kh = jnp.where(lo, vh, kr)
            s = lax.dot_general(qh, kh, _NT, preferred_element_type=F32) * scale
            if fox:
                head = 2 * hp + hh
                cq = jnp.sum(jnp.where(lane == head, cc, 0.0), axis=-1, keepdims=True)
                ck = cr_ref[0, pl.ds(head, 1), :]
                s = s + cq - ck
            s = jnp.where(mask, s, NEG_INF)
            m_prev = m_ref[hh]
            m_new = jnp.maximum(m_prev, jnp.max(s, axis=-1, keepdims=True))
            alpha = jnp.exp(m_prev - m_new)
            p = jnp.exp(s - m_new)
            l_ref[hh] = alpha * l_ref[hh] + jnp.sum(p, axis=-1, keepdims=True)
            acc_ref[hh] = alpha * acc_ref[hh] + jnp.dot(p.astype(BF16), vh, preferred_element_type=F32)
            m_ref[hh] = m_new

    @pl.when(ki == pl.num_programs(3) - 1)
    def _():
        o0 = acc_ref[0] / l_ref[0]
        o1 = acc_ref[1] / l_ref[1]
        if fox:
            o_ref[...] = jnp.where(lo, o0, o1).astype(BF16)
        else:
            o_ref[...] = jnp.where(lo, pltpu.roll(o0, D_V_MLA, 1), o1).astype(BF16)


def _flash(args, *, fox, B, S, scale):
    t = min(T_ATT, S)
    n = S // t
    npairs = (H_FOX if fox else H_MLA) // 2
    qrow = lambda b, hp, qi, ki: (b * n + qi, hp)
    krow = lambda b, hp, qi, ki: (b * n + jnp.minimum(ki, qi), hp)
    if fox:
        in_specs = [pl.BlockSpec((t, LANES), qrow), pl.BlockSpec((t, LANES), krow),
                    pl.BlockSpec((t, LANES), krow),
                    pl.BlockSpec((t, LANES), lambda b, hp, qi, ki: (b * n + qi, 0)),
                    pl.BlockSpec((1, H_FOX, t), lambda b, hp, qi, ki: (b, 0, jnp.minimum(ki, qi)))]
    else:
        in_specs = [pl.BlockSpec((t, 2 * LANES), qrow), pl.BlockSpec((t, 2 * LANES), krow),
                    pl.BlockSpec((t, LANES), lambda b, hp, qi, ki: (b * n + jnp.minimum(ki, qi), 0))]
    return pl.pallas_call(
        functools.partial(_flash_kernel, fox=fox, scale=scale, t=t),
        grid=(B, npairs, n, n),
        in_specs=in_specs,
        out_specs=pl.BlockSpec((t, LANES), qrow),
        out_shape=jax.ShapeDtypeStruct((B * S, npairs * LANES), BF16),
        scratch_shapes=[pltpu.VMEM((2, t, 1), F32), pltpu.VMEM((2, t, 1), F32),
                        pltpu.VMEM((2, t, LANES), F32)],
        compiler_params=_params("parallel", "parallel", "parallel", "arbitrary"),
        name="fox_attention" if fox else "mla_attention",
    )(*args)


def _dsa_kernel(qi_ref, qb_ref, wit_ref, ki_ref, kb_ref, va_ref, o_ref,
                qim_ref, qbm_ref, sc_ref, sb_ref, oacc_ref, *, ksel, scale):
    i = pl.program_id(1)
    nkc = (i * Q_DSA + Q_DSA + KC_DSA - 1) // KC_DSA
    lane = _lane_iota()
    lo = lane < D_DSA
    hi = lane >= D_DSA
    q_chunk = (i * Q_DSA + lane) >> CHUNK_SHIFT
    int_min = jnp.iinfo(jnp.int32).min

    for h in range(H_IDX):
        blk = qi_ref[:, (h // 2) * LANES:(h // 2 + 1) * LANES]
        qim_ref[h] = jnp.where(lo if h % 2 == 0 else hi, blk, jnp.zeros_like(blk))
    for h in range(H_DSA):
        blk = qb_ref[:, (h // 2) * LANES:(h // 2 + 1) * LANES]
        qbm_ref[h] = jnp.where(lo if h % 2 == 0 else hi, blk, jnp.zeros_like(blk))

    def rows(c):
        return pl.ds(pl.multiple_of(c * KC_DSA, KC_DSA), KC_DSA)

    def key_idx(c):
        return c * KC_DSA + lax.broadcasted_iota(jnp.int32, (KC_DSA, 1), 0)

    def admissible(c):
        return (key_idx(c) >> CHUNK_SHIFT) <= q_chunk

    def fold(x):
        return jnp.sum(x.reshape(KC_DSA // SUBLANES, SUBLANES, LANES), axis=0)

    def score_body(c, carry):
        kt = ki_ref[rows(c), :]
        acc = jnp.zeros((KC_DSA, LANES), F32)
        for h in range(H_IDX):
            r = lax.dot_general(kt, qim_ref[h], _NT, preferred_element_type=F32)
            acc = acc + jnp.maximum(r, 0.0) * wit_ref[h:h + 1, :]
        sc_ref[rows(c), :] = jnp.where(admissible(c), acc, NEG_INF)
        return carry

    lax.fori_loop(0, nkc, score_body, 0)

    def count(pred):
        def body(c, cnt):
            return cnt + fold(pred(c, sc_ref[rows(c), :]).astype(F32))
        cnt = lax.fori_loop(0, nkc, body, jnp.zeros((SUBLANES, LANES), F32))
        return jnp.sum(cnt, axis=0, keepdims=True)

    def as_float(key):
        return lax.bitcast_convert_type(jnp.where(key >= 0, key, key ^ 0x7FFFFFFF), F32)

    @pl.when((i + 1) * Q_DSA <= ksel)
    def _():
        def body(c, carry):
            sc_ref[rows(c), :] = jnp.where(admissible(c), 0.0, NEG_INF)
            return carry
        lax.fori_loop(0, nkc, body, 0)

    @pl.when((i + 1) * Q_DSA > ksel)
    def _():
        def bit_body(r, ans):
            cand = ans + lax.shift_left(jnp.int32(1), 31 - r)
            thr = as_float(cand)
            cnt = count(lambda c, s: s >= thr)
            return jnp.where(cnt >= ksel, cand, ans)
        ans = lax.fori_loop(0, 32, bit_body, jnp.full((1, LANES), int_min, jnp.int32))
        t = as_float(ans)
        need = ksel - count(lambda c, s: s > t)
        n_ge = count(lambda c, s: s >= t)

        def tie_bound():
            def bit_body2(r, aj):
                cand = aj + lax.shift_left(jnp.int32(1), 11 - r)
                g = count(lambda c, s: (s == t) & (key_idx(c) < cand))
                return jnp.where(g <= need, cand, aj)
            return lax.fori_loop(0, 12, bit_body2, jnp.zeros((1, LANES), jnp.int32))

        jmax = lax.cond(jnp.max(n_ge) > ksel, tie_bound,
                        lambda: jnp.full((1, LANES), 4095, jnp.int32))

        def body(c, carry):
            s = sc_ref[rows(c), :]
            sel = (s > t) | ((s == t) & (key_idx(c) < jmax))
            sc_ref[rows(c), :] = jnp.where(sel & admissible(c), 0.0, NEG_INF)
            return carry
        lax.fori_loop(0, nkc, body, 0)

    def head_body(h, carry):
        qm = qbm_ref[h]

        def pass1(c, m):
            s = lax.dot_general(kb_ref[rows(c), :], qm, _NT, preferred_element_type=F32) * scale
            s = s + sc_ref[rows(c), :]
            sb_ref[rows(c), :] = s
            return jnp.maximum(m, jnp.max(s.reshape(KC_DSA // SUBLANES, SUBLANES, LANES), axis=0))
        m8 = lax.fori_loop(0, nkc, pass1, jnp.full((SUBLANES, LANES), -jnp.inf, F32))
        m = jnp.max(m8, axis=0, keepdims=True)

        def pass2(c, acc):
            p = jnp.exp(sb_ref[rows(c), :] - m).astype(BF16)
            return acc + lax.dot_general(p, va_ref[rows(c), :], _TN, preferred_element_type=F32)
        oacc_ref[h] = lax.fori_loop(0, nkc, pass2, jnp.zeros((Q_DSA, LANES), F32))
        return carry

    lax.fori_loop(0, H_DSA, head_body, 0)

    for hp in range(H_DSA // 2):
        a, b = oacc_ref[2 * hp], oacc_ref[2 * hp + 1]
        ra = a / pltpu.roll(a, D_DSA, 1)
        rb = b / pltpu.roll(b, D_DSA, 1)
        o_ref[:, hp * LANES:(hp + 1) * LANES] = jnp.where(lo, ra, pltpu.roll(rb, D_DSA, 1)).astype(BF16)


def _dsa(qi, qb, wit, ki2, kb2, va, *, B, S, ksel):
    nb = S // Q_DSA
    qrow = lambda n: pl.BlockSpec((Q_DSA, n), lambda b, i: (b * nb + i, 0))
    seq = pl.BlockSpec((S, LANES), lambda b, i: (b, 0))
    return pl.pallas_call(
        functools.partial(_dsa_kernel, ksel=ksel, scale=D_DSA ** -0.5),
        grid=(B, nb),
        in_specs=[qrow(H_IDX * D_IDX), qrow(H_DSA * D_DSA),
                  pl.BlockSpec((SUBLANES, Q_DSA), lambda b, i: (0, b * nb + i)), seq, seq, seq],
        out_specs=qrow(H_DSA * D_DSA),
        out_shape=jax.ShapeDtypeStruct((B * S, H_DSA * D_DSA), BF16),
        scratch_shapes=[pltpu.VMEM((H_IDX, Q_DSA, LANES), BF16), pltpu.VMEM((H_DSA, Q_DSA, LANES), BF16),
                        pltpu.VMEM((S, LANES), F32), pltpu.VMEM((S, LANES), F32),
                        pltpu.VMEM((H_DSA, Q_DSA, LANES), F32)],
        compiler_params=_params("parallel", "arbitrary"),
        name="dsa_attention",
    )(qi, qb, wit, ki2, kb2, va)


def _out_ln_kernel(*refs, n_in):
    o_refs, w_refs = refs[:n_in], refs[n_in:2 * n_in]
    x_ref, g_ref, b_ref, out_ref = refs[2 * n_in:]
    y = jnp.dot(o_refs[0][...], w_refs[0][...], preferred_element_type=F32)
    for o_r, w_r in zip(o_refs[1:], w_refs[1:]):
        y = y + jnp.dot(o_r[...], w_r[...], preferred_element_type=F32)
    out_ref[...] = _layer_norm(DN_ALPHA * x_ref[...] + y, g_ref[...], b_ref[...])


def _out_ln(os_, ws, x2, g, b):
    T = x2.shape[0]
    tm = min(TM_PROJ, T)
    row = lambda c: pl.BlockSpec((tm, c), lambda i: (i, 0))
    return pl.pallas_call(
        functools.partial(_out_ln_kernel, n_in=len(os_)),
        grid=(T // tm,),
        in_specs=[row(o.shape[1]) for o in os_] + [_full_spec(w.shape) for w in ws]
                 + [row(D_MODEL), _full_spec(g.shape), _full_spec(b.shape)],
        out_specs=row(D_MODEL),
        out_shape=jax.ShapeDtypeStruct((T, D_MODEL), F32),
        compiler_params=_params("parallel"),
        name="out_proj_ln",
    )(*os_, *ws, x2, g, b)


_R_SUB = N_GROUPS


def _router_gates(xb, wr, br):
    lg = jnp.dot(xb, wr, preferred_element_type=F32) + br
    lane = _lane_iota()
    lanef = lane.astype(F32)
    far = float(LANES)
    is_grp = lane < N_GROUPS
    grp = jnp.where(is_grp, lg, -jnp.inf)
    mg = jnp.max(grp, axis=-1, keepdims=True)
    g_star = jnp.min(jnp.where(grp == mg, lanef, far), axis=-1, keepdims=True)
    p_top = 1.0 / jnp.sum(jnp.where(is_grp, jnp.exp(lg - mg), 0.0), axis=-1, keepdims=True)
    lane_grp = ((lane - _R_SUB) >> 2).astype(F32)
    in_grp = (lane_grp == g_star) & (lane >= _R_SUB) & (lane < _R_SUB + N_EXPERTS)
    l1 = jnp.where(in_grp, lg, -jnp.inf)
    v1 = jnp.max(l1, axis=-1, keepdims=True)
    j1 = jnp.min(jnp.where(l1 == v1, lanef, far), axis=-1, keepdims=True)
    l2 = jnp.where(in_grp & (lanef != j1), lg, -jnp.inf)
    v2 = jnp.max(l2, axis=-1, keepdims=True)
    j2 = jnp.min(jnp.where(l2 == v2, lanef, far), axis=-1, keepdims=True)
    e = jnp.exp(v2 - v1)
    w1 = 1.0 / (1.0 + e)
    w2 = e / (1.0 + e)
    return jnp.where(lanef == j1, w1 * p_top, jnp.where(lanef == j2, w2 * p_top, 0.0))


def _moe_kernel(x_ref, wr_ref, br_ref, wg_ref, wu_ref, wd_ref, g_ref, b_ref, out_ref,
                gates_ref, acc_ref):
    e = pl.program_id(1)
    xb = x_ref[...].astype(BF16)

    @pl.when(e == 0)
    def _():
        gates_ref[...] = _router_gates(xb, wr_ref[...], br_ref[...])
        acc_ref[...] = jnp.zeros(acc_ref.shape, F32)

    gate = jnp.sum(jnp.where(_lane_iota() == e + _R_SUB, gates_ref[...], 0.0), axis=-1, keepdims=True)
    a = jnp.dot(xb, wg_ref[0], preferred_element_type=F32)
    u = jnp.dot(xb, wu_ref[0], preferred_element_type=F32)
    h = (a * (1.0 / (1.0 + jnp.exp(-a))) * u).astype(BF16)
    acc_ref[...] += gate * jnp.dot(h, wd_ref[0], preferred_element_type=F32)

    @pl.when(e == N_EXPERTS - 1)
    def _():
        out_ref[...] = _layer_norm(DN_ALPHA * x_ref[...] + acc_ref[...], g_ref[...], b_ref[...])


def _moe_ln(x2, wr, br, wg, wu, wd, g, b):
    T = x2.shape[0]
    tm = min(TM_MOE, T)
    row = pl.BlockSpec((tm, D_MODEL), lambda i, e: (i, 0))
    return pl.pallas_call(
        _moe_kernel,
        grid=(T // tm, N_EXPERTS),
        in_specs=[row, _full_spec(wr.shape), _full_spec(br.shape),
                  pl.BlockSpec((1, D_MODEL, D_EXPERT), lambda i, e: (e, 0, 0)),
                  pl.BlockSpec((1, D_MODEL, D_EXPERT), lambda i, e: (e, 0, 0)),
                  pl.BlockSpec((1, D_EXPERT, D_MODEL), lambda i, e: (e, 0, 0)),
                  _full_spec(g.shape), _full_spec(b.shape)],
        out_specs=row,
        out_shape=jax.ShapeDtypeStruct((T, D_MODEL), F32),
        scratch_shapes=[pltpu.VMEM((tm, LANES), F32), pltpu.VMEM((tm, D_MODEL), F32)],
        compiler_params=_params("parallel", "arbitrary"),
        name="moe_ln",
    )(x2, wr, br, wg, wu, wd, g, b)


def _router_weights(w_grp, b_grp, w_sub, b_sub):
    pad = LANES - N_GROUPS - N_EXPERTS
    wr = jnp.concatenate([w_grp, w_sub, jnp.zeros((D_MODEL, pad), w_grp.dtype)], axis=1).astype(BF16)
    br = jnp.concatenate([b_grp, b_sub, jnp.zeros((pad,), b_grp.dtype)])[None].astype(F32)
    return wr, br


def kernel(x, positions, ev_w_in, ev_g_q, ev_g_kv, ev_w_uq, ev_w_ukv, ev_w_o, od_w_in, od_b_f, od_w_o,
           moe_w_grp, moe_b_grp, moe_w_sub, moe_b_sub, moe_w_gate, moe_w_up, moe_w_down,
           ln1_g, ln1_b, ln2_g, ln2_b):
    B, S, D = x.shape
    T = B * S
    ksel = min(TOPK_MAX, S // 4)
    x2 = x.reshape(T, D)
    tabs = _rope_tables(positions.reshape(T, 1).astype(F32))
    row = lambda v: v[None].astype(F32)
    for layer in range(DEPTH):
        j = layer // 2
        if layer % 2 == 0:
            w1, wuq, wukv = _even_weights(ev_w_in[j], ev_w_uq[j], ev_w_ukv[j])
            qa, kv, kr, qb, kb2, va, qi, ki2, wi = _even_proj(
                x2, w1, row(ev_g_q[j]), row(ev_g_kv[j]), wuq, wukv, tabs)
            o_a = _flash((qa, kv, kr), fox=False, B=B, S=S, scale=(D_NOPE + D_ROPE) ** -0.5)
            wit = jnp.pad(wi[:, :H_IDX].T, ((0, SUBLANES - H_IDX), (0, 0)))
            o_b = _dsa(qi, qb, wit, ki2, kb2, va, B=B, S=S, ksel=ksel)
            wo = ev_w_o[j].astype(BF16)
            n_a = H_MLA * D_V_MLA
            x2 = _out_ln((o_a, o_b), (wo[:n_a], wo[n_a:]), x2, row(ln1_g[layer]), row(ln1_b[layer]))
        else:
            n = H_FOX * D_FOX
            w = jnp.pad(od_w_in[j], ((0, 0), (0, LANES - H_FOX))).astype(BF16)
            bf = jnp.pad(od_b_f[j], (0, LANES - H_FOX))[None].astype(F32)
            q, k, v, lf = _odd_proj(x2, w, bf)
            c = _cumsum_seq(lf, S)
            c_rows = c.reshape(B, S, LANES)[:, :, :H_FOX].swapaxes(1, 2)
            o = _flash((q, k, v, c, c_rows), fox=True, B=B, S=S, scale=D_FOX ** -0.5)
            x2 = _out_ln((o,), (od_w_o[j].astype(BF16),), x2, row(ln1_g[layer]), row(ln1_b[layer]))
        wr, br = _router_weights(moe_w_grp[layer], moe_b_grp[layer], moe_w_sub[layer], moe_b_sub[layer])
        x2 = _moe_ln(x2, wr, br, moe_w_gate[layer].astype(BF16), moe_w_up[layer].astype(BF16),
                     moe_w_down[layer].astype(BF16), row(ln2_g[layer]), row(ln2_b[layer]))
    return x2.reshape(B, S, D)
```

```python
import functools

import jax
import jax.numpy as jnp
from jax import lax
from jax.experimental import pallas as pl
from jax.experimental.pallas import tpu as pltpu

D_MODEL = 1024
DEPTH = 4
CHUNK = 64
CHUNK_SHIFT = 6
ROPE_THETA = 10000.0
NEG_INF = -1e30
LN_EPS = 1e-5
RMS_EPS = 1e-6
H_MLA, D_NOPE, D_ROPE, D_V_MLA, Q_LORA, KV_LORA = 8, 64, 32, 64, 384, 256
H_DSA, D_DSA, H_IDX, D_IDX, TOPK_MAX = 8, 64, 4, 64, 256
H_FOX, D_FOX = 16, 64
N_GROUPS, EXPERTS_PER_GROUP, D_EXPERT = 4, 4, 512
N_EXPERTS = N_GROUPS * EXPERTS_PER_GROUP
DN_ALPHA = (2 * DEPTH) ** 0.25
EV_SPLITS = (Q_LORA, KV_LORA, D_ROPE, H_DSA * D_DSA, D_DSA, D_DSA, H_IDX * D_IDX, D_IDX, H_IDX)

LANES = 128
SUBLANES = 8
VMEM_LIMIT_BYTES = 56 * 1024 * 1024

TM_PROJ = 512
TM_MOE = 512
T_ATT = 256
Q_DSA = 128
KC_DSA = 256

F32 = jnp.float32
BF16 = jnp.bfloat16
_NT = (((1,), (1,)), ((), ()))
_TN = (((0,), (0,)), ((), ()))


def _params(*sem):
    return pltpu.CompilerParams(dimension_semantics=sem, vmem_limit_bytes=VMEM_LIMIT_BYTES)


def _lane_iota(shape=(1, LANES)):
    return lax.broadcasted_iota(jnp.int32, shape, len(shape) - 1)


def _full_spec(shape):
    return pl.BlockSpec(shape, lambda *_: (0,) * len(shape))


def _rope_table_kernel(pos_ref, inv32_ref, sg32_ref, inv64_ref, sg64_ref,
                       c32_ref, s32_ref, c64_ref, s64_ref):
    pos = pos_ref[...]
    a32 = pos * inv32_ref[...]
    a64 = pos * inv64_ref[...]
    c32_ref[...] = jnp.cos(a32)
    s32_ref[...] = jnp.sin(a32) * sg32_ref[...]
    c64_ref[...] = jnp.cos(a64)
    s64_ref[...] = jnp.sin(a64) * sg64_ref[...]


def _rope_tables(posf):
    T = posf.shape[0]
    tm = min(T, 2048)
    lane = jnp.arange(LANES)
    inv16 = jnp.power(ROPE_THETA, -jnp.arange(0, D_ROPE, 2, dtype=F32) / D_ROPE)
    inv32 = jnp.power(ROPE_THETA, -jnp.arange(0, D_DSA, 2, dtype=F32) / D_DSA)
    in32 = (lane >= D_NOPE) & (lane < D_NOPE + D_ROPE)
    t32 = jnp.where(in32, inv16[(lane - D_NOPE) % (D_ROPE // 2)], 0.0).astype(F32)[None]
    g32 = jnp.where(in32, jnp.where(lane < D_NOPE + D_ROPE // 2, -1.0, 1.0), 0.0).astype(F32)[None]
    t64 = inv32[lane % (D_DSA // 2)].astype(F32)[None]
    g64 = jnp.where((lane % D_DSA) < D_DSA // 2, -1.0, 1.0).astype(F32)[None]
    row = pl.BlockSpec((tm, LANES), lambda i: (i, 0))
    tab = jax.ShapeDtypeStruct((T, LANES), F32)
    return pl.pallas_call(
        _rope_table_kernel,
        grid=(T // tm,),
        in_specs=[pl.BlockSpec((tm, 1), lambda i: (i, 0))] + [_full_spec((1, LANES))] * 4,
        out_specs=[row] * 4,
        out_shape=[tab] * 4,
        compiler_params=_params("parallel"),
        name="rope_tables",
    )(posf, t32, g32, t64, g64)


def _rope_block(y, c, s, half, second_half):
    partner = jnp.where(second_half, pltpu.roll(y, half, 1), pltpu.roll(y, LANES - half, 1))
    return y * c + partner * s


def _rms(x, g):
    return x * lax.rsqrt(jnp.mean(x * x, axis=-1, keepdims=True) + RMS_EPS) * g


def _layer_norm(z, g, b):
    mu = jnp.mean(z, axis=-1, keepdims=True)
    d = z - mu
    var = jnp.mean(d * d, axis=-1, keepdims=True)
    return d * lax.rsqrt(var + LN_EPS) * g + b


_E_CQ, _E_CKV, _E_KR, _E_QB, _E_KB, _E_VB, _E_QI, _E_KI, _E_WI, _E_END = (
    0, 384, 640, 768, 1280, 1408, 1536, 1792, 1920, 2048)


def _even_proj_kernel(x_ref, w1_ref, gq_ref, gkv_ref, wuq_ref, wukv_ref,
                      c32_ref, s32_ref, c64_ref, s64_ref,
                      qa_ref, kv_ref, kr_ref, qb_ref, kb_ref, va_ref, qi_ref, ki_ref, wi_ref):
    xb = x_ref[...].astype(BF16)
    lane = _lane_iota()
    sec32 = (lane >= D_NOPE + D_ROPE // 2) & (lane < D_NOPE + D_ROPE)
    sec64 = (lane % D_DSA) >= D_DSA // 2
    c32, s32, c64, s64 = c32_ref[...], s32_ref[...], c64_ref[...], s64_ref[...]

    def seg(a, b):
        return jnp.dot(xb, w1_ref[:, a:b], preferred_element_type=F32)

    def rope64_store(h, out_ref):
        for j in range(h.shape[1] // LANES):
            blk = h[:, j * LANES:(j + 1) * LANES]
            out_ref[:, j * LANES:(j + 1) * LANES] = _rope_block(
                blk, c64, s64, D_DSA // 2, sec64).astype(out_ref.dtype)

    cq = _rms(seg(_E_CQ, _E_CKV), gq_ref[...]).astype(BF16)
    q = jnp.dot(cq, wuq_ref[...], preferred_element_type=F32)
    for h in range(H_MLA):
        blk = q[:, h * LANES:(h + 1) * LANES]
        qa_ref[:, h * LANES:(h + 1) * LANES] = _rope_block(
            blk, c32, s32, D_ROPE // 2, sec32).astype(BF16)
    ckv = _rms(seg(_E_CKV, _E_KR), gkv_ref[...]).astype(BF16)
    kv_ref[...] = jnp.dot(ckv, wukv_ref[...], preferred_element_type=F32).astype(BF16)
    kr_ref[...] = _rope_block(seg(_E_KR, _E_QB), c32, s32, D_ROPE // 2, sec32).astype(BF16)
    rope64_store(seg(_E_QB, _E_KB), qb_ref)
    rope64_store(seg(_E_KB, _E_VB), kb_ref)
    va_ref[...] = jnp.where(lane < D_DSA, seg(_E_VB, _E_QI), 1.0).astype(BF16)
    rope64_store(seg(_E_QI, _E_KI), qi_ref)
    rope64_store(seg(_E_KI, _E_WI), ki_ref)
    wi_ref[...] = seg(_E_WI, _E_END)


def _even_weights(w_in, w_uq, w_ukv):
    o = [0]
    for s_ in EV_SPLITS:
        o.append(o[-1] + s_)
    cq, ckv, kr, qb, kb, vb, qi, ki, wi = (w_in[:, o[i]:o[i + 1]] for i in range(9))
    z = lambda n: jnp.zeros((D_MODEL, n), w_in.dtype)
    w1 = jnp.concatenate([
        cq, ckv,
        z(D_NOPE), kr, z(LANES - D_NOPE - D_ROPE),
        qb,
        kb, kb,
        vb, z(LANES - D_DSA),
        qi,
        ki, ki,
        wi, z(LANES - H_IDX)], axis=1).astype(BF16)
    wq = w_uq.reshape(Q_LORA, H_MLA, D_NOPE + D_ROPE)
    wq = jnp.pad(wq, ((0, 0), (0, 0), (0, LANES - D_NOPE - D_ROPE))).reshape(Q_LORA, H_MLA * LANES)
    return w1, wq.astype(BF16), w_ukv.astype(BF16)


def _even_proj(x2, w1, gq, gkv, wuq, wukv, tabs):
    T = x2.shape[0]
    tm = min(TM_PROJ, T)
    row = lambda n: pl.BlockSpec((tm, n), lambda i: (i, 0))
    outs = [(H_MLA * LANES, BF16), (H_MLA * LANES, BF16), (LANES, BF16), (H_DSA * D_DSA, BF16),
            (LANES, BF16), (LANES, BF16), (H_IDX * D_IDX, BF16), (LANES, BF16), (LANES, F32)]
    return pl.pallas_call(
        _even_proj_kernel,
        grid=(T // tm,),
        in_specs=[row(D_MODEL), _full_spec(w1.shape), _full_spec(gq.shape), _full_spec(gkv.shape),
                  _full_spec(wuq.shape), _full_spec(wukv.shape)] + [row(LANES)] * 4,
        out_specs=[row(n) for n, _ in outs],
        out_shape=[jax.ShapeDtypeStruct((T, n), d) for n, d in outs],
        compiler_params=_params("parallel"),
        name="even_proj",
    )(x2, w1, gq, gkv, wuq, wukv, *tabs)


def _odd_proj_kernel(x_ref, w_ref, bf_ref, q_ref, k_ref, v_ref, lf_ref):
    xb = x_ref[...].astype(BF16)
    n = H_FOX * D_FOX
    for j, out in enumerate((q_ref, k_ref, v_ref)):
        out[...] = jnp.dot(xb, w_ref[:, j * n:(j + 1) * n], preferred_element_type=F32).astype(BF16)
    z = jnp.dot(xb, w_ref[:, 3 * n:], preferred_element_type=F32) + bf_ref[...]
    lf_ref[...] = jnp.minimum(z, 0.0) - jnp.log1p(jnp.exp(-jnp.abs(z)))


def _odd_proj(x2, w, bf):
    T = x2.shape[0]
    tm = min(TM_PROJ, T)
    n = H_FOX * D_FOX
    row = lambda c: pl.BlockSpec((tm, c), lambda i: (i, 0))
    return pl.pallas_call(
        _odd_proj_kernel,
        grid=(T // tm,),
        in_specs=[row(D_MODEL), _full_spec(w.shape), _full_spec(bf.shape)],
        out_specs=[row(n), row(n), row(n), row(LANES)],
        out_shape=[jax.ShapeDtypeStruct((T, n), BF16)] * 3 + [jax.ShapeDtypeStruct((T, LANES), F32)],
        compiler_params=_params("parallel"),
        name="odd_proj",
    )(x2, w, bf)


def _cumsum_kernel(lf_ref, c_ref):
    a = lf_ref[...]
    rows = lax.broadcasted_iota(jnp.int32, a.shape, 0)
    k = 1
    while k < a.shape[0]:
        a = a + jnp.where(rows >= k, pltpu.roll(a, k, 0), 0.0)
        k *= 2
    c_ref[...] = a


def _cumsum_seq(lf, S):
    T = lf.shape[0]
    blk = pl.BlockSpec((S, LANES), lambda b: (b, 0))
    return pl.pallas_call(
        _cumsum_kernel, grid=(T // S,), in_specs=[blk], out_specs=blk,
        out_shape=jax.ShapeDtypeStruct((T, LANES), F32),
        compiler_params=_params("parallel"), name="forget_cumsum",
    )(lf)


def _flash_kernel(*refs, fox, scale, t, n_tiles, S):
    if fox:
        q_ref, k_ref, v_ref, cc_ref, cr_ref, o_ref, kx_ref, sb_ref = refs
    else:
        q_ref, kv_ref, kr_ref, o_ref, kx_ref, sb_ref = refs
    hp, qi = pl.program_id(1), pl.program_id(2)
    lane = _lane_iota()
    lo = lane < D_FOX
    hi = lane >= D_FOX
    exp2_scale = scale * 1.4426950408889634

    @pl.when(qi == 0)
    def _():
        for hh in range(2):
            if fox:
                col = jnp.sum(jnp.where(lane == 2 * hp + hh, cc_ref[0], 0.0), axis=-1, keepdims=True)
                kx_ref[hh] = jnp.broadcast_to(col, (S, LANES))
            else:
                kx_ref[hh] = jnp.where(lo, kv_ref[0, :, hh * LANES:(hh + 1) * LANES], kr_ref[0])

    def body(n):
        q_idx = n * t + lax.broadcasted_iota(jnp.int32, (1, t), 1)
        k_loc = lax.broadcasted_iota(jnp.int32, (t, 1), 0)
        outs = []
        for hh in range(2):
            if fox:
                q2 = q_ref[0] * jnp.asarray(scale, BF16)
                qh = jnp.where(lo if hh == 0 else hi, q2, jnp.zeros_like(q2))
                cq = cr_ref[0, pl.ds(2 * hp + hh, 1), n * t:(n + 1) * t]
            else:
                qh = q_ref[0, :, hh * LANES:(hh + 1) * LANES]
            m8 = jnp.full((SUBLANES, t), -jnp.inf, F32)
            for c in range(n + 1):
                rows = slice(c * t, (c + 1) * t)
                kh = k_ref[0, rows, :] if fox else kx_ref[hh, rows, :]
                s = lax.dot_general(kh, qh, _NT, preferred_element_type=F32)
                if fox:
                    s = s + cq - jnp.concatenate([kx_ref[hh, rows, :]] * (t // LANES), axis=1)
                if c == n:
                    k_idx = c * t + k_loc
                    vis = (k_idx <= q_idx) if fox else ((k_idx >> CHUNK_SHIFT) <= (q_idx >> CHUNK_SHIFT))
                    s = jnp.where(vis, s, NEG_INF)
                sb_ref[hh, rows, :] = s
                m8 = jnp.maximum(m8, jnp.max(s.reshape(t // SUBLANES, SUBLANES, t), axis=0))
            m = jnp.max(m8, axis=0, keepdims=True)
            acc = jnp.zeros((t, LANES), F32)
            l8 = jnp.zeros((SUBLANES, t), F32)
            for c in range(n + 1):
                rows = slice(c * t, (c + 1) * t)
                x = sb_ref[hh, rows, :] - m
                p = jnp.exp(x) if fox else jnp.exp2(x * exp2_scale)
                l8 = l8 + jnp.sum(p.reshape(t // SUBLANES, SUBLANES, t), axis=0)
                vh = v_ref[0, rows, :] if fox else kv_ref[0, rows, hh * LANES:(hh + 1) * LANES]
                acc = acc + lax.dot_general(p.astype(BF16), vh, _TN, preferred_element_type=F32)
            l = jnp.sum(l8, axis=0, keepdims=True)
            outs.append(acc / jnp.transpose(jnp.broadcast_to(l, (LANES, t))))
        if fox:
            o_ref[0] = jnp.where(lo, outs[0], outs[1]).astype(BF16)
        else:
            o_ref[0] = jnp.where(lo, pltpu.roll(outs[0], D_V_MLA, 1), outs[1]).astype(BF16)

    for n in range(n_tiles):
        pl.when(qi == n)(functools.partial(body, n))


def _flash(args, *, fox, B, S, scale):
    t = min(T_ATT, S)
    n = S // t
    npairs = (H_FOX if fox else H_MLA) // 2
    args = tuple(a.reshape(B, S, a.shape[-1]) if a.ndim == 2 else a for a in args)
    qblk = lambda b, hp, qi: (b, qi, hp)
    seq = lambda b, hp, qi: (b, 0, hp)
    whole = lambda b, hp, qi: (b, 0, 0)
    if fox:
        in_specs = [pl.BlockSpec((1, t, LANES), qblk), pl.BlockSpec((1, S, LANES), seq),
                    pl.BlockSpec((1, S, LANES), seq), pl.BlockSpec((1, S, LANES), whole),
                    pl.BlockSpec((1, H_FOX, S), whole)]
        kx = pltpu.VMEM((2, S, LANES), F32)
    else:
        in_specs = [pl.BlockSpec((1, t, 2 * LANES), qblk), pl.BlockSpec((1, S, 2 * LANES), seq),
                    pl.BlockSpec((1, S, LANES), whole)]
        kx = pltpu.VMEM((2, S, LANES), BF16)
    out = pl.pallas_call(
        functools.partial(_flash_kernel, fox=fox, scale=scale, t=t, n_tiles=n, S=S),
        grid=(B, npairs, n),
        in_specs=in_specs,
        out_specs=pl.BlockSpec((1, t, LANES), qblk),
        out_shape=jax.ShapeDtypeStruct((B, S, npairs * LANES), BF16),
        scratch_shapes=[kx, pltpu.VMEM((2, S, t), F32)],
        compiler_params=_params("parallel", "parallel", "arbitrary"),
        name="fox_attention" if fox else "mla_attention",
    )(*args)
    return out.reshape(B * S, npairs * LANES)


def _dsa_kernel(qi_ref, qb_ref, wit_ref, ki_ref, kb_ref, va_ref, o_ref,
                qim_ref, qbm_ref, sc_ref, sb_ref, *, nk, i0, ksel):
    i = i0 + pl.program_id(1)
    nc = nk // KC_DSA
    lane = _lane_iota()
    lo = lane < D_DSA
    hi = lane >= D_DSA
    q_chunk = (i * Q_DSA + lane) >> CHUNK_SHIFT
    int_min = jnp.iinfo(jnp.int32).min
    select = nk > ksel

    def chunk(c):
        return slice(c * KC_DSA, (c + 1) * KC_DSA)

    def admissible(c):
        k_idx = c * KC_DSA + lax.broadcasted_iota(jnp.int32, (KC_DSA, 1), 0)
        return (k_idx >> CHUNK_SHIFT) <= q_chunk

    for h in range(H_DSA):
        blk = qb_ref[:, (h // 2) * LANES:(h // 2 + 1) * LANES] * jnp.asarray(D_DSA ** -0.5, BF16)
        qbm_ref[h * Q_DSA:(h + 1) * Q_DSA, :] = jnp.where(lo if h % 2 == 0 else hi, blk, jnp.zeros_like(blk))

    if select:
        for h in range(H_IDX):
            blk = qi_ref[:, (h // 2) * LANES:(h // 2 + 1) * LANES]
            qim_ref[h * Q_DSA:(h + 1) * Q_DSA, :] = jnp.where(lo if h % 2 == 0 else hi, blk, jnp.zeros_like(blk))
        w_flat = jnp.concatenate([wit_ref[h:h + 1, :] for h in range(H_IDX)], axis=1)
        for c in range(nc):
            r = lax.dot_general(ki_ref[0, chunk(c), :], qim_ref[...], _NT, preferred_element_type=F32)
            r = jnp.maximum(r, 0.0) * w_flat
            acc = r[:, :LANES]
            for h in range(1, H_IDX):
                acc = acc + r[:, h * LANES:(h + 1) * LANES]
            sc_ref[chunk(c), :] = jnp.where(admissible(c), acc, NEG_INF)

        def count(mask):
            part = jnp.sum(mask.astype(F32).reshape(nk // 32, 32, LANES), axis=0)
            return jnp.sum(part, axis=0, keepdims=True)

        def as_float(key):
            return lax.bitcast_convert_type(jnp.where(key >= 0, key, key ^ 0x7FFFFFFF), F32)

        def bit_body(r, ans):
            cand = ans + lax.shift_left(jnp.int32(1), 31 - r)
            cnt = count(sc_ref[...] >= as_float(cand))
            return jnp.where(cnt >= ksel, cand, ans)
        ans = lax.fori_loop(0, 32, bit_body, jnp.full((1, LANES), int_min, jnp.int32))
        t = as_float(ans)
        s_all = sc_ref[...]
        need = ksel - count(s_all > t)
        n_ge = count(s_all >= t)
        k_all = lax.broadcasted_iota(jnp.int32, (nk, 1), 0)

        def tie_bound():
            def bit_body2(r, aj):
                cand = aj + lax.shift_left(jnp.int32(1), 11 - r)
                g = count((sc_ref[...] == t) & (k_all < cand))
                return jnp.where(g <= need, cand, aj)
            return lax.fori_loop(0, 12, bit_body2, jnp.zeros((1, LANES), jnp.int32))

        jmax = lax.cond(jnp.max(n_ge) > ksel, tie_bound,
                        lambda: jnp.full((1, LANES), 4095, jnp.int32))
        for c in range(nc):
            s = sc_ref[chunk(c), :]
            k_idx = c * KC_DSA + lax.broadcasted_iota(jnp.int32, (KC_DSA, 1), 0)
            sel = (s > t) | ((s == t) & (k_idx < jmax))
            sc_ref[chunk(c), :] = jnp.where(sel & admissible(c), 0.0, NEG_INF)
    else:
        for c in range(nc):
            sc_ref[chunk(c), :] = jnp.where(admissible(c), 0.0, NEG_INF)

    m8 = jnp.full((SUBLANES, H_DSA * LANES), -jnp.inf, F32)
    for c in range(nc):
        s = lax.dot_general(kb_ref[0, chunk(c), :], qbm_ref[...], _NT, preferred_element_type=F32)
        s = s + jnp.concatenate([sc_ref[chunk(c), :]] * H_DSA, axis=1)
        sb_ref[chunk(c), :] = s
        m8 = jnp.maximum(m8, jnp.max(s.reshape(KC_DSA // SUBLANES, SUBLANES, H_DSA * LANES), axis=0))
    m = jnp.max(m8, axis=0, keepdims=True)
    acc = jnp.zeros((H_DSA * Q_DSA, LANES), F32)
    for c in range(nc):
        p = jnp.exp(sb_ref[chunk(c), :] - m).astype(BF16)
        acc = acc + lax.dot_general(p, va_ref[0, chunk(c), :], _TN, preferred_element_type=F32)

    for hp in range(H_DSA // 2):
        a = acc[(2 * hp) * Q_DSA:(2 * hp + 1) * Q_DSA]
        b = acc[(2 * hp + 1) * Q_DSA:(2 * hp + 2) * Q_DSA]
        ra = a / pltpu.roll(a, D_DSA, 1)
        rb = b / pltpu.roll(b, D_DSA, 1)
        o_ref[:, hp * LANES:(hp + 1) * LANES] = jnp.where(lo, ra, pltpu.roll(rb, D_DSA, 1)).astype(BF16)


def _dsa(qi, qb, wit, ki2, kb2, va, *, B, S, ksel):
    nb = S // Q_DSA
    per = KC_DSA // Q_DSA
    seq3 = lambda a: a.reshape(B, S, LANES)
    ki3, kb3, va3 = seq3(ki2), seq3(kb2), seq3(va)
    outs = []
    for g in range(S // KC_DSA):
        nk, i0 = (g + 1) * KC_DSA, g * per
        qrow = lambda n, i0=i0: pl.BlockSpec((Q_DSA, n), lambda b, j: (b * nb + i0 + j, 0))
        seq = pl.BlockSpec((1, nk, LANES), lambda b, j: (b, 0, 0))
        o = pl.pallas_call(
            functools.partial(_dsa_kernel, nk=nk, i0=i0, ksel=ksel),
            grid=(B, per),
            in_specs=[qrow(H_IDX * D_IDX), qrow(H_DSA * D_DSA),
                      pl.BlockSpec((SUBLANES, Q_DSA), lambda b, j, i0=i0: (0, b * nb + i0 + j)),
                      seq, seq, seq],
            out_specs=pl.BlockSpec((Q_DSA, H_DSA * D_DSA), lambda b, j: (b * per + j, 0)),
            out_shape=jax.ShapeDtypeStruct((B * per * Q_DSA, H_DSA * D_DSA), BF16),
            scratch_shapes=[pltpu.VMEM((H_IDX * Q_DSA, LANES), BF16), pltpu.VMEM((H_DSA * Q_DSA, LANES), BF16),
                            pltpu.VMEM((nk, LANES), F32), pltpu.VMEM((nk, H_DSA * LANES), F32)],
            compiler_params=_params("parallel", "arbitrary"),
            name="dsa_attention_%d" % nk,
        )(qi, qb, wit, ki3, kb3, va3)
        outs.append(o.reshape(B, per * Q_DSA, H_DSA * D_DSA))
    return jnp.concatenate(outs, axis=1).reshape(B * S, H_DSA * D_DSA)


def _out_ln_kernel(*refs, n_in):
    o_refs, w_refs = refs[:n_in], refs[n_in:2 * n_in]
    x_ref, g_ref, b_ref, out_ref = refs[2 * n_in:]
    y = jnp.dot(o_refs[0][...], w_refs[0][...], preferred_element_type=F32)
    for o_r, w_r in zip(o_refs[1:], w_refs[1:]):
        y = y + jnp.dot(o_r[...], w_r[...], preferred_element_type=F32)
    out_ref[...] = _layer_norm(DN_ALPHA * x_ref[...] + y, g_ref[...], b_ref[...])


def _out_ln(os_, ws, x2, g, b):
    T = x2.shape[0]
    tm = min(TM_PROJ, T)
    row = lambda c: pl.BlockSpec((tm, c), lambda i: (i, 0))
    return pl.pallas_call(
        functools.partial(_out_ln_kernel, n_in=len(os_)),
        grid=(T // tm,),
        in_specs=[row(o.shape[1]) for o in os_] + [_full_spec(w.shape) for w in ws]
                 + [row(D_MODEL), _full_spec(g.shape), _full_spec(b.shape)],
        out_specs=row(D_MODEL),
        out_shape=jax.ShapeDtypeStruct((T, D_MODEL), F32),
        compiler_params=_params("parallel"),
        name="out_proj_ln",
    )(*os_, *ws, x2, g, b)


_R_SUB = N_GROUPS


def _router_gates(xb, wr, br):
    lg = jnp.dot(xb, wr, preferred_element_type=F32) + br
    lane = _lane_iota()
    lanef = lane.astype(F32)
    far = float(LANES)
    is_grp = lane < N_GROUPS
    grp = jnp.where(is_grp, lg, -jnp.inf)
    mg = jnp.max(grp, axis=-1, keepdims=True)
    g_star = jnp.min(jnp.where(grp == mg, lanef, far), axis=-1, keepdims=True)
    p_top = 1.0 / jnp.sum(jnp.where(is_grp, jnp.exp(lg - mg), 0.0), axis=-1, keepdims=True)
    lane_grp = ((lane - _R_SUB) >> 2).astype(F32)
    in_grp = (lane_grp == g_star) & (lane >= _R_SUB) & (lane < _R_SUB + N_EXPERTS)
    l1 = jnp.where(in_grp, lg, -jnp.inf)
    v1 = jnp.max(l1, axis=-1, keepdims=True)
    j1 = jnp.min(jnp.where(l1 == v1, lanef, far), axis=-1, keepdims=True)
    l2 = jnp.where(in_grp & (lanef != j1), lg, -jnp.inf)
    v2 = jnp.max(l2, axis=-1, keepdims=True)
    j2 = jnp.min(jnp.where(l2 == v2, lanef, far), axis=-1, keepdims=True)
    e = jnp.exp(v2 - v1)
    w1 = 1.0 / (1.0 + e)
    w2 = e / (1.0 + e)
    return jnp.where(lanef == j1, w1 * p_top, jnp.where(lanef == j2, w2 * p_top, 0.0))


def _moe_kernel(x_ref, wr_ref, br_ref, wg_ref, wu_ref, wd_ref, g_ref, b_ref, out_ref,
                gates_ref, acc_ref):
    e = pl.program_id(1)
    xb = x_ref[...].astype(BF16)

    @pl.when(e == 0)
    def _():
        gates_ref[...] = _router_gates(xb, wr_ref[...], br_ref[...])
        acc_ref[...] = jnp.zeros(acc_ref.shape, F32)

    gate = jnp.sum(jnp.where(_lane_iota() == e + _R_SUB, gates_ref[...], 0.0), axis=-1, keepdims=True)
    a = jnp.dot(xb, wg_ref[0], preferred_element_type=F32)
    u = jnp.dot(xb, wu_ref[0], preferred_element_type=F32)
    h = (a * (1.0 / (1.0 + jnp.exp(-a))) * u).astype(BF16)
    acc_ref[...] += gate * jnp.dot(h, wd_ref[0], preferred_element_type=F32)

    @pl.when(e == N_EXPERTS - 1)
    def _():
        out_ref[...] = _layer_norm(DN_ALPHA * x_ref[...] + acc_ref[...], g_ref[...], b_ref[...])


def _moe_ln(x2, wr, br, wg, wu, wd, g, b):
    T = x2.shape[0]
    tm = min(TM_MOE, T)
    row = pl.BlockSpec((tm, D_MODEL), lambda i, e: (i, 0))
    return pl.pallas_call(
        _moe_kernel,
        grid=(T // tm, N_EXPERTS),
        in_specs=[row, _full_spec(wr.shape), _full_spec(br.shape),
                  pl.BlockSpec((1, D_MODEL, D_EXPERT), lambda i, e: (e, 0, 0)),
                  pl.BlockSpec((1, D_MODEL, D_EXPERT), lambda i, e: (e, 0, 0)),
                  pl.BlockSpec((1, D_EXPERT, D_MODEL), lambda i, e: (e, 0, 0)),
                  _full_spec(g.shape), _full_spec(b.shape)],
        out_specs=row,
        out_shape=jax.ShapeDtypeStruct((T, D_MODEL), F32),
        scratch_shapes=[pltpu.VMEM((tm, LANES), F32), pltpu.VMEM((tm, D_MODEL), F32)],
        compiler_params=_params("parallel", "arbitrary"),
        name="moe_ln",
    )(x2, wr, br, wg, wu, wd, g, b)


def _router_weights(w_grp, b_grp, w_sub, b_sub):
    pad = LANES - N_GROUPS - N_EXPERTS
    wr = jnp.concatenate([w_grp, w_sub, jnp.zeros((D_MODEL, pad), w_grp.dtype)], axis=1).astype(BF16)
    br = jnp.concatenate([b_grp, b_sub, jnp.zeros((pad,), b_grp.dtype)])[None].astype(F32)
    return wr, br


def kernel(x, positions, ev_w_in, ev_g_q, ev_g_kv, ev_w_uq, ev_w_ukv, ev_w_o, od_w_in, od_b_f, od_w_o,
           moe_w_grp, moe_b_grp, moe_w_sub, moe_b_sub, moe_w_gate, moe_w_up, moe_w_down,
           ln1_g, ln1_b, ln2_g, ln2_b):
    B, S, D = x.shape
    T = B * S
    ksel = min(TOPK_MAX, S // 4)
    x2 = x.reshape(T, D)
    tabs = _rope_tables(positions.reshape(T, 1).astype(F32))
    row = lambda v: v[None].astype(F32)
    for layer in range(DEPTH):
        j = layer // 2
        if layer % 2 == 0:
            w1, wuq, wukv = _even_weights(ev_w_in[j], ev_w_uq[j], ev_w_ukv[j])
            qa, kv, kr, qb, kb2, va, qi, ki2, wi = _even_proj(
                x2, w1, row(ev_g_q[j]), row(ev_g_kv[j]), wuq, wukv, tabs)
            o_a = _flash((qa, kv, kr), fox=False, B=B, S=S, scale=(D_NOPE + D_ROPE) ** -0.5)
            wit = jnp.pad(wi[:, :H_IDX].T, ((0, SUBLANES - H_IDX), (0, 0)))
            o_b = _dsa(qi, qb, wit, ki2, kb2, va, B=B, S=S, ksel=ksel)
            wo = ev_w_o[j].astype(BF16)
            n_a = H_MLA * D_V_MLA
            x2 = _out_ln((o_a, o_b), (wo[:n_a], wo[n_a:]), x2, row(ln1_g[layer]), row(ln1_b[layer]))
        else:
            n = H_FOX * D_FOX
            w = jnp.pad(od_w_in[j], ((0, 0), (0, LANES - H_FOX))).astype(BF16)
            bf = jnp.pad(od_b_f[j], (0, LANES - H_FOX))[None].astype(F32)
            q, k, v, lf = _odd_proj(x2, w, bf)
            c = _cumsum_seq(lf, S)
            c_rows = c.reshape(B, S, LANES)[:, :, :H_FOX].swapaxes(1, 2)
            o = _flash((q, k, v, c, c_rows), fox=True, B=B, S=S, scale=D_FOX ** -0.5)
            x2 = _out_ln((o,), (od_w_o[j].astype(BF16),), x2, row(ln1_g[layer]), row(ln1_b[layer]))
        wr, br = _router_weights(moe_w_grp[layer], moe_b_grp[layer], moe_w_sub[layer], moe_b_sub[layer])
        x2 = _moe_ln(x2, wr, br, moe_w_gate[layer].astype(BF16), moe_w_up[layer].astype(BF16),
                     moe_w_down[layer].astype(BF16), row(ln2_g[layer]), row(ln2_b[layer]))
    return x2.reshape(B, S, D)
```

```python
import functools

import jax
import jax.numpy as jnp
from jax import lax
from jax.experimental import pallas as pl
from jax.experimental.pallas import tpu as pltpu

D_MODEL = 1024
DEPTH = 4
CHUNK = 64
CHUNK_SHIFT = 6
ROPE_THETA = 10000.0
NEG_INF = -1e30
LN_EPS = 1e-5
RMS_EPS = 1e-6
H_MLA, D_NOPE, D_ROPE, D_V_MLA, Q_LORA, KV_LORA = 8, 64, 32, 64, 384, 256
H_DSA, D_DSA, H_IDX, D_IDX, TOPK_MAX = 8, 64, 4, 64, 256
H_FOX, D_FOX = 16, 64
N_GROUPS, EXPERTS_PER_GROUP, D_EXPERT = 4, 4, 512
N_EXPERTS = N_GROUPS * EXPERTS_PER_GROUP
DN_ALPHA = (2 * DEPTH) ** 0.25
EV_SPLITS = (Q_LORA, KV_LORA, D_ROPE, H_DSA * D_DSA, D_DSA, D_DSA, H_IDX * D_IDX, D_IDX, H_IDX)

LANES = 128
SUBLANES = 8
VMEM_LIMIT_BYTES = 56 * 1024 * 1024

TM_PROJ = 512
TM_MOE = 512
TE_MOE = 256
T_ATT = 256
Q_DSA = 128
KC_DSA = 256

F32 = jnp.float32
BF16 = jnp.bfloat16
_NT = (((1,), (1,)), ((), ()))
_TN = (((0,), (0,)), ((), ()))


def _params(*sem):
    return pltpu.CompilerParams(dimension_semantics=sem, vmem_limit_bytes=VMEM_LIMIT_BYTES)


def _lane_iota(shape=(1, LANES)):
    return lax.broadcasted_iota(jnp.int32, shape, len(shape) - 1)


def _full_spec(shape):
    return pl.BlockSpec(shape, lambda *_: (0,) * len(shape))


def _rope_table_kernel(pos_ref, inv32_ref, sg32_ref, inv64_ref, sg64_ref,
                       c32_ref, s32_ref, c64_ref, s64_ref):
    pos = pos_ref[...]
    a32 = pos * inv32_ref[...]
    a64 = pos * inv64_ref[...]
    c32_ref[...] = jnp.cos(a32)
    s32_ref[...] = jnp.sin(a32) * sg32_ref[...]
    c64_ref[...] = jnp.cos(a64)
    s64_ref[...] = jnp.sin(a64) * sg64_ref[...]


def _rope_tables(posf):
    T = posf.shape[0]
    tm = min(T, 2048)
    lane = jnp.arange(LANES)
    inv16 = jnp.power(ROPE_THETA, -jnp.arange(0, D_ROPE, 2, dtype=F32) / D_ROPE)
    inv32 = jnp.power(ROPE_THETA, -jnp.arange(0, D_DSA, 2, dtype=F32) / D_DSA)
    in32 = (lane >= D_NOPE) & (lane < D_NOPE + D_ROPE)
    t32 = jnp.where(in32, inv16[(lane - D_NOPE) % (D_ROPE // 2)], 0.0).astype(F32)[None]
    g32 = jnp.where(in32, jnp.where(lane < D_NOPE + D_ROPE // 2, -1.0, 1.0), 0.0).astype(F32)[None]
    t64 = inv32[lane % (D_DSA // 2)].astype(F32)[None]
    g64 = jnp.where((lane % D_DSA) < D_DSA // 2, -1.0, 1.0).astype(F32)[None]
    row = pl.BlockSpec((tm, LANES), lambda i: (i, 0))
    tab = jax.ShapeDtypeStruct((T, LANES), F32)
    return pl.pallas_call(
        _rope_table_kernel,
        grid=(T // tm,),
        in_specs=[pl.BlockSpec((tm, 1), lambda i: (i, 0))] + [_full_spec((1, LANES))] * 4,
        out_specs=[row] * 4,
        out_shape=[tab] * 4,
        compiler_params=_params("parallel"),
        name="rope_tables",
    )(posf, t32, g32, t64, g64)


def _rope_block(y, c, s, half, second_half):
    partner = jnp.where(second_half, pltpu.roll(y, half, 1), pltpu.roll(y, LANES - half, 1))
    return y * c + partner * s


def _rms(x, g):
    return x * lax.rsqrt(jnp.mean(x * x, axis=-1, keepdims=True) + RMS_EPS) * g


def _layer_norm(z, g, b):
    mu = jnp.mean(z, axis=-1, keepdims=True)
    d = z - mu
    var = jnp.mean(d * d, axis=-1, keepdims=True)
    return d * lax.rsqrt(var + LN_EPS) * g + b


_E_CQ, _E_CKV, _E_KR, _E_QB, _E_KB, _E_VB, _E_QI, _E_KI, _E_WI, _E_END = (
    0, 384, 640, 768, 1280, 1408, 1536, 1792, 1920, 2048)


def _even_proj_kernel(x_ref, w1_ref, gq_ref, gkv_ref, wuq_ref, wukv_ref,
                      c32_ref, s32_ref, c64_ref, s64_ref,
                      qa_ref, kv_ref, kr_ref, qb_ref, kb_ref, va_ref, qi_ref, ki_ref, wi_ref):
    xb = x_ref[...].astype(BF16)
    lane = _lane_iota()
    sec32 = (lane >= D_NOPE + D_ROPE // 2) & (lane < D_NOPE + D_ROPE)
    sec64 = (lane % D_DSA) >= D_DSA // 2
    c32, s32, c64, s64 = c32_ref[...], s32_ref[...], c64_ref[...], s64_ref[...]

    def seg(a, b):
        return jnp.dot(xb, w1_ref[:, a:b], preferred_element_type=F32)

    def rope64_store(h, out_ref):
        for j in range(h.shape[1] // LANES):
            blk = h[:, j * LANES:(j + 1) * LANES]
            out_ref[:, j * LANES:(j + 1) * LANES] = _rope_block(
                blk, c64, s64, D_DSA // 2, sec64).astype(out_ref.dtype)

    cq = _rms(seg(_E_CQ, _E_CKV), gq_ref[...]).astype(BF16)
    q = jnp.dot(cq, wuq_ref[...], preferred_element_type=F32)
    for h in range(H_MLA):
        blk = q[:, h * LANES:(h + 1) * LANES]
        qa_ref[:, h * LANES:(h + 1) * LANES] = _rope_block(
            blk, c32, s32, D_ROPE // 2, sec32).astype(BF16)
    ckv = _rms(seg(_E_CKV, _E_KR), gkv_ref[...]).astype(BF16)
    kv_ref[...] = jnp.dot(ckv, wukv_ref[...], preferred_element_type=F32).astype(BF16)
    kr_ref[...] = _rope_block(seg(_E_KR, _E_QB), c32, s32, D_ROPE // 2, sec32).astype(BF16)
    rope64_store(seg(_E_QB, _E_KB), qb_ref)
    rope64_store(seg(_E_KB, _E_VB), kb_ref)
    va_ref[...] = jnp.where(lane < D_DSA, seg(_E_VB, _E_QI), 1.0).astype(BF16)
    rope64_store(seg(_E_QI, _E_KI), qi_ref)
    rope64_store(seg(_E_KI, _E_WI), ki_ref)
    wi_ref[...] = seg(_E_WI, _E_END)


def _even_weights(w_in, w_uq, w_ukv):
    o = [0]
    for s_ in EV_SPLITS:
        o.append(o[-1] + s_)
    cq, ckv, kr, qb, kb, vb, qi, ki, wi = (w_in[:, o[i]:o[i + 1]] for i in range(9))
    z = lambda n: jnp.zeros((D_MODEL, n), w_in.dtype)
    w1 = jnp.concatenate([
        cq, ckv,
        z(D_NOPE), kr, z(LANES - D_NOPE - D_ROPE),
        qb,
        kb, kb,
        vb, z(LANES - D_DSA),
        qi,
        ki, ki,
        wi, z(LANES - H_IDX)], axis=1).astype(BF16)
    wq = w_uq.reshape(Q_LORA, H_MLA, D_NOPE + D_ROPE)
    wq = jnp.pad(wq, ((0, 0), (0, 0), (0, LANES - D_NOPE - D_ROPE))).reshape(Q_LORA, H_MLA * LANES)
    return w1, wq.astype(BF16), w_ukv.astype(BF16)


def _even_proj(x2, w1, gq, gkv, wuq, wukv, tabs):
    T = x2.shape[0]
    tm = min(TM_PROJ, T)
    row = lambda n: pl.BlockSpec((tm, n), lambda i: (i, 0))
    outs = [(H_MLA * LANES, BF16), (H_MLA * LANES, BF16), (LANES, BF16), (H_DSA * D_DSA, BF16),
            (LANES, BF16), (LANES, BF16), (H_IDX * D_IDX, BF16), (LANES, BF16), (LANES, F32)]
    return pl.pallas_call(
        _even_proj_kernel,
        grid=(T // tm,),
        in_specs=[row(D_MODEL), _full_spec(w1.shape), _full_spec(gq.shape), _full_spec(gkv.shape),
                  _full_spec(wuq.shape), _full_spec(wukv.shape)] + [row(LANES)] * 4,
        out_specs=[row(n) for n, _ in outs],
        out_shape=[jax.ShapeDtypeStruct((T, n), d) for n, d in outs],
        compiler_params=_params("parallel"),
        name="even_proj",
    )(x2, w1, gq, gkv, wuq, wukv, *tabs)


def _odd_proj_kernel(x_ref, w_ref, bf_ref, q_ref, k_ref, v_ref, lf_ref):
    xb = x_ref[...].astype(BF16)
    n = H_FOX * D_FOX
    for j, out in enumerate((q_ref, k_ref, v_ref)):
        out[...] = jnp.dot(xb, w_ref[:, j * n:(j + 1) * n], preferred_element_type=F32).astype(BF16)
    z = jnp.dot(xb, w_ref[:, 3 * n:], preferred_element_type=F32) + bf_ref[...]
    lf_ref[...] = jnp.minimum(z, 0.0) - jnp.log1p(jnp.exp(-jnp.abs(z)))


def _odd_proj(x2, w, bf):
    T = x2.shape[0]
    tm = min(TM_PROJ, T)
    n = H_FOX * D_FOX
    row = lambda c: pl.BlockSpec((tm, c), lambda i: (i, 0))
    return pl.pallas_call(
        _odd_proj_kernel,
        grid=(T // tm,),
        in_specs=[row(D_MODEL), _full_spec(w.shape), _full_spec(bf.shape)],
        out_specs=[row(n), row(n), row(n), row(LANES)],
        out_shape=[jax.ShapeDtypeStruct((T, n), BF16)] * 3 + [jax.ShapeDtypeStruct((T, LANES), F32)],
        compiler_params=_params("parallel"),
        name="odd_proj",
    )(x2, w, bf)


def _cumsum_kernel(lf_ref, c_ref):
    a = lf_ref[...]
    rows = lax.broadcasted_iota(jnp.int32, a.shape, 0)
    k = 1
    while k < a.shape[0]:
        a = a + jnp.where(rows >= k, pltpu.roll(a, k, 0), 0.0)
        k *= 2
    c_ref[...] = a


def _cumsum_seq(lf, S):
    T = lf.shape[0]
    blk = pl.BlockSpec((S, LANES), lambda b: (b, 0))
    return pl.pallas_call(
        _cumsum_kernel, grid=(T // S,), in_specs=[blk], out_specs=blk,
        out_shape=jax.ShapeDtypeStruct((T, LANES), F32),
        compiler_params=_params("parallel"), name="forget_cumsum",
    )(lf)


def _flash_kernel(*refs, fox, scale, t, n_tiles, S):
    if fox:
        q_ref, k_ref, v_ref, cc_ref, cr_ref, o_ref, kx_ref, sb_ref = refs
    else:
        q_ref, kv_ref, kr_ref, o_ref, kx_ref, sb_ref = refs
    hp, qi = pl.program_id(1), pl.program_id(2)
    lane = _lane_iota()
    lo = lane < D_FOX
    hi = lane >= D_FOX
    exp2_scale = scale * 1.4426950408889634

    @pl.when(qi == 0)
    def _():
        for hh in range(2):
            if fox:
                col = jnp.sum(jnp.where(lane == 2 * hp + hh, cc_ref[0], 0.0), axis=-1, keepdims=True)
                kx_ref[hh] = jnp.broadcast_to(col, (S, LANES))
            else:
                kx_ref[hh] = jnp.where(lo, kv_ref[0, :, hh * LANES:(hh + 1) * LANES], kr_ref[0])

    def body(n):
        q_idx = n * t + lax.broadcasted_iota(jnp.int32, (1, t), 1)
        k_loc = lax.broadcasted_iota(jnp.int32, (t, 1), 0)
        outs = []
        for hh in range(2):
            if fox:
                q2 = q_ref[0] * jnp.asarray(scale, BF16)
                qh = jnp.where(lo if hh == 0 else hi, q2, jnp.zeros_like(q2))
                cq = cr_ref[0, pl.ds(2 * hp + hh, 1), n * t:(n + 1) * t]
            else:
                qh = q_ref[0, :, hh * LANES:(hh + 1) * LANES]
            m8 = jnp.full((SUBLANES, t), -jnp.inf, F32)
            for c in range(n + 1):
                rows = slice(c * t, (c + 1) * t)
                kh = k_ref[0, rows, :] if fox else kx_ref[hh, rows, :]
                s = lax.dot_general(kh, qh, _NT, preferred_element_type=F32)
                if fox:
                    s = s + cq - jnp.concatenate([kx_ref[hh, rows, :]] * (t // LANES), axis=1)
                if c == n:
                    k_idx = c * t + k_loc
                    vis = (k_idx <= q_idx) if fox else ((k_idx >> CHUNK_SHIFT) <= (q_idx >> CHUNK_SHIFT))
                    s = jnp.where(vis, s, NEG_INF)
                sb_ref[hh, rows, :] = s
                m8 = jnp.maximum(m8, jnp.max(s.reshape(t // SUBLANES, SUBLANES, t), axis=0))
            m = jnp.max(m8, axis=0, keepdims=True)
            acc = jnp.zeros((t, LANES), F32)
            l8 = jnp.zeros((SUBLANES, t), F32)
            for c in range(n + 1):
                rows = slice(c * t, (c + 1) * t)
                x = sb_ref[hh, rows, :] - m
                p = jnp.exp(x) if fox else jnp.exp2(x * exp2_scale)
                l8 = l8 + jnp.sum(p.reshape(t // SUBLANES, SUBLANES, t), axis=0)
                vh = v_ref[0, rows, :] if fox else kv_ref[0, rows, hh * LANES:(hh + 1) * LANES]
                acc = acc + lax.dot_general(p.astype(BF16), vh, _TN, preferred_element_type=F32)
            l = jnp.sum(l8, axis=0, keepdims=True)
            outs.append(acc / jnp.transpose(jnp.broadcast_to(l, (LANES, t))))
        if fox:
            o_ref[0] = jnp.where(lo, outs[0], outs[1]).astype(BF16)
        else:
            o_ref[0] = jnp.where(lo, pltpu.roll(outs[0], D_V_MLA, 1), outs[1]).astype(BF16)

    for n in range(n_tiles):
        pl.when(qi == n)(functools.partial(body, n))


def _flash(args, *, fox, B, S, scale):
    t = min(T_ATT, S)
    n = S // t
    npairs = (H_FOX if fox else H_MLA) // 2
    args = tuple(a.reshape(B, S, a.shape[-1]) if a.ndim == 2 else a for a in args)
    qblk = lambda b, hp, qi: (b, qi, hp)
    seq = lambda b, hp, qi: (b, 0, hp)
    whole = lambda b, hp, qi: (b, 0, 0)
    if fox:
        in_specs = [pl.BlockSpec((1, t, LANES), qblk), pl.BlockSpec((1, S, LANES), seq),
                    pl.BlockSpec((1, S, LANES), seq), pl.BlockSpec((1, S, LANES), whole),
                    pl.BlockSpec((1, H_FOX, S), whole)]
        kx = pltpu.VMEM((2, S, LANES), F32)
    else:
        in_specs = [pl.BlockSpec((1, t, 2 * LANES), qblk), pl.BlockSpec((1, S, 2 * LANES), seq),
                    pl.BlockSpec((1, S, LANES), whole)]
        kx = pltpu.VMEM((2, S, LANES), BF16)
    out = pl.pallas_call(
        functools.partial(_flash_kernel, fox=fox, scale=scale, t=t, n_tiles=n, S=S),
        grid=(B, npairs, n),
        in_specs=in_specs,
        out_specs=pl.BlockSpec((1, t, LANES), qblk),
        out_shape=jax.ShapeDtypeStruct((B, S, npairs * LANES), BF16),
        scratch_shapes=[kx, pltpu.VMEM((2, S, t), F32)],
        compiler_params=_params("parallel", "parallel", "arbitrary"),
        name="fox_attention" if fox else "mla_attention",
    )(*args)
    return out.reshape(B * S, npairs * LANES)


def _dsa_kernel(qi_ref, qb_ref, wit_ref, ki_ref, kb_ref, va_ref, o_ref,
                qim_ref, qbm_ref, sc_ref, sb_ref, *, nk, i0, ksel):
    i = i0 + pl.program_id(1)
    nc = nk // KC_DSA
    lane = _lane_iota()
    lo = lane < D_DSA
    hi = lane >= D_DSA
    q_chunk = (i * Q_DSA + lane) >> CHUNK_SHIFT
    int_min = jnp.iinfo(jnp.int32).min
    select = nk > ksel

    def chunk(c):
        return slice(c * KC_DSA, (c + 1) * KC_DSA)

    def admissible(c):
        k_idx = c * KC_DSA + lax.broadcasted_iota(jnp.int32, (KC_DSA, 1), 0)
        return (k_idx >> CHUNK_SHIFT) <= q_chunk

    for h in range(H_DSA):
        blk = qb_ref[:, (h // 2) * LANES:(h // 2 + 1) * LANES] * jnp.asarray(D_DSA ** -0.5, BF16)
        qbm_ref[h * Q_DSA:(h + 1) * Q_DSA, :] = jnp.where(lo if h % 2 == 0 else hi, blk, jnp.zeros_like(blk))

    if select:
        for h in range(H_IDX):
            blk = qi_ref[:, (h // 2) * LANES:(h // 2 + 1) * LANES]
            qim_ref[h * Q_DSA:(h + 1) * Q_DSA, :] = jnp.where(lo if h % 2 == 0 else hi, blk, jnp.zeros_like(blk))
        w_flat = jnp.concatenate([wit_ref[h:h + 1, :] for h in range(H_IDX)], axis=1)
        for c in range(nc):
            r = lax.dot_general(ki_ref[0, chunk(c), :], qim_ref[...], _NT, preferred_element_type=F32)
            r = jnp.maximum(r, 0.0) * w_flat
            acc = r[:, :LANES]
            for h in range(1, H_IDX):
                acc = acc + r[:, h * LANES:(h + 1) * LANES]
            sc_ref[chunk(c), :] = jnp.where(admissible(c), acc, NEG_INF)

        def count(mask):
            part = jnp.sum(mask.astype(F32).reshape(nk // 32, 32, LANES), axis=0)
            return jnp.sum(part, axis=0, keepdims=True)

        def as_float(key):
            return lax.bitcast_convert_type(jnp.where(key >= 0, key, key ^ 0x7FFFFFFF), F32)

        def bit_body(r, ans):
            cand = ans + lax.shift_left(jnp.int32(1), 31 - r)
            cnt = count(sc_ref[...] >= as_float(cand))
            return jnp.where(cnt >= ksel, cand, ans)
        ans = lax.fori_loop(0, 32, bit_body, jnp.full((1, LANES), int_min, jnp.int32))
        t = as_float(ans)
        s_all = sc_ref[...]
        need = ksel - count(s_all > t)
        n_ge = count(s_all >= t)
        k_all = lax.broadcasted_iota(jnp.int32, (nk, 1), 0)

        def tie_bound():
            def bit_body2(r, aj):
                cand = aj + lax.shift_left(jnp.int32(1), 11 - r)
                g = count((sc_ref[...] == t) & (k_all < cand))
                return jnp.where(g <= need, cand, aj)
            return lax.fori_loop(0, 12, bit_body2, jnp.zeros((1, LANES), jnp.int32))

        jmax = lax.cond(jnp.max(n_ge) > ksel, tie_bound,
                        lambda: jnp.full((1, LANES), 4095, jnp.int32))
        for c in range(nc):
            s = sc_ref[chunk(c), :]
            k_idx = c * KC_DSA + lax.broadcasted_iota(jnp.int32, (KC_DSA, 1), 0)
            sel = (s > t) | ((s == t) & (k_idx < jmax))
            sc_ref[chunk(c), :] = jnp.where(sel & admissible(c), 0.0, NEG_INF)
    else:
        for c in range(nc):
            sc_ref[chunk(c), :] = jnp.where(admissible(c), 0.0, NEG_INF)

    m8 = jnp.full((SUBLANES, H_DSA * LANES), -jnp.inf, F32)
    for c in range(nc):
        s = lax.dot_general(kb_ref[0, chunk(c), :], qbm_ref[...], _NT, preferred_element_type=F32)
        s = s + jnp.concatenate([sc_ref[chunk(c), :]] * H_DSA, axis=1)
        sb_ref[chunk(c), :] = s
        m8 = jnp.maximum(m8, jnp.max(s.reshape(KC_DSA // SUBLANES, SUBLANES, H_DSA * LANES), axis=0))
    m = jnp.max(m8, axis=0, keepdims=True)
    acc = jnp.zeros((H_DSA * Q_DSA, LANES), F32)
    for c in range(nc):
        p = jnp.exp(sb_ref[chunk(c), :] - m).astype(BF16)
        acc = acc + lax.dot_general(p, va_ref[0, chunk(c), :], _TN, preferred_element_type=F32)

    for hp in range(H_DSA // 2):
        a = acc[(2 * hp) * Q_DSA:(2 * hp + 1) * Q_DSA]
        b = acc[(2 * hp + 1) * Q_DSA:(2 * hp + 2) * Q_DSA]
        ra = a / pltpu.roll(a, D_DSA, 1)
        rb = b / pltpu.roll(b, D_DSA, 1)
        o_ref[:, hp * LANES:(hp + 1) * LANES] = jnp.where(lo, ra, pltpu.roll(rb, D_DSA, 1)).astype(BF16)


def _dsa(qi, qb, wit, ki2, kb2, va, *, B, S, ksel):
    nb = S // Q_DSA
    per = KC_DSA // Q_DSA
    seq3 = lambda a: a.reshape(B, S, LANES)
    ki3, kb3, va3 = seq3(ki2), seq3(kb2), seq3(va)
    outs = []
    for g in range(S // KC_DSA):
        nk, i0 = (g + 1) * KC_DSA, g * per
        qrow = lambda n, i0=i0: pl.BlockSpec((Q_DSA, n), lambda b, j: (b * nb + i0 + j, 0))
        seq = pl.BlockSpec((1, nk, LANES), lambda b, j: (b, 0, 0))
        o = pl.pallas_call(
            functools.partial(_dsa_kernel, nk=nk, i0=i0, ksel=ksel),
            grid=(B, per),
            in_specs=[qrow(H_IDX * D_IDX), qrow(H_DSA * D_DSA),
                      pl.BlockSpec((SUBLANES, Q_DSA), lambda b, j, i0=i0: (0, b * nb + i0 + j)),
                      seq, seq, seq],
            out_specs=pl.BlockSpec((Q_DSA, H_DSA * D_DSA), lambda b, j: (b * per + j, 0)),
            out_shape=jax.ShapeDtypeStruct((B * per * Q_DSA, H_DSA * D_DSA), BF16),
            scratch_shapes=[pltpu.VMEM((H_IDX * Q_DSA, LANES), BF16), pltpu.VMEM((H_DSA * Q_DSA, LANES), BF16),
                            pltpu.VMEM((nk, LANES), F32), pltpu.VMEM((nk, H_DSA * LANES), F32)],
            compiler_params=_params("parallel", "arbitrary"),
            name="dsa_attention_%d" % nk,
        )(qi, qb, wit, ki3, kb3, va3)
        outs.append(o.reshape(B, per * Q_DSA, H_DSA * D_DSA))
    return jnp.concatenate(outs, axis=1).reshape(B * S, H_DSA * D_DSA)


def _out_ln_kernel(*refs, n_in):
    o_refs, w_refs = refs[:n_in], refs[n_in:2 * n_in]
    x_ref, g_ref, b_ref, out_ref = refs[2 * n_in:]
    y = jnp.dot(o_refs[0][...], w_refs[0][...], preferred_element_type=F32)
    for o_r, w_r in zip(o_refs[1:], w_refs[1:]):
        y = y + jnp.dot(o_r[...], w_r[...], preferred_element_type=F32)
    out_ref[...] = _layer_norm(DN_ALPHA * x_ref[...] + y, g_ref[...], b_ref[...])


def _out_ln(os_, ws, x2, g, b):
    T = x2.shape[0]
    tm = min(TM_PROJ, T)
    row = lambda c: pl.BlockSpec((tm, c), lambda i: (i, 0))
    return pl.pallas_call(
        functools.partial(_out_ln_kernel, n_in=len(os_)),
        grid=(T // tm,),
        in_specs=[row(o.shape[1]) for o in os_] + [_full_spec(w.shape) for w in ws]
                 + [row(D_MODEL), _full_spec(g.shape), _full_spec(b.shape)],
        out_specs=row(D_MODEL),
        out_shape=jax.ShapeDtypeStruct((T, D_MODEL), F32),
        compiler_params=_params("parallel"),
        name="out_proj_ln",
    )(*os_, *ws, x2, g, b)


_R_SUB = N_GROUPS


_PAIRS = [(a, b) for a in range(EXPERTS_PER_GROUP) for b in range(a + 1, EXPERTS_PER_GROUP)]
N_CLASSES = N_GROUPS * len(_PAIRS)


def _route_kernel(x_ref, wr_ref, br_ref, tri_ref, route_ref, cnt_ref, run_ref):
    @pl.when(pl.program_id(0) == 0)
    def _():
        run_ref[...] = jnp.zeros(run_ref.shape, F32)

    lg = jnp.dot(x_ref[...].astype(BF16), wr_ref[...], preferred_element_type=F32) + br_ref[...]
    lane = _lane_iota()
    lanef = lane.astype(F32)
    far = float(LANES)
    grp = jnp.where(lane < N_GROUPS, lg, -jnp.inf)
    mg = jnp.max(grp, axis=-1, keepdims=True)
    g_star = jnp.min(jnp.where(grp == mg, lanef, far), axis=-1, keepdims=True)
    lane_grp = ((lane - _R_SUB) >> 2).astype(F32)
    in_grp = (lane_grp == g_star) & (lane >= _R_SUB) & (lane < _R_SUB + N_EXPERTS)
    l1 = jnp.where(in_grp, lg, -jnp.inf)
    v1 = jnp.max(l1, axis=-1, keepdims=True)
    j1 = jnp.min(jnp.where(l1 == v1, lanef, far), axis=-1, keepdims=True)
    l2 = jnp.where(in_grp & (lanef != j1), lg, -jnp.inf)
    v2 = jnp.max(l2, axis=-1, keepdims=True)
    j2 = jnp.min(jnp.where(l2 == v2, lanef, far), axis=-1, keepdims=True)
    base = _R_SUB + EXPERTS_PER_GROUP * g_star
    ja = jnp.minimum(j1, j2) - base
    jb = jnp.maximum(j1, j2) - base
    cls = g_star * len(_PAIRS) + ja * (7.0 - ja) * 0.5 + (jb - ja - 1.0)
    onehot = lanef == cls
    before = jnp.dot(tri_ref[...], jnp.where(onehot, 1.0, 0.0).astype(BF16), preferred_element_type=F32)
    rank = jnp.sum(jnp.where(onehot, before + run_ref[...], 0.0), axis=-1, keepdims=True)
    route_ref[...] = jnp.where(lane == 0, cls, jnp.where(lane == 1, rank, 0.0))
    run_ref[...] += jnp.sum(jnp.where(onehot, 1.0, 0.0), axis=0, keepdims=True)
    cnt_ref[...] = jnp.broadcast_to(run_ref[...], cnt_ref.shape)


def _route(x2, wr, br):
    T = x2.shape[0]
    tm = min(TM_MOE, T)
    tri = jnp.tril(jnp.ones((tm, tm), BF16), -1)
    return pl.pallas_call(
        _route_kernel,
        grid=(T // tm,),
        in_specs=[pl.BlockSpec((tm, D_MODEL), lambda i: (i, 0)), _full_spec(wr.shape), _full_spec(br.shape),
                  _full_spec(tri.shape)],
        out_specs=[pl.BlockSpec((tm, LANES), lambda i: (i, 0)), _full_spec((SUBLANES, LANES))],
        out_shape=[jax.ShapeDtypeStruct((T, LANES), F32), jax.ShapeDtypeStruct((SUBLANES, LANES), F32)],
        scratch_shapes=[pltpu.VMEM((1, LANES), F32)],
        compiler_params=_params("arbitrary"),
        name="moe_route",
    )(x2, wr, br, tri)


def _row_copy(src_ref, dst_ref, src_row, dst_row, sem):
    return pltpu.make_async_copy(src_ref.at[pl.ds(src_row, 1), :], dst_ref.at[pl.ds(dst_row, 1), :], sem)


def _dispatch_kernel(pos_ref, x_ref, init_ref, xs_ref, sem):
    del init_ref
    tm = x_ref.shape[0]

    def start(t, carry):
        _row_copy(x_ref, xs_ref, t, pos_ref[0, 0, t], sem).start()
        return carry
    lax.fori_loop(0, tm, start, 0, unroll=8)

    def wait(t, carry):
        _row_copy(x_ref, xs_ref, 0, 0, sem).wait()
        return carry
    lax.fori_loop(0, tm, wait, 0, unroll=8)


def _collect_kernel(pos_ref, ys_ref, out_ref, sem):
    tm = out_ref.shape[0]

    def start(t, carry):
        _row_copy(ys_ref, out_ref, pos_ref[0, 0, t], t, sem).start()
        return carry
    lax.fori_loop(0, tm, start, 0, unroll=8)

    def wait(t, carry):
        _row_copy(ys_ref, out_ref, 0, 0, sem).wait()
        return carry
    lax.fori_loop(0, tm, wait, 0, unroll=8)


def _dispatch(x2, pos3, n_rows):
    T = x2.shape[0]
    tm = pos3.shape[-1]
    return pl.pallas_call(
        _dispatch_kernel,
        grid=(T // tm,),
        in_specs=[pl.BlockSpec((1, 1, tm), lambda i: (i, 0, 0), memory_space=pltpu.SMEM),
                  pl.BlockSpec((tm, D_MODEL), lambda i: (i, 0)),
                  pl.BlockSpec(memory_space=pl.ANY)],
        out_specs=pl.BlockSpec(memory_space=pl.ANY),
        out_shape=jax.ShapeDtypeStruct((n_rows, D_MODEL), F32),
        scratch_shapes=[pltpu.SemaphoreType.DMA(())],
        input_output_aliases={2: 0},
        compiler_params=_params("arbitrary"),
        name="moe_dispatch",
    )(pos3, x2, jnp.zeros((n_rows, D_MODEL), F32))


def _collect(ys, pos3):
    T = pos3.shape[0] * pos3.shape[-1]
    tm = pos3.shape[-1]
    return pl.pallas_call(
        _collect_kernel,
        grid=(T // tm,),
        in_specs=[pl.BlockSpec((1, 1, tm), lambda i: (i, 0, 0), memory_space=pltpu.SMEM),
                  pl.BlockSpec(memory_space=pl.ANY)],
        out_specs=pl.BlockSpec((tm, D_MODEL), lambda i: (i, 0)),
        out_shape=jax.ShapeDtypeStruct((T, D_MODEL), F32),
        scratch_shapes=[pltpu.SemaphoreType.DMA(())],
        compiler_params=_params("arbitrary"),
        name="moe_collect",
    )(pos3, ys)


def _expert_kernel(tg_ref, ta_ref, tb_ref, tv_ref, x_ref, wr_ref, br_ref,
                   wga_ref, wua_ref, wda_ref, wgb_ref, wub_ref, wdb_ref, g_ref, b_ref, out_ref):
    r = pl.program_id(0)

    @pl.when(tv_ref[r] == 0)
    def _():
        out_ref[...] = jnp.zeros(out_ref.shape, F32)

    @pl.when(tv_ref[r] != 0)
    def _():
        x = x_ref[...]
        xb = x.astype(BF16)
        lg = jnp.dot(xb, wr_ref[...], preferred_element_type=F32) + br_ref[...]
        lane = _lane_iota()
        is_grp = lane < N_GROUPS
        mg = jnp.max(jnp.where(is_grp, lg, -jnp.inf), axis=-1, keepdims=True)
        den = jnp.sum(jnp.where(is_grp, jnp.exp(lg - mg), 0.0), axis=-1, keepdims=True)
        pick = lambda l: jnp.sum(jnp.where(lane == l, lg, 0.0), axis=-1, keepdims=True)
        p_top = jnp.exp(pick(tg_ref[r]) - mg) / den
        va, vb = pick(_R_SUB + ta_ref[r]), pick(_R_SUB + tb_ref[r])
        e = jnp.exp(jnp.minimum(va, vb) - jnp.maximum(va, vb))
        w_top, w_oth = 1.0 / (1.0 + e), e / (1.0 + e)
        a_top = va >= vb
        gates = (jnp.where(a_top, w_top, w_oth) * p_top, jnp.where(a_top, w_oth, w_top) * p_top)
        y = jnp.zeros(x.shape, F32)
        for gate, wg, wu, wd in ((gates[0], wga_ref, wua_ref, wda_ref), (gates[1], wgb_ref, wub_ref, wdb_ref)):
            a = jnp.dot(xb, wg[0], preferred_element_type=F32)
            u = jnp.dot(xb, wu[0], preferred_element_type=F32)
            h = (a * (1.0 / (1.0 + jnp.exp(-a))) * u).astype(BF16)
            y = y + gate * jnp.dot(h, wd[0], preferred_element_type=F32)
        out_ref[...] = _layer_norm(DN_ALPHA * x + y, g_ref[...], b_ref[...])


def _experts(xs, tile_g, tile_a, tile_b, tile_v, wr, br, wg, wu, wd, g, b):
    n_tiles = tile_g.shape[0]
    te = xs.shape[0] // n_tiles
    row = pl.BlockSpec((te, D_MODEL), lambda r, *_: (r, 0))
    full = lambda shape: pl.BlockSpec(shape, lambda r, *_: (0,) * len(shape))
    up_a = pl.BlockSpec((1, D_MODEL, D_EXPERT), lambda r, tg, ta, tb, tv: (ta[r], 0, 0))
    up_b = pl.BlockSpec((1, D_MODEL, D_EXPERT), lambda r, tg, ta, tb, tv: (tb[r], 0, 0))
    dn_a = pl.BlockSpec((1, D_EXPERT, D_MODEL), lambda r, tg, ta, tb, tv: (ta[r], 0, 0))
    dn_b = pl.BlockSpec((1, D_EXPERT, D_MODEL), lambda r, tg, ta, tb, tv: (tb[r], 0, 0))
    return pl.pallas_call(
        _expert_kernel,
        grid_spec=pltpu.PrefetchScalarGridSpec(
            num_scalar_prefetch=4, grid=(n_tiles,),
            in_specs=[row, full(wr.shape), full(br.shape), up_a, up_a, dn_a, up_b, up_b, dn_b,
                      full(g.shape), full(b.shape)],
            out_specs=row),
        out_shape=jax.ShapeDtypeStruct(xs.shape, F32),
        compiler_params=_params("arbitrary"),
        name="moe_experts",
    )(tile_g, tile_a, tile_b, tile_v, xs, wr, br, wg, wu, wd, wg, wu, wd, g, b)


def _moe_ln(x2, wr, br, wg, wu, wd, g, b):
    T = x2.shape[0]
    te = TE_MOE
    tm = min(TM_MOE, T)
    n_tiles = (T + N_CLASSES * (te - 1) + te - 1) // te
    route, counts = _route(x2, wr, br)
    cls = route[:, 0].astype(jnp.int32)
    rank = route[:, 1].astype(jnp.int32)
    cnt = counts[0, :N_CLASSES].astype(jnp.int32)
    padded = (cnt + te - 1) // te * te
    ends = jnp.cumsum(padded)
    pos3 = ((ends - padded)[cls] + rank).reshape(T // tm, 1, tm)
    tile_start = jnp.arange(n_tiles, dtype=jnp.int32) * te
    tile_cls = jnp.minimum(jnp.sum(tile_start[:, None] >= ends[None, :], axis=1), N_CLASSES - 1).astype(jnp.int32)
    tile_v = (tile_start < ends[-1]).astype(jnp.int32)
    pair = jnp.asarray(_PAIRS, jnp.int32)[tile_cls % len(_PAIRS)]
    tile_g = tile_cls // len(_PAIRS)
    tile_a = tile_g * EXPERTS_PER_GROUP + pair[:, 0]
    tile_b = tile_g * EXPERTS_PER_GROUP + pair[:, 1]
    xs = _dispatch(x2, pos3, n_tiles * te)
    ys = _experts(xs, tile_g, tile_a, tile_b, tile_v, wr, br, wg, wu, wd, g, b)
    return _collect(ys, pos3)


def _router_weights(w_grp, b_grp, w_sub, b_sub):
    pad = LANES - N_GROUPS - N_EXPERTS
    wr = jnp.concatenate([w_grp, w_sub, jnp.zeros((D_MODEL, pad), w_grp.dtype)], axis=1).astype(BF16)
    br = jnp.concatenate([b_grp, b_sub, jnp.zeros((pad,), b_grp.dtype)])[None].astype(F32)
    return wr, br


def kernel(x, positions, ev_w_in, ev_g_q, ev_g_kv, ev_w_uq, ev_w_ukv, ev_w_o, od_w_in, od_b_f, od_w_o,
           moe_w_grp, moe_b_grp, moe_w_sub, moe_b_sub, moe_w_gate, moe_w_up, moe_w_down,
           ln1_g, ln1_b, ln2_g, ln2_b):
    B, S, D = x.shape
    T = B * S
    ksel = min(TOPK_MAX, S // 4)
    x2 = x.reshape(T, D)
    tabs = _rope_tables(positions.reshape(T, 1).astype(F32))
    row = lambda v: v[None].astype(F32)
    for layer in range(DEPTH):
        j = layer // 2
        if layer % 2 == 0:
            w1, wuq, wukv = _even_weights(ev_w_in[j], ev_w_uq[j], ev_w_ukv[j])
            qa, kv, kr, qb, kb2, va, qi, ki2, wi = _even_proj(
                x2, w1, row(ev_g_q[j]), row(ev_g_kv[j]), wuq, wukv, tabs)
            o_a = _flash((qa, kv, kr), fox=False, B=B, S=S, scale=(D_NOPE + D_ROPE) ** -0.5)
            wit = jnp.pad(wi[:, :H_IDX].T, ((0, SUBLANES - H_IDX), (0, 0)))
            o_b = _dsa(qi, qb, wit, ki2, kb2, va, B=B, S=S, ksel=ksel)
            wo = ev_w_o[j].astype(BF16)
            n_a = H_MLA * D_V_MLA
            x2 = _out_ln((o_a, o_b), (wo[:n_a], wo[n_a:]), x2, row(ln1_g[layer]), row(ln1_b[layer]))
        else:
            n = H_FOX * D_FOX
            w = jnp.pad(od_w_in[j], ((0, 0), (0, LANES - H_FOX))).astype(BF16)
            bf = jnp.pad(od_b_f[j], (0, LANES - H_FOX))[None].astype(F32)
            q, k, v, lf = _odd_proj(x2, w, bf)
            c = _cumsum_seq(lf, S)
            c_rows = c.reshape(B, S, LANES)[:, :, :H_FOX].swapaxes(1, 2)
            o = _flash((q, k, v, c, c_rows), fox=True, B=B, S=S, scale=D_FOX ** -0.5)
            x2 = _out_ln((o,), (od_w_o[j].astype(BF16),), x2, row(ln1_g[layer]), row(ln1_b[layer]))
        wr, br = _router_weights(moe_w_grp[layer], moe_b_grp[layer], moe_w_sub[layer], moe_b_sub[layer])
        x2 = _moe_ln(x2, wr, br, moe_w_gate[layer].astype(BF16), moe_w_up[layer].astype(BF16),
                     moe_w_down[layer].astype(BF16), row(ln2_g[layer]), row(ln2_b[layer]))
    return x2.reshape(B, S, D)
```

```python
import functools

import jax
import jax.numpy as jnp
from jax import lax
from jax.experimental import pallas as pl
from jax.experimental.pallas import tpu as pltpu

D_MODEL = 1024
DEPTH = 4
CHUNK = 64
CHUNK_SHIFT = 6
ROPE_THETA = 10000.0
NEG_INF = -1e30
LN_EPS = 1e-5
RMS_EPS = 1e-6
H_MLA, D_NOPE, D_ROPE, D_V_MLA, Q_LORA, KV_LORA = 8, 64, 32, 64, 384, 256
H_DSA, D_DSA, H_IDX, D_IDX, TOPK_MAX = 8, 64, 4, 64, 256
H_FOX, D_FOX = 16, 64
N_GROUPS, EXPERTS_PER_GROUP, D_EXPERT = 4, 4, 512
N_EXPERTS = N_GROUPS * EXPERTS_PER_GROUP
DN_ALPHA = (2 * DEPTH) ** 0.25
EV_SPLITS = (Q_LORA, KV_LORA, D_ROPE, H_DSA * D_DSA, D_DSA, D_DSA, H_IDX * D_IDX, D_IDX, H_IDX)

LANES = 128
SUBLANES = 8
VMEM_LIMIT_BYTES = 56 * 1024 * 1024

TM_PROJ = 512
TM_MOE = 512
TE_MOE = 256
T_ATT = 256
VT_ROWS = 80
Q_DSA = 128
KC_DSA = 256

F32 = jnp.float32
BF16 = jnp.bfloat16
_NT = (((1,), (1,)), ((), ()))
_TN = (((0,), (0,)), ((), ()))


def _params(*sem):
    return pltpu.CompilerParams(dimension_semantics=sem, vmem_limit_bytes=VMEM_LIMIT_BYTES)


def _lane_iota(shape=(1, LANES)):
    return lax.broadcasted_iota(jnp.int32, shape, len(shape) - 1)


def _full_spec(shape):
    return pl.BlockSpec(shape, lambda *_: (0,) * len(shape))


def _rope_table_kernel(pos_ref, inv32_ref, sg32_ref, inv64_ref, sg64_ref,
                       c32_ref, s32_ref, c64_ref, s64_ref):
    pos = pos_ref[...]
    a32 = pos * inv32_ref[...]
    a64 = pos * inv64_ref[...]
    c32_ref[...] = jnp.cos(a32)
    s32_ref[...] = jnp.sin(a32) * sg32_ref[...]
    c64_ref[...] = jnp.cos(a64)
    s64_ref[...] = jnp.sin(a64) * sg64_ref[...]


def _rope_tables(posf):
    T = posf.shape[0]
    tm = min(T, 2048)
    lane = jnp.arange(LANES)
    inv16 = jnp.power(ROPE_THETA, -jnp.arange(0, D_ROPE, 2, dtype=F32) / D_ROPE)
    inv32 = jnp.power(ROPE_THETA, -jnp.arange(0, D_DSA, 2, dtype=F32) / D_DSA)
    in32 = (lane >= D_NOPE) & (lane < D_NOPE + D_ROPE)
    t32 = jnp.where(in32, inv16[(lane - D_NOPE) % (D_ROPE // 2)], 0.0).astype(F32)[None]
    g32 = jnp.where(in32, jnp.where(lane < D_NOPE + D_ROPE // 2, -1.0, 1.0), 0.0).astype(F32)[None]
    t64 = inv32[lane % (D_DSA // 2)].astype(F32)[None]
    g64 = jnp.where((lane % D_DSA) < D_DSA // 2, -1.0, 1.0).astype(F32)[None]
    row = pl.BlockSpec((tm, LANES), lambda i: (i, 0))
    tab = jax.ShapeDtypeStruct((T, LANES), F32)
    return pl.pallas_call(
        _rope_table_kernel,
        grid=(T // tm,),
        in_specs=[pl.BlockSpec((tm, 1), lambda i: (i, 0))] + [_full_spec((1, LANES))] * 4,
        out_specs=[row] * 4,
        out_shape=[tab] * 4,
        compiler_params=_params("parallel"),
        name="rope_tables",
    )(posf, t32, g32, t64, g64)


def _rope_block(y, c, s, half, second_half):
    partner = jnp.where(second_half, pltpu.roll(y, half, 1), pltpu.roll(y, LANES - half, 1))
    return y * c + partner * s


def _rms(x, g):
    return x * lax.rsqrt(jnp.mean(x * x, axis=-1, keepdims=True) + RMS_EPS) * g


def _layer_norm(z, g, b):
    mu = jnp.mean(z, axis=-1, keepdims=True)
    d = z - mu
    var = jnp.mean(d * d, axis=-1, keepdims=True)
    return d * lax.rsqrt(var + LN_EPS) * g + b


_E_CQ, _E_CKV, _E_KR, _E_QB, _E_KB, _E_VB, _E_QI, _E_KI, _E_WI, _E_END = (
    0, 384, 640, 768, 1280, 1408, 1536, 1792, 1920, 2048)


def _even_proj_kernel(x_ref, w1_ref, gq_ref, gkv_ref, wuq_ref, wukv_ref,
                      c32_ref, s32_ref, c64_ref, s64_ref,
                      qa_ref, kv_ref, kr_ref, qb_ref, kb_ref, va_ref, qi_ref, ki_ref, wi_ref):
    xb = x_ref[...].astype(BF16)
    lane = _lane_iota()
    sec32 = (lane >= D_NOPE + D_ROPE // 2) & (lane < D_NOPE + D_ROPE)
    sec64 = (lane % D_DSA) >= D_DSA // 2
    c32, s32, c64, s64 = c32_ref[...], s32_ref[...], c64_ref[...], s64_ref[...]

    def seg(a, b):
        return jnp.dot(xb, w1_ref[:, a:b], preferred_element_type=F32)

    def rope64_store(h, out_ref):
        for j in range(h.shape[1] // LANES):
            blk = h[:, j * LANES:(j + 1) * LANES]
            out_ref[:, j * LANES:(j + 1) * LANES] = _rope_block(
                blk, c64, s64, D_DSA // 2, sec64).astype(out_ref.dtype)

    cq = _rms(seg(_E_CQ, _E_CKV), gq_ref[...]).astype(BF16)
    q = jnp.dot(cq, wuq_ref[...], preferred_element_type=F32)
    for h in range(H_MLA):
        blk = q[:, h * LANES:(h + 1) * LANES]
        qa_ref[:, h * LANES:(h + 1) * LANES] = _rope_block(
            blk, c32, s32, D_ROPE // 2, sec32).astype(BF16)
    ckv = _rms(seg(_E_CKV, _E_KR), gkv_ref[...]).astype(BF16)
    kv_ref[...] = jnp.dot(ckv, wukv_ref[...], preferred_element_type=F32).astype(BF16)
    kr_ref[...] = _rope_block(seg(_E_KR, _E_QB), c32, s32, D_ROPE // 2, sec32).astype(BF16)
    rope64_store(seg(_E_QB, _E_KB), qb_ref)
    rope64_store(seg(_E_KB, _E_VB), kb_ref)
    va_ref[...] = jnp.where(lane < D_DSA, seg(_E_VB, _E_QI), 1.0).astype(BF16)
    rope64_store(seg(_E_QI, _E_KI), qi_ref)
    rope64_store(seg(_E_KI, _E_WI), ki_ref)
    wi_ref[...] = seg(_E_WI, _E_END)


def _even_weights(w_in, w_uq, w_ukv):
    o = [0]
    for s_ in EV_SPLITS:
        o.append(o[-1] + s_)
    cq, ckv, kr, qb, kb, vb, qi, ki, wi = (w_in[:, o[i]:o[i + 1]] for i in range(9))
    z = lambda n: jnp.zeros((D_MODEL, n), w_in.dtype)
    w1 = jnp.concatenate([
        cq, ckv,
        z(D_NOPE), kr, z(LANES - D_NOPE - D_ROPE),
        qb,
        kb, kb,
        vb, z(LANES - D_DSA),
        qi,
        ki, ki,
        wi, z(LANES - H_IDX)], axis=1).astype(BF16)
    wq = w_uq.reshape(Q_LORA, H_MLA, D_NOPE + D_ROPE)
    wq = jnp.pad(wq, ((0, 0), (0, 0), (0, LANES - D_NOPE - D_ROPE))).reshape(Q_LORA, H_MLA * LANES)
    return w1, wq.astype(BF16), w_ukv.astype(BF16)


def _even_proj(x2, w1, gq, gkv, wuq, wukv, tabs):
    T = x2.shape[0]
    tm = min(TM_PROJ, T)
    row = lambda n: pl.BlockSpec((tm, n), lambda i: (i, 0))
    outs = [(H_MLA * LANES, BF16), (H_MLA * LANES, BF16), (LANES, BF16), (H_DSA * D_DSA, BF16),
            (LANES, BF16), (LANES, BF16), (H_IDX * D_IDX, BF16), (LANES, BF16), (LANES, F32)]
    return pl.pallas_call(
        _even_proj_kernel,
        grid=(T // tm,),
        in_specs=[row(D_MODEL), _full_spec(w1.shape), _full_spec(gq.shape), _full_spec(gkv.shape),
                  _full_spec(wuq.shape), _full_spec(wukv.shape)] + [row(LANES)] * 4,
        out_specs=[row(n) for n, _ in outs],
        out_shape=[jax.ShapeDtypeStruct((T, n), d) for n, d in outs],
        compiler_params=_params("parallel"),
        name="even_proj",
    )(x2, w1, gq, gkv, wuq, wukv, *tabs)


def _odd_proj_kernel(x_ref, w_ref, bf_ref, q_ref, k_ref, v_ref, lf_ref):
    xb = x_ref[...].astype(BF16)
    n = H_FOX * D_FOX
    for j, out in enumerate((q_ref, k_ref, v_ref)):
        out[...] = jnp.dot(xb, w_ref[:, j * n:(j + 1) * n], preferred_element_type=F32).astype(BF16)
    z = jnp.dot(xb, w_ref[:, 3 * n:], preferred_element_type=F32) + bf_ref[...]
    lf_ref[...] = jnp.minimum(z, 0.0) - jnp.log1p(jnp.exp(-jnp.abs(z)))


def _odd_proj(x2, w, bf):
    T = x2.shape[0]
    tm = min(TM_PROJ, T)
    n = H_FOX * D_FOX
    row = lambda c: pl.BlockSpec((tm, c), lambda i: (i, 0))
    return pl.pallas_call(
        _odd_proj_kernel,
        grid=(T // tm,),
        in_specs=[row(D_MODEL), _full_spec(w.shape), _full_spec(bf.shape)],
        out_specs=[row(n), row(n), row(n), row(LANES)],
        out_shape=[jax.ShapeDtypeStruct((T, n), BF16)] * 3 + [jax.ShapeDtypeStruct((T, LANES), F32)],
        compiler_params=_params("parallel"),
        name="odd_proj",
    )(x2, w, bf)


def _cumsum_kernel(lf_ref, c_ref):
    a = lf_ref[...]
    rows = lax.broadcasted_iota(jnp.int32, a.shape, 0)
    k = 1
    while k < a.shape[0]:
        a = a + jnp.where(rows >= k, pltpu.roll(a, k, 0), 0.0)
        k *= 2
    c_ref[...] = a


def _cumsum_seq(lf, S):
    T = lf.shape[0]
    blk = pl.BlockSpec((S, LANES), lambda b: (b, 0))
    return pl.pallas_call(
        _cumsum_kernel, grid=(T // S,), in_specs=[blk], out_specs=blk,
        out_shape=jax.ShapeDtypeStruct((T, LANES), F32),
        compiler_params=_params("parallel"), name="forget_cumsum",
    )(lf)


def _flash_kernel(*refs, fox, scale, t, n_tiles, S):
    if fox:
        q_ref, k_ref, v_ref, cc_ref, cr_ref, o_ref, kx_ref, vt_ref, sb0_ref, sb1_ref = refs
    else:
        q_ref, kv_ref, kr_ref, o_ref, kx_ref, vt_ref, sb0_ref, sb1_ref = refs
    sb_refs = (sb0_ref, sb1_ref)
    hp, qi = pl.program_id(1), pl.program_id(2)
    lane = _lane_iota()
    lo = lane < D_FOX
    hi = lane >= D_FOX
    exp2_scale = scale * 1.4426950408889634

    @pl.when(qi == 0)
    def _():
        ones = jnp.ones((VT_ROWS - D_FOX, S), BF16)
        if fox:
            vt = jnp.transpose(v_ref[0].astype(F32)).astype(BF16)
        for hh in range(2):
            if fox:
                col = jnp.sum(jnp.where(lane == 2 * hp + hh, cc_ref[0], 0.0), axis=-1, keepdims=True)
                kx_ref[hh] = jnp.broadcast_to(col, (S, LANES))
                vt_ref[hh, :D_FOX, :] = vt[hh * D_FOX:(hh + 1) * D_FOX]
            else:
                kvh = kv_ref[0, :, hh * LANES:(hh + 1) * LANES]
                kx_ref[hh] = jnp.where(lo, kvh, kr_ref[0])
                vt_ref[hh, :D_V_MLA, :] = jnp.transpose(kvh.astype(F32))[D_NOPE:].astype(BF16)
            vt_ref[hh, D_FOX:, :] = ones

    def body(n):
        q_idx = n * t + lax.broadcasted_iota(jnp.int32, (1, t), 1)
        k_loc = lax.broadcasted_iota(jnp.int32, (t, 1), 0)
        tiles = [slice(c * t, (c + 1) * t) for c in range(n + 1)]

        outs = []
        for hh in range(2):
            if fox:
                q2 = q_ref[0] * jnp.asarray(scale, BF16)
                qh = jnp.where(lo if hh == 0 else hi, q2, jnp.zeros_like(q2))
                cq = cr_ref[0, pl.ds(2 * hp + hh, 1), n * t:(n + 1) * t]
            else:
                qh = q_ref[0, :, hh * LANES:(hh + 1) * LANES]
            m8 = jnp.full((SUBLANES, t), -jnp.inf, F32)
            for c in range(n + 1):
                kh = k_ref[0, tiles[c], :] if fox else kx_ref[hh, tiles[c], :]
                s = lax.dot_general(kh, qh, _NT, preferred_element_type=F32)
                if fox:
                    s = s + cq - jnp.concatenate([kx_ref[hh, tiles[c], :]] * (t // LANES), axis=1)
                if c == n:
                    k_idx = c * t + k_loc
                    vis = (k_idx <= q_idx) if fox else ((k_idx >> CHUNK_SHIFT) <= (q_idx >> CHUNK_SHIFT))
                    s = jnp.where(vis, s, NEG_INF)
                sb_refs[hh][tiles[c], :] = s
                m8 = jnp.maximum(m8, jnp.max(s.reshape(t // SUBLANES, SUBLANES, t), axis=0))
            m = jnp.max(m8, axis=0, keepdims=True)
            acc = jnp.zeros((VT_ROWS, t), F32)
            for c in range(n + 1):
                x = sb_refs[hh][tiles[c], :] - m
                p = jnp.exp(x) if fox else jnp.exp2(x * exp2_scale)
                acc = acc + jnp.dot(vt_ref[hh, :, tiles[c]], p.astype(BF16), preferred_element_type=F32)
            outs.append(acc[:D_FOX] / acc[D_FOX:D_FOX + 1])
        o_ref[0] = jnp.transpose(jnp.concatenate(outs, axis=0)).astype(BF16)

    for n in range(n_tiles):
        pl.when(qi == n)(functools.partial(body, n))


def _flash(args, *, fox, B, S, scale):
    t = min(T_ATT, S)
    n = S // t
    npairs = (H_FOX if fox else H_MLA) // 2
    args = tuple(a.reshape(B, S, a.shape[-1]) if a.ndim == 2 else a for a in args)
    qblk = lambda b, hp, qi: (b, qi, hp)
    seq = lambda b, hp, qi: (b, 0, hp)
    whole = lambda b, hp, qi: (b, 0, 0)
    scratch = [pltpu.VMEM((2, S, LANES), F32 if fox else BF16), pltpu.VMEM((2, VT_ROWS, S), BF16),
               pltpu.VMEM((S, t), F32), pltpu.VMEM((S, t), F32)]
    if fox:
        in_specs = [pl.BlockSpec((1, t, LANES), qblk), pl.BlockSpec((1, S, LANES), seq),
                    pl.BlockSpec((1, S, LANES), seq), pl.BlockSpec((1, S, LANES), whole),
                    pl.BlockSpec((1, H_FOX, S), whole)]
    else:
        in_specs = [pl.BlockSpec((1, t, 2 * LANES), qblk), pl.BlockSpec((1, S, 2 * LANES), seq),
                    pl.BlockSpec((1, S, LANES), whole)]
    out = pl.pallas_call(
        functools.partial(_flash_kernel, fox=fox, scale=scale, t=t, n_tiles=n, S=S),
        grid=(B, npairs, n),
        in_specs=in_specs,
        out_specs=pl.BlockSpec((1, t, LANES), qblk),
        out_shape=jax.ShapeDtypeStruct((B, S, npairs * LANES), BF16),
        scratch_shapes=scratch,
        compiler_params=_params("parallel", "parallel", "arbitrary"),
        name="fox_attention" if fox else "mla_attention",
    )(*args)
    return out.reshape(B * S, npairs * LANES)


def _dsa_kernel(qi_ref, qb_ref, wit_ref, ki_ref, kb_ref, va_ref, o_ref,
                qim_ref, qbm_ref, sc_ref, sb_ref, khi_ref, klo_ref, *, nk, i0, ksel):
    i = i0 + pl.program_id(1)
    nc = nk // KC_DSA
    lane = _lane_iota()
    lo = lane < D_DSA
    hi = lane >= D_DSA
    q_chunk = (i * Q_DSA + lane) >> CHUNK_SHIFT
    select = nk > ksel

    def chunk(c):
        return slice(c * KC_DSA, (c + 1) * KC_DSA)

    def admissible(c):
        k_idx = c * KC_DSA + lax.broadcasted_iota(jnp.int32, (KC_DSA, 1), 0)
        return (k_idx >> CHUNK_SHIFT) <= q_chunk

    for h in range(H_DSA):
        blk = qb_ref[:, (h // 2) * LANES:(h // 2 + 1) * LANES] * jnp.asarray(D_DSA ** -0.5, BF16)
        qbm_ref[h * Q_DSA:(h + 1) * Q_DSA, :] = jnp.where(lo if h % 2 == 0 else hi, blk, jnp.zeros_like(blk))

    if select:
        for h in range(H_IDX):
            blk = qi_ref[:, (h // 2) * LANES:(h // 2 + 1) * LANES]
            qim_ref[h * Q_DSA:(h + 1) * Q_DSA, :] = jnp.where(lo if h % 2 == 0 else hi, blk, jnp.zeros_like(blk))
        w_flat = jnp.concatenate([wit_ref[h:h + 1, :] for h in range(H_IDX)], axis=1)
        for c in range(nc):
            r = lax.dot_general(ki_ref[0, chunk(c), :], qim_ref[...], _NT, preferred_element_type=F32)
            r = jnp.maximum(r, 0.0) * w_flat
            acc = r[:, :LANES]
            for h in range(1, H_IDX):
                acc = acc + r[:, h * LANES:(h + 1) * LANES]
            sc_ref[chunk(c), :] = jnp.where(admissible(c), acc, NEG_INF)

        def count(mask):
            part = jnp.sum(mask.astype(F32).reshape(nk // 32, 32, LANES), axis=0)
            return jnp.sum(part, axis=0, keepdims=True)

        def as_float(key):
            return lax.bitcast_convert_type(jnp.where(key >= 0, key, key ^ 0x7FFFFFFF), F32)

        s_all = sc_ref[...]
        bits = lax.bitcast_convert_type(s_all, jnp.int32)
        key = jnp.where(bits >= 0, bits, bits ^ 0x7FFFFFFF)
        khi_ref[...] = (key >> 16).astype(jnp.int16)
        klo_ref[...] = ((key & 0xFFFF) - 32768).astype(jnp.int16)
        i16_min = -32768

        def count16(mask):
            ones = jnp.where(mask, jnp.ones((), BF16), jnp.zeros((), BF16))
            parts = [ones[r:r + 64] for r in range(0, nk, 64)]
            while len(parts) > 1:
                parts = [parts[j] + parts[j + 1] for j in range(0, len(parts) - 1, 2)] + parts[len(parts) & ~1:]
            return jnp.sum(parts[0].astype(F32), axis=0, keepdims=True).astype(jnp.int32)

        def bisect16(ref, base):
            def bit_body(r, ans):
                cand = ans + lax.shift_left(jnp.int32(1), 15 - r)
                cnt = base + count16(ref[...] >= cand.astype(jnp.int16))
                return jnp.where(cnt >= ksel, cand, ans)
            return lax.fori_loop(0, 16, bit_body, jnp.full((1, LANES), i16_min, jnp.int32))

        k_hi = bisect16(khi_ref, 0)
        hi16 = k_hi.astype(jnp.int16)
        above = count16(khi_ref[...] > hi16)
        klo_ref[...] = jnp.where(khi_ref[...] == hi16, klo_ref[...], jnp.int16(i16_min))
        k_lo = bisect16(klo_ref, above)
        t = as_float(lax.shift_left(k_hi, 16) + (k_lo - i16_min))
        need = ksel - count(s_all > t)
        n_ge = count(s_all >= t)
        k_all = lax.broadcasted_iota(jnp.int32, (nk, 1), 0)

        def tie_bound():
            def bit_body2(r, aj):
                cand = aj + lax.shift_left(jnp.int32(1), 11 - r)
                g = count((sc_ref[...] == t) & (k_all < cand))
                return jnp.where(g <= need, cand, aj)
            return lax.fori_loop(0, 12, bit_body2, jnp.zeros((1, LANES), jnp.int32))

        jmax = lax.cond(jnp.max(n_ge) > ksel, tie_bound,
                        lambda: jnp.full((1, LANES), 4095, jnp.int32))
        for c in range(nc):
            s = sc_ref[chunk(c), :]
            k_idx = c * KC_DSA + lax.broadcasted_iota(jnp.int32, (KC_DSA, 1), 0)
            sel = (s > t) | ((s == t) & (k_idx < jmax))
            sc_ref[chunk(c), :] = jnp.where(sel & admissible(c), 0.0, NEG_INF)
    else:
        for c in range(nc):
            sc_ref[chunk(c), :] = jnp.where(admissible(c), 0.0, NEG_INF)

    m8 = jnp.full((SUBLANES, H_DSA * LANES), -jnp.inf, F32)
    for c in range(nc):
        s = lax.dot_general(kb_ref[0, chunk(c), :], qbm_ref[...], _NT, preferred_element_type=F32)
        s = s + jnp.concatenate([sc_ref[chunk(c), :]] * H_DSA, axis=1)
        sb_ref[chunk(c), :] = s
        m8 = jnp.maximum(m8, jnp.max(s.reshape(KC_DSA // SUBLANES, SUBLANES, H_DSA * LANES), axis=0))
    m = jnp.max(m8, axis=0, keepdims=True)
    acc = jnp.zeros((H_DSA * Q_DSA, LANES), F32)
    for c in range(nc):
        p = jnp.exp(sb_ref[chunk(c), :] - m).astype(BF16)
        acc = acc + lax.dot_general(p, va_ref[0, chunk(c), :], _TN, preferred_element_type=F32)

    for hp in range(H_DSA // 2):
        a = acc[(2 * hp) * Q_DSA:(2 * hp + 1) * Q_DSA]
        b = acc[(2 * hp + 1) * Q_DSA:(2 * hp + 2) * Q_DSA]
        ra = a / pltpu.roll(a, D_DSA, 1)
        rb = b / pltpu.roll(b, D_DSA, 1)
        o_ref[:, hp * LANES:(hp + 1) * LANES] = jnp.where(lo, ra, pltpu.roll(rb, D_DSA, 1)).astype(BF16)


def _dsa(qi, qb, wit, ki2, kb2, va, *, B, S, ksel):
    nb = S // Q_DSA
    per = KC_DSA // Q_DSA
    seq3 = lambda a: a.reshape(B, S, LANES)
    ki3, kb3, va3 = seq3(ki2), seq3(kb2), seq3(va)
    outs = []
    for g in range(S // KC_DSA):
        nk, i0 = (g + 1) * KC_DSA, g * per
        qrow = lambda n, i0=i0: pl.BlockSpec((Q_DSA, n), lambda b, j: (b * nb + i0 + j, 0))
        seq = pl.BlockSpec((1, nk, LANES), lambda b, j: (b, 0, 0))
        o = pl.pallas_call(
            functools.partial(_dsa_kernel, nk=nk, i0=i0, ksel=ksel),
            grid=(B, per),
            in_specs=[qrow(H_IDX * D_IDX), qrow(H_DSA * D_DSA),
                      pl.BlockSpec((SUBLANES, Q_DSA), lambda b, j, i0=i0: (0, b * nb + i0 + j)),
                      seq, seq, seq],
            out_specs=pl.BlockSpec((Q_DSA, H_DSA * D_DSA), lambda b, j: (b * per + j, 0)),
            out_shape=jax.ShapeDtypeStruct((B * per * Q_DSA, H_DSA * D_DSA), BF16),
            scratch_shapes=[pltpu.VMEM((H_IDX * Q_DSA, LANES), BF16), pltpu.VMEM((H_DSA * Q_DSA, LANES), BF16),
                            pltpu.VMEM((nk, LANES), F32), pltpu.VMEM((nk, H_DSA * LANES), F32),
                            pltpu.VMEM((nk, LANES), jnp.int16), pltpu.VMEM((nk, LANES), jnp.int16)],
            compiler_params=_params("parallel", "arbitrary"),
            name="dsa_attention_%d" % nk,
        )(qi, qb, wit, ki3, kb3, va3)
        outs.append(o.reshape(B, per * Q_DSA, H_DSA * D_DSA))
    return jnp.concatenate(outs, axis=1).reshape(B * S, H_DSA * D_DSA)


def _out_ln_kernel(*refs, n_in):
    o_refs, w_refs = refs[:n_in], refs[n_in:2 * n_in]
    x_ref, g_ref, b_ref, out_ref = refs[2 * n_in:]
    y = jnp.dot(o_refs[0][...], w_refs[0][...], preferred_element_type=F32)
    for o_r, w_r in zip(o_refs[1:], w_refs[1:]):
        y = y + jnp.dot(o_r[...], w_r[...], preferred_element_type=F32)
    out_ref[...] = _layer_norm(DN_ALPHA * x_ref[...] + y, g_ref[...], b_ref[...])


def _out_ln(os_, ws, x2, g, b):
    T = x2.shape[0]
    tm = min(TM_PROJ, T)
    row = lambda c: pl.BlockSpec((tm, c), lambda i: (i, 0))
    return pl.pallas_call(
        functools.partial(_out_ln_kernel, n_in=len(os_)),
        grid=(T // tm,),
        in_specs=[row(o.shape[1]) for o in os_] + [_full_spec(w.shape) for w in ws]
                 + [row(D_MODEL), _full_spec(g.shape), _full_spec(b.shape)],
        out_specs=row(D_MODEL),
        out_shape=jax.ShapeDtypeStruct((T, D_MODEL), F32),
        compiler_params=_params("parallel"),
        name="out_proj_ln",
    )(*os_, *ws, x2, g, b)


_R_SUB = N_GROUPS


_PAIRS = [(a, b) for a in range(EXPERTS_PER_GROUP) for b in range(a + 1, EXPERTS_PER_GROUP)]
N_CLASSES = N_GROUPS * len(_PAIRS)


def _route_kernel(x_ref, wr_ref, br_ref, tri_ref, route_ref, cnt_ref, run_ref):
    @pl.when(pl.program_id(0) == 0)
    def _():
        run_ref[...] = jnp.zeros(run_ref.shape, F32)

    lg = jnp.dot(x_ref[...].astype(BF16), wr_ref[...], preferred_element_type=F32) + br_ref[...]
    lane = _lane_iota()
    lanef = lane.astype(F32)
    far = float(LANES)
    grp = jnp.where(lane < N_GROUPS, lg, -jnp.inf)
    mg = jnp.max(grp, axis=-1, keepdims=True)
    g_star = jnp.min(jnp.where(grp == mg, lanef, far), axis=-1, keepdims=True)
    lane_grp = ((lane - _R_SUB) >> 2).astype(F32)
    in_grp = (lane_grp == g_star) & (lane >= _R_SUB) & (lane < _R_SUB + N_EXPERTS)
    l1 = jnp.where(in_grp, lg, -jnp.inf)
    v1 = jnp.max(l1, axis=-1, keepdims=True)
    j1 = jnp.min(jnp.where(l1 == v1, lanef, far), axis=-1, keepdims=True)
    l2 = jnp.where(in_grp & (lanef != j1), lg, -jnp.inf)
    v2 = jnp.max(l2, axis=-1, keepdims=True)
    j2 = jnp.min(jnp.where(l2 == v2, lanef, far), axis=-1, keepdims=True)
    base = _R_SUB + EXPERTS_PER_GROUP * g_star
    ja = jnp.minimum(j1, j2) - base
    jb = jnp.maximum(j1, j2) - base
    cls = g_star * len(_PAIRS) + ja * (7.0 - ja) * 0.5 + (jb - ja - 1.0)
    onehot = lanef == cls
    before = jnp.dot(tri_ref[...], jnp.where(onehot, 1.0, 0.0).astype(BF16), preferred_element_type=F32)
    rank = jnp.sum(jnp.where(onehot, before + run_ref[...], 0.0), axis=-1, keepdims=True)
    route_ref[...] = jnp.where(lane == 0, cls, jnp.where(lane == 1, rank, 0.0))
    run_ref[...] += jnp.sum(jnp.where(onehot, 1.0, 0.0), axis=0, keepdims=True)
    cnt_ref[...] = jnp.broadcast_to(run_ref[...], cnt_ref.shape)


def _route(x2, wr, br):
    T = x2.shape[0]
    tm = min(TM_MOE, T)
    tri = jnp.tril(jnp.ones((tm, tm), BF16), -1)
    return pl.pallas_call(
        _route_kernel,
        grid=(T // tm,),
        in_specs=[pl.BlockSpec((tm, D_MODEL), lambda i: (i, 0)), _full_spec(wr.shape), _full_spec(br.shape),
                  _full_spec(tri.shape)],
        out_specs=[pl.BlockSpec((tm, LANES), lambda i: (i, 0)), _full_spec((SUBLANES, LANES))],
        out_shape=[jax.ShapeDtypeStruct((T, LANES), F32), jax.ShapeDtypeStruct((SUBLANES, LANES), F32)],
        scratch_shapes=[pltpu.VMEM((1, LANES), F32)],
        compiler_params=_params("arbitrary"),
        name="moe_route",
    )(x2, wr, br, tri)


def _row_copy(src_ref, dst_ref, src_row, dst_row, sem):
    return pltpu.make_async_copy(src_ref.at[pl.ds(src_row, 1), :], dst_ref.at[pl.ds(dst_row, 1), :], sem)


def _dispatch_kernel(pos_ref, x_ref, init_ref, xs_ref, sem):
    del init_ref
    tm = x_ref.shape[0]

    def start(t, carry):
        _row_copy(x_ref, xs_ref, t, pos_ref[0, 0, t], sem).start()
        return carry
    lax.fori_loop(0, tm, start, 0, unroll=8)

    def wait(t, carry):
        _row_copy(x_ref, xs_ref, 0, 0, sem).wait()
        return carry
    lax.fori_loop(0, tm, wait, 0, unroll=8)


def _collect_kernel(pos_ref, ys_ref, out_ref, sem):
    tm = out_ref.shape[0]

    def start(t, carry):
        _row_copy(ys_ref, out_ref, pos_ref[0, 0, t], t, sem).start()
        return carry
    lax.fori_loop(0, tm, start, 0, unroll=8)

    def wait(t, carry):
        _row_copy(ys_ref, out_ref, 0, 0, sem).wait()
        return carry
    lax.fori_loop(0, tm, wait, 0, unroll=8)


def _dispatch(x2, pos3, n_rows):
    T = x2.shape[0]
    tm = pos3.shape[-1]
    return pl.pallas_call(
        _dispatch_kernel,
        grid=(T // tm,),
        in_specs=[pl.BlockSpec((1, 1, tm), lambda i: (i, 0, 0), memory_space=pltpu.SMEM),
                  pl.BlockSpec((tm, D_MODEL), lambda i: (i, 0)),
                  pl.BlockSpec(memory_space=pl.ANY)],
        out_specs=pl.BlockSpec(memory_space=pl.ANY),
        out_shape=jax.ShapeDtypeStruct((n_rows, D_MODEL), F32),
        scratch_shapes=[pltpu.SemaphoreType.DMA(())],
        input_output_aliases={2: 0},
        compiler_params=_params("arbitrary"),
        name="moe_dispatch",
    )(pos3, x2, jnp.zeros((n_rows, D_MODEL), F32))


def _collect(ys, pos3):
    T = pos3.shape[0] * pos3.shape[-1]
    tm = pos3.shape[-1]
    return pl.pallas_call(
        _collect_kernel,
        grid=(T // tm,),
        in_specs=[pl.BlockSpec((1, 1, tm), lambda i: (i, 0, 0), memory_space=pltpu.SMEM),
                  pl.BlockSpec(memory_space=pl.ANY)],
        out_specs=pl.BlockSpec((tm, D_MODEL), lambda i: (i, 0)),
        out_shape=jax.ShapeDtypeStruct((T, D_MODEL), F32),
        scratch_shapes=[pltpu.SemaphoreType.DMA(())],
        compiler_params=_params("arbitrary"),
        name="moe_collect",
    )(pos3, ys)


def _expert_kernel(tg_ref, ta_ref, tb_ref, tv_ref, x_ref, wr_ref, br_ref,
                   wga_ref, wua_ref, wda_ref, wgb_ref, wub_ref, wdb_ref, g_ref, b_ref, out_ref):
    r = pl.program_id(0)

    @pl.when(tv_ref[r] == 0)
    def _():
        out_ref[...] = jnp.zeros(out_ref.shape, F32)

    @pl.when(tv_ref[r] != 0)
    def _():
        x = x_ref[...]
        xb = x.astype(BF16)
        lg = jnp.dot(xb, wr_ref[...], preferred_element_type=F32) + br_ref[...]
        lane = _lane_iota()
        is_grp = lane < N_GROUPS
        mg = jnp.max(jnp.where(is_grp, lg, -jnp.inf), axis=-1, keepdims=True)
        den = jnp.sum(jnp.where(is_grp, jnp.exp(lg - mg), 0.0), axis=-1, keepdims=True)
        pick = lambda l: jnp.sum(jnp.where(lane == l, lg, 0.0), axis=-1, keepdims=True)
        p_top = jnp.exp(pick(tg_ref[r]) - mg) / den
        va, vb = pick(_R_SUB + ta_ref[r]), pick(_R_SUB + tb_ref[r])
        e = jnp.exp(jnp.minimum(va, vb) - jnp.maximum(va, vb))
        w_top, w_oth = 1.0 / (1.0 + e), e / (1.0 + e)
        a_top = va >= vb
        gates = (jnp.where(a_top, w_top, w_oth) * p_top, jnp.where(a_top, w_oth, w_top) * p_top)
        y = jnp.zeros(x.shape, F32)
        for gate, wg, wu, wd in ((gates[0], wga_ref, wua_ref, wda_ref), (gates[1], wgb_ref, wub_ref, wdb_ref)):
            a = jnp.dot(xb, wg[0], preferred_element_type=F32)
            u = jnp.dot(xb, wu[0], preferred_element_type=F32)
            h = (a * (1.0 / (1.0 + jnp.exp(-a))) * u).astype(BF16)
            y = y + gate * jnp.dot(h, wd[0], preferred_element_type=F32)
        out_ref[...] = _layer_norm(DN_ALPHA * x + y, g_ref[...], b_ref[...])


def _experts(xs, tile_g, tile_a, tile_b, tile_v, wr, br, wg, wu, wd, g, b):
    n_tiles = tile_g.shape[0]
    te = xs.shape[0] // n_tiles
    row = pl.BlockSpec((te, D_MODEL), lambda r, *_: (r, 0))
    full = lambda shape: pl.BlockSpec(shape, lambda r, *_: (0,) * len(shape))
    up_a = pl.BlockSpec((1, D_MODEL, D_EXPERT), lambda r, tg, ta, tb, tv: (ta[r], 0, 0))
    up_b = pl.BlockSpec((1, D_MODEL, D_EXPERT), lambda r, tg, ta, tb, tv: (tb[r], 0, 0))
    dn_a = pl.BlockSpec((1, D_EXPERT, D_MODEL), lambda r, tg, ta, tb, tv: (ta[r], 0, 0))
    dn_b = pl.BlockSpec((1, D_EXPERT, D_MODEL), lambda r, tg, ta, tb, tv: (tb[r], 0, 0))
    return pl.pallas_call(
        _expert_kernel,
        grid_spec=pltpu.PrefetchScalarGridSpec(
            num_scalar_prefetch=4, grid=(n_tiles,),
            in_specs=[row, full(wr.shape), full(br.shape), up_a, up_a, dn_a, up_b, up_b, dn_b,
                      full(g.shape), full(b.shape)],
            out_specs=row),
        out_shape=jax.ShapeDtypeStruct(xs.shape, F32),
        compiler_params=_params("arbitrary"),
        name="moe_experts",
    )(tile_g, tile_a, tile_b, tile_v, xs, wr, br, wg, wu, wd, wg, wu, wd, g, b)


def _moe_ln(x2, wr, br, wg, wu, wd, g, b):
    T = x2.shape[0]
    te = TE_MOE
    tm = min(TM_MOE, T)
    n_tiles = (T + N_CLASSES * (te - 1) + te - 1) // te
    route, counts = _route(x2, wr, br)
    cls = route[:, 0].astype(jnp.int32)
    rank = route[:, 1].astype(jnp.int32)
    cnt = counts[0, :N_CLASSES].astype(jnp.int32)
    padded = (cnt + te - 1) // te * te
    ends = jnp.cumsum(padded)
    pos3 = ((ends - padded)[cls] + rank).reshape(T // tm, 1, tm)
    tile_start = jnp.arange(n_tiles, dtype=jnp.int32) * te
    tile_cls = jnp.minimum(jnp.sum(tile_start[:, None] >= ends[None, :], axis=1), N_CLASSES - 1).astype(jnp.int32)
    tile_v = (tile_start < ends[-1]).astype(jnp.int32)
    pair = jnp.asarray(_PAIRS, jnp.int32)[tile_cls % len(_PAIRS)]
    tile_g = tile_cls // len(_PAIRS)
    tile_a = tile_g * EXPERTS_PER_GROUP + pair[:, 0]
    tile_b = tile_g * EXPERTS_PER_GROUP + pair[:, 1]
    xs = _dispatch(x2, pos3, n_tiles * te)
    ys = _experts(xs, tile_g, tile_a, tile_b, tile_v, wr, br, wg, wu, wd, g, b)
    return _collect(ys, pos3)


def _router_weights(w_grp, b_grp, w_sub, b_sub):
    pad = LANES - N_GROUPS - N_EXPERTS
    wr = jnp.concatenate([w_grp, w_sub, jnp.zeros((D_MODEL, pad), w_grp.dtype)], axis=1).astype(BF16)
    br = jnp.concatenate([b_grp, b_sub, jnp.zeros((pad,), b_grp.dtype)])[None].astype(F32)
    return wr, br


def kernel(x, positions, ev_w_in, ev_g_q, ev_g_kv, ev_w_uq, ev_w_ukv, ev_w_o, od_w_in, od_b_f, od_w_o,
           moe_w_grp, moe_b_grp, moe_w_sub, moe_b_sub, moe_w_gate, moe_w_up, moe_w_down,
           ln1_g, ln1_b, ln2_g, ln2_b):
    B, S, D = x.shape
    T = B * S
    ksel = min(TOPK_MAX, S // 4)
    x2 = x.reshape(T, D)
    tabs = _rope_tables(positions.reshape(T, 1).astype(F32))
    row = lambda v: v[None].astype(F32)
    for layer in range(DEPTH):
        j = layer // 2
        if layer % 2 == 0:
            w1, wuq, wukv = _even_weights(ev_w_in[j], ev_w_uq[j], ev_w_ukv[j])
            qa, kv, kr, qb, kb2, va, qi, ki2, wi = _even_proj(
                x2, w1, row(ev_g_q[j]), row(ev_g_kv[j]), wuq, wukv, tabs)
            o_a = _flash((qa, kv, kr), fox=False, B=B, S=S, scale=(D_NOPE + D_ROPE) ** -0.5)
            wit = jnp.pad(wi[:, :H_IDX].T, ((0, SUBLANES - H_IDX), (0, 0)))
            o_b = _dsa(qi, qb, wit, ki2, kb2, va, B=B, S=S, ksel=ksel)
            wo = ev_w_o[j].astype(BF16)
            n_a = H_MLA * D_V_MLA
            x2 = _out_ln((o_a, o_b), (wo[:n_a], wo[n_a:]), x2, row(ln1_g[layer]), row(ln1_b[layer]))
        else:
            n = H_FOX * D_FOX
            w = jnp.pad(od_w_in[j], ((0, 0), (0, LANES - H_FOX))).astype(BF16)
            bf = jnp.pad(od_b_f[j], (0, LANES - H_FOX))[None].astype(F32)
            q, k, v, lf = _odd_proj(x2, w, bf)
            c = _cumsum_seq(lf, S)
            c_rows = c.reshape(B, S, LANES)[:, :, :H_FOX].swapaxes(1, 2)
            o = _flash((q, k, v, c, c_rows), fox=True, B=B, S=S, scale=D_FOX ** -0.5)
            x2 = _out_ln((o,), (od_w_o[j].astype(BF16),), x2, row(ln1_g[layer]), row(ln1_b[layer]))
        wr, br = _router_weights(moe_w_grp[layer], moe_b_grp[layer], moe_w_sub[layer], moe_b_sub[layer])
        x2 = _moe_ln(x2, wr, br, moe_w_gate[layer].astype(BF16), moe_w_up[layer].astype(BF16),
                     moe_w_down[layer].astype(BF16), row(ln2_g[layer]), row(ln2_b[layer]))
    return x2.reshape(B, S, D)
```

```python
import functools

import jax
import jax.numpy as jnp
from jax import lax
from jax.experimental import pallas as pl
from jax.experimental.pallas import tpu as pltpu

D_MODEL = 1024
DEPTH = 4
CHUNK = 64
CHUNK_SHIFT = 6
ROPE_THETA = 10000.0
NEG_INF = -1e30
LN_EPS = 1e-5
RMS_EPS = 1e-6
H_MLA, D_NOPE, D_ROPE, D_V_MLA, Q_LORA, KV_LORA = 8, 64, 32, 64, 384, 256
H_DSA, D_DSA, H_IDX, D_IDX, TOPK_MAX = 8, 64, 4, 64, 256
H_FOX, D_FOX = 16, 64
N_GROUPS, EXPERTS_PER_GROUP, D_EXPERT = 4, 4, 512
N_EXPERTS = N_GROUPS * EXPERTS_PER_GROUP
DN_ALPHA = (2 * DEPTH) ** 0.25
EV_SPLITS = (Q_LORA, KV_LORA, D_ROPE, H_DSA * D_DSA, D_DSA, D_DSA, H_IDX * D_IDX, D_IDX, H_IDX)

LANES = 128
SUBLANES = 8
VMEM_LIMIT_BYTES = 56 * 1024 * 1024

TM_PROJ = 512
TM_MOE = 512
TE_MOE = 256
T_ATT = 512
VT_ROWS = 80
Q_DSA = 128
KC_DSA = 256

F32 = jnp.float32
BF16 = jnp.bfloat16
_NT = (((1,), (1,)), ((), ()))
_TN = (((0,), (0,)), ((), ()))


def _params(*sem):
    return pltpu.CompilerParams(dimension_semantics=sem, vmem_limit_bytes=VMEM_LIMIT_BYTES)


def _lane_iota(shape=(1, LANES)):
    return lax.broadcasted_iota(jnp.int32, shape, len(shape) - 1)


def _full_spec(shape):
    return pl.BlockSpec(shape, lambda *_: (0,) * len(shape))


def _rope_table_kernel(pos_ref, inv32_ref, sg32_ref, inv64_ref, sg64_ref,
                       c32_ref, s32_ref, c64_ref, s64_ref):
    pos = pos_ref[...]
    a32 = pos * inv32_ref[...]
    a64 = pos * inv64_ref[...]
    c32_ref[...] = jnp.cos(a32)
    s32_ref[...] = jnp.sin(a32) * sg32_ref[...]
    c64_ref[...] = jnp.cos(a64)
    s64_ref[...] = jnp.sin(a64) * sg64_ref[...]


def _rope_tables(posf):
    T = posf.shape[0]
    tm = min(T, 2048)
    lane = jnp.arange(LANES)
    inv16 = jnp.power(ROPE_THETA, -jnp.arange(0, D_ROPE, 2, dtype=F32) / D_ROPE)
    inv32 = jnp.power(ROPE_THETA, -jnp.arange(0, D_DSA, 2, dtype=F32) / D_DSA)
    in32 = (lane >= D_NOPE) & (lane < D_NOPE + D_ROPE)
    t32 = jnp.where(in32, inv16[(lane - D_NOPE) % (D_ROPE // 2)], 0.0).astype(F32)[None]
    g32 = jnp.where(in32, jnp.where(lane < D_NOPE + D_ROPE // 2, -1.0, 1.0), 0.0).astype(F32)[None]
    t64 = inv32[lane % (D_DSA // 2)].astype(F32)[None]
    g64 = jnp.where((lane % D_DSA) < D_DSA // 2, -1.0, 1.0).astype(F32)[None]
    row = pl.BlockSpec((tm, LANES), lambda i: (i, 0))
    tab = jax.ShapeDtypeStruct((T, LANES), F32)
    return pl.pallas_call(
        _rope_table_kernel,
        grid=(T // tm,),
        in_specs=[pl.BlockSpec((tm, 1), lambda i: (i, 0))] + [_full_spec((1, LANES))] * 4,
        out_specs=[row] * 4,
        out_shape=[tab] * 4,
        compiler_params=_params("parallel"),
        name="rope_tables",
    )(posf, t32, g32, t64, g64)


def _rope_block(y, c, s, half, second_half):
    partner = jnp.where(second_half, pltpu.roll(y, half, 1), pltpu.roll(y, LANES - half, 1))
    return y * c + partner * s


def _rms(x, g):
    return x * lax.rsqrt(jnp.mean(x * x, axis=-1, keepdims=True) + RMS_EPS) * g


def _layer_norm(z, g, b):
    mu = jnp.mean(z, axis=-1, keepdims=True)
    d = z - mu
    var = jnp.mean(d * d, axis=-1, keepdims=True)
    return d * lax.rsqrt(var + LN_EPS) * g + b


_E_CQ, _E_CKV, _E_KR, _E_QB, _E_KB, _E_VB, _E_QI, _E_KI, _E_WI, _E_END = (
    0, 384, 640, 768, 1280, 1408, 1536, 1792, 1920, 2048)


def _even_proj_kernel(x_ref, w1_ref, gq_ref, gkv_ref, wuq_ref, wukv_ref,
                      c32_ref, s32_ref, c64_ref, s64_ref,
                      qa_ref, kv_ref, kr_ref, qb_ref, kb_ref, va_ref, qi_ref, ki_ref, wi_ref):
    xb = x_ref[...].astype(BF16)
    lane = _lane_iota()
    sec32 = (lane >= D_NOPE + D_ROPE // 2) & (lane < D_NOPE + D_ROPE)
    sec64 = (lane % D_DSA) >= D_DSA // 2
    c32, s32, c64, s64 = c32_ref[...], s32_ref[...], c64_ref[...], s64_ref[...]

    def seg(a, b):
        return jnp.dot(xb, w1_ref[:, a:b], preferred_element_type=F32)

    def rope64_store(h, out_ref):
        for j in range(h.shape[1] // LANES):
            blk = h[:, j * LANES:(j + 1) * LANES]
            out_ref[:, j * LANES:(j + 1) * LANES] = _rope_block(
                blk, c64, s64, D_DSA // 2, sec64).astype(out_ref.dtype)

    cq = _rms(seg(_E_CQ, _E_CKV), gq_ref[...]).astype(BF16)
    q = jnp.dot(cq, wuq_ref[...], preferred_element_type=F32)
    for h in range(H_MLA):
        blk = q[:, h * LANES:(h + 1) * LANES]
        qa_ref[:, h * LANES:(h + 1) * LANES] = _rope_block(
            blk, c32, s32, D_ROPE // 2, sec32).astype(BF16)
    ckv = _rms(seg(_E_CKV, _E_KR), gkv_ref[...]).astype(BF16)
    kv_ref[...] = jnp.dot(ckv, wukv_ref[...], preferred_element_type=F32).astype(BF16)
    kr_ref[...] = _rope_block(seg(_E_KR, _E_QB), c32, s32, D_ROPE // 2, sec32).astype(BF16)
    rope64_store(seg(_E_QB, _E_KB), qb_ref)
    rope64_store(seg(_E_KB, _E_VB), kb_ref)
    va_ref[...] = jnp.where(lane < D_DSA, seg(_E_VB, _E_QI), 1.0).astype(BF16)
    rope64_store(seg(_E_QI, _E_KI), qi_ref)
    rope64_store(seg(_E_KI, _E_WI), ki_ref)
    wi_ref[...] = seg(_E_WI, _E_END)


def _even_weights(w_in, w_uq, w_ukv):
    o = [0]
    for s_ in EV_SPLITS:
        o.append(o[-1] + s_)
    cq, ckv, kr, qb, kb, vb, qi, ki, wi = (w_in[:, o[i]:o[i + 1]] for i in range(9))
    z = lambda n: jnp.zeros((D_MODEL, n), w_in.dtype)
    w1 = jnp.concatenate([
        cq, ckv,
        z(D_NOPE), kr, z(LANES - D_NOPE - D_ROPE),
        qb,
        kb, kb,
        vb, z(LANES - D_DSA),
        qi,
        ki, ki,
        wi, z(LANES - H_IDX)], axis=1).astype(BF16)
    wq = w_uq.reshape(Q_LORA, H_MLA, D_NOPE + D_ROPE)
    wq = jnp.pad(wq, ((0, 0), (0, 0), (0, LANES - D_NOPE - D_ROPE))).reshape(Q_LORA, H_MLA * LANES)
    return w1, wq.astype(BF16), w_ukv.astype(BF16)


def _even_proj(x2, w1, gq, gkv, wuq, wukv, tabs):
    T = x2.shape[0]
    tm = min(TM_PROJ, T)
    row = lambda n: pl.BlockSpec((tm, n), lambda i: (i, 0))
    outs = [(H_MLA * LANES, BF16), (H_MLA * LANES, BF16), (LANES, BF16), (H_DSA * D_DSA, BF16),
            (LANES, BF16), (LANES, BF16), (H_IDX * D_IDX, BF16), (LANES, BF16), (LANES, F32)]
    return pl.pallas_call(
        _even_proj_kernel,
        grid=(T // tm,),
        in_specs=[row(D_MODEL), _full_spec(w1.shape), _full_spec(gq.shape), _full_spec(gkv.shape),
                  _full_spec(wuq.shape), _full_spec(wukv.shape)] + [row(LANES)] * 4,
        out_specs=[row(n) for n, _ in outs],
        out_shape=[jax.ShapeDtypeStruct((T, n), d) for n, d in outs],
        compiler_params=_params("parallel"),
        name="even_proj",
    )(x2, w1, gq, gkv, wuq, wukv, *tabs)


def _odd_proj_kernel(x_ref, w_ref, bf_ref, q_ref, k_ref, v_ref, lf_ref):
    xb = x_ref[...].astype(BF16)
    n = H_FOX * D_FOX
    for j, out in enumerate((q_ref, k_ref, v_ref)):
        out[...] = jnp.dot(xb, w_ref[:, j * n:(j + 1) * n], preferred_element_type=F32).astype(BF16)
    z = jnp.dot(xb, w_ref[:, 3 * n:], preferred_element_type=F32) + bf_ref[...]
    lf_ref[...] = jnp.minimum(z, 0.0) - jnp.log1p(jnp.exp(-jnp.abs(z)))


def _odd_proj(x2, w, bf):
    T = x2.shape[0]
    tm = min(TM_PROJ, T)
    n = H_FOX * D_FOX
    row = lambda c: pl.BlockSpec((tm, c), lambda i: (i, 0))
    return pl.pallas_call(
        _odd_proj_kernel,
        grid=(T // tm,),
        in_specs=[row(D_MODEL), _full_spec(w.shape), _full_spec(bf.shape)],
        out_specs=[row(n), row(n), row(n), row(LANES)],
        out_shape=[jax.ShapeDtypeStruct((T, n), BF16)] * 3 + [jax.ShapeDtypeStruct((T, LANES), F32)],
        compiler_params=_params("parallel"),
        name="odd_proj",
    )(x2, w, bf)


def _cumsum_kernel(lf_ref, c_ref):
    a = lf_ref[...]
    rows = lax.broadcasted_iota(jnp.int32, a.shape, 0)
    k = 1
    while k < a.shape[0]:
        a = a + jnp.where(rows >= k, pltpu.roll(a, k, 0), 0.0)
        k *= 2
    c_ref[...] = a


def _cumsum_seq(lf, S):
    T = lf.shape[0]
    blk = pl.BlockSpec((S, LANES), lambda b: (b, 0))
    return pl.pallas_call(
        _cumsum_kernel, grid=(T // S,), in_specs=[blk], out_specs=blk,
        out_shape=jax.ShapeDtypeStruct((T, LANES), F32),
        compiler_params=_params("parallel"), name="forget_cumsum",
    )(lf)


def _flash_kernel(*refs, fox, scale, t, n_tiles, S):
    if fox:
        q_ref, k_ref, v_ref, cc_ref, cr_ref, o_ref, kx_ref, vt_ref, sb0_ref, sb1_ref = refs
    else:
        q_ref, kv_ref, kr_ref, o_ref, kx_ref, vt_ref, sb0_ref, sb1_ref = refs
    sb_refs = (sb0_ref, sb1_ref)
    hp, qi = pl.program_id(1), pl.program_id(2)
    lane = _lane_iota()
    lo = lane < D_FOX
    hi = lane >= D_FOX
    exp2_scale = scale * 1.4426950408889634

    @pl.when(qi == 0)
    def _():
        ones = jnp.ones((VT_ROWS - D_FOX, S), BF16)
        if fox:
            vt = jnp.transpose(v_ref[0].astype(F32)).astype(BF16)
        for hh in range(2):
            if fox:
                col = jnp.sum(jnp.where(lane == 2 * hp + hh, cc_ref[0], 0.0), axis=-1, keepdims=True)
                kx_ref[hh] = jnp.broadcast_to(col, (S, LANES))
                vt_ref[hh, :D_FOX, :] = vt[hh * D_FOX:(hh + 1) * D_FOX]
            else:
                kvh = kv_ref[0, :, hh * LANES:(hh + 1) * LANES]
                kx_ref[hh] = jnp.where(lo, kvh, kr_ref[0])
                vt_ref[hh, :D_V_MLA, :] = jnp.transpose(kvh.astype(F32))[D_NOPE:].astype(BF16)
            vt_ref[hh, D_FOX:, :] = ones

    def body(n):
        q_idx = n * t + lax.broadcasted_iota(jnp.int32, (1, t), 1)
        k_loc = lax.broadcasted_iota(jnp.int32, (t, 1), 0)
        tiles = [slice(c * t, (c + 1) * t) for c in range(n + 1)]

        outs = []
        for hh in range(2):
            if fox:
                q2 = q_ref[0] * jnp.asarray(scale, BF16)
                qh = jnp.where(lo if hh == 0 else hi, q2, jnp.zeros_like(q2))
                cq = cr_ref[0, pl.ds(2 * hp + hh, 1), n * t:(n + 1) * t]
            else:
                qh = q_ref[0, :, hh * LANES:(hh + 1) * LANES]
            m8 = jnp.full((SUBLANES, t), -jnp.inf, F32)
            for c in range(n + 1):
                kh = k_ref[0, tiles[c], :] if fox else kx_ref[hh, tiles[c], :]
                s = lax.dot_general(kh, qh, _NT, preferred_element_type=F32)
                if fox:
                    s = s + cq - jnp.concatenate([kx_ref[hh, tiles[c], :]] * (t // LANES), axis=1)
                if c == n:
                    k_idx = c * t + k_loc
                    vis = (k_idx <= q_idx) if fox else ((k_idx >> CHUNK_SHIFT) <= (q_idx >> CHUNK_SHIFT))
                    s = jnp.where(vis, s, NEG_INF)
                sb_refs[hh][tiles[c], :] = s
                m8 = jnp.maximum(m8, jnp.max(s.reshape(t // SUBLANES, SUBLANES, t), axis=0))
            m = jnp.max(m8, axis=0, keepdims=True)
            acc = jnp.zeros((VT_ROWS, t), F32)
            for c in range(n + 1):
                x = sb_refs[hh][tiles[c], :] - m
                p = jnp.exp(x) if fox else jnp.exp2(x * exp2_scale)
                acc = acc + jnp.dot(vt_ref[hh, :, tiles[c]], p.astype(BF16), preferred_element_type=F32)
            outs.append(acc[:D_FOX] / acc[D_FOX:D_FOX + 1])
        o_ref[0] = jnp.transpose(jnp.concatenate(outs, axis=0)).astype(BF16)

    for n in range(n_tiles):
        pl.when(qi == n)(functools.partial(body, n))


def _flash(args, *, fox, B, S, scale):
    t = min(T_ATT, S)
    n = S // t
    npairs = (H_FOX if fox else H_MLA) // 2
    args = tuple(a.reshape(B, S, a.shape[-1]) if a.ndim == 2 else a for a in args)
    qblk = lambda b, hp, qi: (b, qi, hp)
    seq = lambda b, hp, qi: (b, 0, hp)
    whole = lambda b, hp, qi: (b, 0, 0)
    scratch = [pltpu.VMEM((2, S, LANES), F32 if fox else BF16), pltpu.VMEM((2, VT_ROWS, S), BF16),
               pltpu.VMEM((S, t), F32), pltpu.VMEM((S, t), F32)]
    if fox:
        in_specs = [pl.BlockSpec((1, t, LANES), qblk), pl.BlockSpec((1, S, LANES), seq),
                    pl.BlockSpec((1, S, LANES), seq), pl.BlockSpec((1, S, LANES), whole),
                    pl.BlockSpec((1, H_FOX, S), whole)]
    else:
        in_specs = [pl.BlockSpec((1, t, 2 * LANES), qblk), pl.BlockSpec((1, S, 2 * LANES), seq),
                    pl.BlockSpec((1, S, LANES), whole)]
    out = pl.pallas_call(
        functools.partial(_flash_kernel, fox=fox, scale=scale, t=t, n_tiles=n, S=S),
        grid=(B, npairs, n),
        in_specs=in_specs,
        out_specs=pl.BlockSpec((1, t, LANES), qblk),
        out_shape=jax.ShapeDtypeStruct((B, S, npairs * LANES), BF16),
        scratch_shapes=scratch,
        compiler_params=_params("parallel", "parallel", "arbitrary"),
        name="fox_attention" if fox else "mla_attention",
    )(*args)
    return out.reshape(B * S, npairs * LANES)


def _dsa_kernel(qi_ref, qb_ref, wit_ref, ki_ref, kb_ref, va_ref, o_ref,
                qim_ref, qbm_ref, sc_ref, sb_ref, *, nk, i0, ksel):
    i = i0 + pl.program_id(1)
    nc = nk // KC_DSA
    lane = _lane_iota()
    lo = lane < D_DSA
    hi = lane >= D_DSA
    q_chunk = (i * Q_DSA + lane) >> CHUNK_SHIFT
    select = nk > ksel

    def chunk(c):
        return slice(c * KC_DSA, (c + 1) * KC_DSA)

    def admissible(c):
        k_idx = c * KC_DSA + lax.broadcasted_iota(jnp.int32, (KC_DSA, 1), 0)
        return (k_idx >> CHUNK_SHIFT) <= q_chunk

    for h in range(H_DSA):
        blk = qb_ref[:, (h // 2) * LANES:(h // 2 + 1) * LANES] * jnp.asarray(D_DSA ** -0.5, BF16)
        qbm_ref[h * Q_DSA:(h + 1) * Q_DSA, :] = jnp.where(lo if h % 2 == 0 else hi, blk, jnp.zeros_like(blk))

    if select:
        for h in range(H_IDX):
            blk = qi_ref[:, (h // 2) * LANES:(h // 2 + 1) * LANES]
            qim_ref[h * Q_DSA:(h + 1) * Q_DSA, :] = jnp.where(lo if h % 2 == 0 else hi, blk, jnp.zeros_like(blk))
        w_flat = jnp.concatenate([wit_ref[h:h + 1, :] for h in range(H_IDX)], axis=1)
        for c in range(nc):
            r = lax.dot_general(ki_ref[0, chunk(c), :], qim_ref[...], _NT, preferred_element_type=F32)
            r = jnp.maximum(r, 0.0) * w_flat
            acc = r[:, :LANES]
            for h in range(1, H_IDX):
                acc = acc + r[:, h * LANES:(h + 1) * LANES]
            sc_ref[chunk(c), :] = jnp.where(admissible(c), acc, NEG_INF)

        def count(mask):
            part = jnp.sum(mask.astype(F32).reshape(nk // 32, 32, LANES), axis=0)
            return jnp.sum(part, axis=0, keepdims=True)

        def as_float(key):
            return lax.bitcast_convert_type(jnp.where(key >= 0, key, key ^ 0x7FFFFFFF), F32)

        def bit_body(r, ans):
            cand = ans + lax.shift_left(jnp.int32(1), 31 - r)
            cnt = count(sc_ref[...] >= as_float(cand))
            return jnp.where(cnt >= ksel, cand, ans)
        ans = lax.fori_loop(0, 32, bit_body, jnp.full((1, LANES), jnp.iinfo(jnp.int32).min, jnp.int32))
        t = as_float(ans)
        s_all = sc_ref[...]
        need = ksel - count(s_all > t)
        n_ge = count(s_all >= t)
        k_all = lax.broadcasted_iota(jnp.int32, (nk, 1), 0)

        def tie_bound():
            def bit_body2(r, aj):
                cand = aj + lax.shift_left(jnp.int32(1), 11 - r)
                g = count((sc_ref[...] == t) & (k_all < cand))
                return jnp.where(g <= need, cand, aj)
            return lax.fori_loop(0, 12, bit_body2, jnp.zeros((1, LANES), jnp.int32))

        jmax = lax.cond(jnp.max(n_ge) > ksel, tie_bound,
                        lambda: jnp.full((1, LANES), 4095, jnp.int32))
        for c in range(nc):
            s = sc_ref[chunk(c), :]
            k_idx = c * KC_DSA + lax.broadcasted_iota(jnp.int32, (KC_DSA, 1), 0)
            sel = (s > t) | ((s == t) & (k_idx < jmax))
            sc_ref[chunk(c), :] = jnp.where(sel & admissible(c), 0.0, NEG_INF)
    else:
        for c in range(nc):
            sc_ref[chunk(c), :] = jnp.where(admissible(c), 0.0, NEG_INF)

    m8 = jnp.full((SUBLANES, H_DSA * LANES), -jnp.inf, F32)
    for c in range(nc):
        s = lax.dot_general(kb_ref[0, chunk(c), :], qbm_ref[...], _NT, preferred_element_type=F32)
        s = s + jnp.concatenate([sc_ref[chunk(c), :]] * H_DSA, axis=1)
        sb_ref[chunk(c), :] = s
        m8 = jnp.maximum(m8, jnp.max(s.reshape(KC_DSA // SUBLANES, SUBLANES, H_DSA * LANES), axis=0))
    m = jnp.max(m8, axis=0, keepdims=True)
    acc = jnp.zeros((H_DSA * Q_DSA, LANES), F32)
    for c in range(nc):
        p = jnp.exp(sb_ref[chunk(c), :] - m).astype(BF16)
        acc = acc + lax.dot_general(p, va_ref[0, chunk(c), :], _TN, preferred_element_type=F32)

    for hp in range(H_DSA // 2):
        a = acc[(2 * hp) * Q_DSA:(2 * hp + 1) * Q_DSA]
        b = acc[(2 * hp + 1) * Q_DSA:(2 * hp + 2) * Q_DSA]
        ra = a / pltpu.roll(a, D_DSA, 1)
        rb = b / pltpu.roll(b, D_DSA, 1)
        o_ref[:, hp * LANES:(hp + 1) * LANES] = jnp.where(lo, ra, pltpu.roll(rb, D_DSA, 1)).astype(BF16)


def _dsa(qi, qb, wit, ki2, kb2, va, *, B, S, ksel):
    nb = S // Q_DSA
    per = KC_DSA // Q_DSA
    seq3 = lambda a: a.reshape(B, S, LANES)
    ki3, kb3, va3 = seq3(ki2), seq3(kb2), seq3(va)
    outs = []
    for g in range(S // KC_DSA):
        nk, i0 = (g + 1) * KC_DSA, g * per
        qrow = lambda n, i0=i0: pl.BlockSpec((Q_DSA, n), lambda b, j: (b * nb + i0 + j, 0))
        seq = pl.BlockSpec((1, nk, LANES), lambda b, j: (b, 0, 0))
        o = pl.pallas_call(
            functools.partial(_dsa_kernel, nk=nk, i0=i0, ksel=ksel),
            grid=(B, per),
            in_specs=[qrow(H_IDX * D_IDX), qrow(H_DSA * D_DSA),
                      pl.BlockSpec((SUBLANES, Q_DSA), lambda b, j, i0=i0: (0, b * nb + i0 + j)),
                      seq, seq, seq],
            out_specs=pl.BlockSpec((Q_DSA, H_DSA * D_DSA), lambda b, j: (b * per + j, 0)),
            out_shape=jax.ShapeDtypeStruct((B * per * Q_DSA, H_DSA * D_DSA), BF16),
            scratch_shapes=[pltpu.VMEM((H_IDX * Q_DSA, LANES), BF16), pltpu.VMEM((H_DSA * Q_DSA, LANES), BF16),
                            pltpu.VMEM((nk, LANES), F32), pltpu.VMEM((nk, H_DSA * LANES), F32)],
            compiler_params=_params("parallel", "arbitrary"),
            name="dsa_attention_%d" % nk,
        )(qi, qb, wit, ki3, kb3, va3)
        outs.append(o.reshape(B, per * Q_DSA, H_DSA * D_DSA))
    return jnp.concatenate(outs, axis=1).reshape(B * S, H_DSA * D_DSA)


_R_SUB = N_GROUPS
_PAIRS = [(a, b) for a in range(EXPERTS_PER_GROUP) for b in range(a + 1, EXPERTS_PER_GROUP)]
N_CLASSES = N_GROUPS * len(_PAIRS)


def _out_ln_kernel(*refs, n_in):
    o_refs, w_refs = refs[:n_in], refs[n_in:2 * n_in]
    x_ref, g_ref, b_ref, wr_ref, br_ref, tri_ref, out_ref, route_ref, cnt_ref, run_ref = refs[2 * n_in:]
    y = jnp.dot(o_refs[0][...], w_refs[0][...], preferred_element_type=F32)
    for o_r, w_r in zip(o_refs[1:], w_refs[1:]):
        y = y + jnp.dot(o_r[...], w_r[...], preferred_element_type=F32)
    x_new = _layer_norm(DN_ALPHA * x_ref[...] + y, g_ref[...], b_ref[...])
    out_ref[...] = x_new
    _route_rows(x_new, wr_ref, br_ref, tri_ref, route_ref, cnt_ref, run_ref)


def _out_ln(os_, ws, x2, g, b, wr, br):
    T = x2.shape[0]
    tm = min(TM_PROJ, T)
    row = lambda c: pl.BlockSpec((tm, c), lambda i: (i, 0))
    tri = jnp.tril(jnp.ones((tm, tm), BF16), -1)
    return pl.pallas_call(
        functools.partial(_out_ln_kernel, n_in=len(os_)),
        grid=(T // tm,),
        in_specs=[row(o.shape[1]) for o in os_] + [_full_spec(w.shape) for w in ws]
                 + [row(D_MODEL), _full_spec(g.shape), _full_spec(b.shape),
                    _full_spec(wr.shape), _full_spec(br.shape), _full_spec(tri.shape)],
        out_specs=[row(D_MODEL), row(LANES), _full_spec((SUBLANES, LANES))],
        out_shape=[jax.ShapeDtypeStruct((T, D_MODEL), F32), jax.ShapeDtypeStruct((T, LANES), F32),
                   jax.ShapeDtypeStruct((SUBLANES, LANES), F32)],
        scratch_shapes=[pltpu.VMEM((1, LANES), F32)],
        compiler_params=_params("arbitrary"),
        name="out_proj_ln",
    )(*os_, *ws, x2, g, b, wr, br, tri)


def _route_rows(x, wr_ref, br_ref, tri_ref, route_ref, cnt_ref, run_ref):
    @pl.when(pl.program_id(0) == 0)
    def _():
        run_ref[...] = jnp.zeros(run_ref.shape, F32)

    lg = jnp.dot(x.astype(BF16), wr_ref[...], preferred_element_type=F32) + br_ref[...]
    lane = _lane_iota()
    lanef = lane.astype(F32)
    far = float(LANES)
    grp = jnp.where(lane < N_GROUPS, lg, -jnp.inf)
    mg = jnp.max(grp, axis=-1, keepdims=True)
    g_star = jnp.min(jnp.where(grp == mg, lanef, far), axis=-1, keepdims=True)
    lane_grp = ((lane - _R_SUB) >> 2).astype(F32)
    in_grp = (lane_grp == g_star) & (lane >= _R_SUB) & (lane < _R_SUB + N_EXPERTS)
    l1 = jnp.where(in_grp, lg, -jnp.inf)
    v1 = jnp.max(l1, axis=-1, keepdims=True)
    j1 = jnp.min(jnp.where(l1 == v1, lanef, far), axis=-1, keepdims=True)
    l2 = jnp.where(in_grp & (lanef != j1), lg, -jnp.inf)
    v2 = jnp.max(l2, axis=-1, keepdims=True)
    j2 = jnp.min(jnp.where(l2 == v2, lanef, far), axis=-1, keepdims=True)
    base = _R_SUB + EXPERTS_PER_GROUP * g_star
    ja = jnp.minimum(j1, j2) - base
    jb = jnp.maximum(j1, j2) - base
    cls = g_star * len(_PAIRS) + ja * (7.0 - ja) * 0.5 + (jb - ja - 1.0)
    onehot = lanef == cls
    before = jnp.dot(tri_ref[...], jnp.where(onehot, 1.0, 0.0).astype(BF16), preferred_element_type=F32)
    rank = jnp.sum(jnp.where(onehot, before + run_ref[...], 0.0), axis=-1, keepdims=True)
    route_ref[...] = jnp.where(lane == 0, cls, jnp.where(lane == 1, rank, 0.0))
    run_ref[...] += jnp.sum(jnp.where(onehot, 1.0, 0.0), axis=0, keepdims=True)
    cnt_ref[...] = jnp.broadcast_to(run_ref[...], cnt_ref.shape)


def _row_copy(src_ref, dst_ref, src_row, dst_row, sem):
    return pltpu.make_async_copy(src_ref.at[pl.ds(src_row, 1), :], dst_ref.at[pl.ds(dst_row, 1), :], sem)


def _dispatch_kernel(pos_ref, x_ref, init_ref, xs_ref, sem):
    del init_ref
    tm = x_ref.shape[0]

    def start(t, carry):
        _row_copy(x_ref, xs_ref, t, pos_ref[0, 0, t], sem).start()
        return carry
    lax.fori_loop(0, tm, start, 0, unroll=8)

    def wait(t, carry):
        _row_copy(x_ref, xs_ref, 0, 0, sem).wait()
        return carry
    lax.fori_loop(0, tm, wait, 0, unroll=8)


def _collect_kernel(pos_ref, ys_ref, out_ref, sem):
    tm = out_ref.shape[0]

    def start(t, carry):
        _row_copy(ys_ref, out_ref, pos_ref[0, 0, t], t, sem).start()
        return carry
    lax.fori_loop(0, tm, start, 0, unroll=8)

    def wait(t, carry):
        _row_copy(ys_ref, out_ref, 0, 0, sem).wait()
        return carry
    lax.fori_loop(0, tm, wait, 0, unroll=8)


def _dispatch(x2, pos3, n_rows):
    T = x2.shape[0]
    tm = pos3.shape[-1]
    return pl.pallas_call(
        _dispatch_kernel,
        grid=(T // tm,),
        in_specs=[pl.BlockSpec((1, 1, tm), lambda i: (i, 0, 0), memory_space=pltpu.SMEM),
                  pl.BlockSpec((tm, D_MODEL), lambda i: (i, 0)),
                  pl.BlockSpec(memory_space=pl.ANY)],
        out_specs=pl.BlockSpec(memory_space=pl.ANY),
        out_shape=jax.ShapeDtypeStruct((n_rows, D_MODEL), F32),
        scratch_shapes=[pltpu.SemaphoreType.DMA(())],
        input_output_aliases={2: 0},
        compiler_params=_params("arbitrary"),
        name="moe_dispatch",
    )(pos3, x2, jnp.zeros((n_rows, D_MODEL), F32))


def _collect(ys, pos3):
    T = pos3.shape[0] * pos3.shape[-1]
    tm = pos3.shape[-1]
    return pl.pallas_call(
        _collect_kernel,
        grid=(T // tm,),
        in_specs=[pl.BlockSpec((1, 1, tm), lambda i: (i, 0, 0), memory_space=pltpu.SMEM),
                  pl.BlockSpec(memory_space=pl.ANY)],
        out_specs=pl.BlockSpec((tm, D_MODEL), lambda i: (i, 0)),
        out_shape=jax.ShapeDtypeStruct((T, D_MODEL), F32),
        scratch_shapes=[pltpu.SemaphoreType.DMA(())],
        compiler_params=_params("arbitrary"),
        name="moe_collect",
    )(pos3, ys)


def _expert_kernel(tg_ref, ta_ref, tb_ref, tv_ref, x_ref, wr_ref, br_ref,
                   wga_ref, wua_ref, wda_ref, wgb_ref, wub_ref, wdb_ref, g_ref, b_ref, out_ref):
    r = pl.program_id(0)

    @pl.when(tv_ref[r] == 0)
    def _():
        out_ref[...] = jnp.zeros(out_ref.shape, F32)

    @pl.when(tv_ref[r] != 0)
    def _():
        x = x_ref[...]
        xb = x.astype(BF16)
        lg = jnp.dot(xb, wr_ref[...], preferred_element_type=F32) + br_ref[...]
        lane = _lane_iota()
        is_grp = lane < N_GROUPS
        mg = jnp.max(jnp.where(is_grp, lg, -jnp.inf), axis=-1, keepdims=True)
        den = jnp.sum(jnp.where(is_grp, jnp.exp(lg - mg), 0.0), axis=-1, keepdims=True)
        pick = lambda l: jnp.sum(jnp.where(lane == l, lg, 0.0), axis=-1, keepdims=True)
        p_top = jnp.exp(pick(tg_ref[r]) - mg) / den
        va, vb = pick(_R_SUB + ta_ref[r]), pick(_R_SUB + tb_ref[r])
        e = jnp.exp(jnp.minimum(va, vb) - jnp.maximum(va, vb))
        w_top, w_oth = 1.0 / (1.0 + e), e / (1.0 + e)
        a_top = va >= vb
        gates = (jnp.where(a_top, w_top, w_oth) * p_top, jnp.where(a_top, w_oth, w_top) * p_top)
        y = jnp.zeros(x.shape, F32)
        for gate, wg, wu, wd in ((gates[0], wga_ref, wua_ref, wda_ref), (gates[1], wgb_ref, wub_ref, wdb_ref)):
            a = jnp.dot(xb, wg[0], preferred_element_type=F32)
            u = jnp.dot(xb, wu[0], preferred_element_type=F32)
            h = (a * (1.0 / (1.0 + jnp.exp(-a))) * u).astype(BF16)
            y = y + gate * jnp.dot(h, wd[0], preferred_element_type=F32)
        out_ref[...] = _layer_norm(DN_ALPHA * x + y, g_ref[...], b_ref[...])


def _experts(xs, tile_g, tile_a, tile_b, tile_v, wr, br, wg, wu, wd, g, b):
    n_tiles = tile_g.shape[0]
    te = xs.shape[0] // n_tiles
    row = pl.BlockSpec((te, D_MODEL), lambda r, *_: (r, 0))
    full = lambda shape: pl.BlockSpec(shape, lambda r, *_: (0,) * len(shape))
    up_a = pl.BlockSpec((1, D_MODEL, D_EXPERT), lambda r, tg, ta, tb, tv: (ta[r], 0, 0))
    up_b = pl.BlockSpec((1, D_MODEL, D_EXPERT), lambda r, tg, ta, tb, tv: (tb[r], 0, 0))
    dn_a = pl.BlockSpec((1, D_EXPERT, D_MODEL), lambda r, tg, ta, tb, tv: (ta[r], 0, 0))
    dn_b = pl.BlockSpec((1, D_EXPERT, D_MODEL), lambda r, tg, ta, tb, tv: (tb[r], 0, 0))
    return pl.pallas_call(
        _expert_kernel,
        grid_spec=pltpu.PrefetchScalarGridSpec(
            num_scalar_prefetch=4, grid=(n_tiles,),
            in_specs=[row, full(wr.shape), full(br.shape), up_a, up_a, dn_a, up_b, up_b, dn_b,
                      full(g.shape), full(b.shape)],
            out_specs=row),
        out_shape=jax.ShapeDtypeStruct(xs.shape, F32),
        compiler_params=_params("arbitrary"),
        name="moe_experts",
    )(tile_g, tile_a, tile_b, tile_v, xs, wr, br, wg, wu, wd, wg, wu, wd, g, b)


def _slot_kernel(route_ref, start_ref, pos_ref):
    r = route_ref[...]
    lane = _lane_iota()
    cls = jnp.sum(jnp.where(lane == 0, r, 0.0), axis=-1, keepdims=True)
    rank = jnp.sum(jnp.where(lane == 1, r, 0.0), axis=-1, keepdims=True)
    start = jnp.sum(jnp.where(lane.astype(F32) == cls, start_ref[...], 0.0), axis=-1, keepdims=True)
    slot = jnp.transpose(jnp.broadcast_to(start + rank, r.shape))
    pos_ref[0] = slot[:1].astype(jnp.int32)


def _slots(route, starts, tm):
    T = route.shape[0]
    return pl.pallas_call(
        _slot_kernel,
        grid=(T // tm,),
        in_specs=[pl.BlockSpec((tm, LANES), lambda i: (i, 0)), _full_spec(starts.shape)],
        out_specs=pl.BlockSpec((1, 1, tm), lambda i: (i, 0, 0)),
        out_shape=jax.ShapeDtypeStruct((T // tm, 1, tm), jnp.int32),
        compiler_params=_params("parallel"),
        name="moe_slots",
    )(route, starts)


def _moe_ln(x2, route, counts, wr, br, wg, wu, wd, g, b):
    T = x2.shape[0]
    te = TE_MOE
    tm = min(TM_MOE, T)
    n_tiles = (T + N_CLASSES * (te - 1) + te - 1) // te
    cnt = counts[0, :N_CLASSES].astype(jnp.int32)
    padded = (cnt + te - 1) // te * te
    ends = jnp.cumsum(padded)
    starts = jnp.pad((ends - padded).astype(F32), (0, LANES - N_CLASSES))[None]
    pos3 = _slots(route, starts, tm)
    tile_start = jnp.arange(n_tiles, dtype=jnp.int32) * te
    tile_cls = jnp.minimum(jnp.sum(tile_start[:, None] >= ends[None, :], axis=1), N_CLASSES - 1).astype(jnp.int32)
    tile_v = (tile_start < ends[-1]).astype(jnp.int32)
    pair = jnp.asarray(_PAIRS, jnp.int32)[tile_cls % len(_PAIRS)]
    tile_g = tile_cls // len(_PAIRS)
    tile_a = tile_g * EXPERTS_PER_GROUP + pair[:, 0]
    tile_b = tile_g * EXPERTS_PER_GROUP + pair[:, 1]
    xs = _dispatch(x2, pos3, n_tiles * te)
    ys = _experts(xs, tile_g, tile_a, tile_b, tile_v, wr, br, wg, wu, wd, g, b)
    return _collect(ys, pos3)


def _router_weights(w_grp, b_grp, w_sub, b_sub):
    pad = LANES - N_GROUPS - N_EXPERTS
    wr = jnp.concatenate([w_grp, w_sub, jnp.zeros((D_MODEL, pad), w_grp.dtype)], axis=1).astype(BF16)
    br = jnp.concatenate([b_grp, b_sub, jnp.zeros((pad,), b_grp.dtype)])[None].astype(F32)
    return wr, br


def kernel(x, positions, ev_w_in, ev_g_q, ev_g_kv, ev_w_uq, ev_w_ukv, ev_w_o, od_w_in, od_b_f, od_w_o,
           moe_w_grp, moe_b_grp, moe_w_sub, moe_b_sub, moe_w_gate, moe_w_up, moe_w_down,
           ln1_g, ln1_b, ln2_g, ln2_b):
    B, S, D = x.shape
    T = B * S
    ksel = min(TOPK_MAX, S // 4)
    x2 = x.reshape(T, D)
    tabs = _rope_tables(positions.reshape(T, 1).astype(F32))
    row = lambda v: v[None].astype(F32)
    for layer in range(DEPTH):
        j = layer // 2
        wr, br = _router_weights(moe_w_grp[layer], moe_b_grp[layer], moe_w_sub[layer], moe_b_sub[layer])
        ln1 = (row(ln1_g[layer]), row(ln1_b[layer]), wr, br)
        if layer % 2 == 0:
            w1, wuq, wukv = _even_weights(ev_w_in[j], ev_w_uq[j], ev_w_ukv[j])
            qa, kv, kr, qb, kb2, va, qi, ki2, wi = _even_proj(
                x2, w1, row(ev_g_q[j]), row(ev_g_kv[j]), wuq, wukv, tabs)
            o_a = _flash((qa, kv, kr), fox=False, B=B, S=S, scale=(D_NOPE + D_ROPE) ** -0.5)
            wit = jnp.pad(wi[:, :H_IDX].T, ((0, SUBLANES - H_IDX), (0, 0)))
            o_b = _dsa(qi, qb, wit, ki2, kb2, va, B=B, S=S, ksel=ksel)
            wo = ev_w_o[j].astype(BF16)
            n_a = H_MLA * D_V_MLA
            x2, route, counts = _out_ln((o_a, o_b), (wo[:n_a], wo[n_a:]), x2, *ln1)
        else:
            n = H_FOX * D_FOX
            w = jnp.pad(od_w_in[j], ((0, 0), (0, LANES - H_FOX))).astype(BF16)
            bf = jnp.pad(od_b_f[j], (0, LANES - H_FOX))[None].astype(F32)
            q, k, v, lf = _odd_proj(x2, w, bf)
            c = _cumsum_seq(lf, S)
            c_rows = c.reshape(B, S, LANES)[:, :, :H_FOX].swapaxes(1, 2)
            o = _flash((q, k, v, c, c_rows), fox=True, B=B, S=S, scale=D_FOX ** -0.5)
            x2, route, counts = _out_ln((o,), (od_w_o[j].astype(BF16),), x2, *ln1)
        x2 = _moe_ln(x2, route, counts, wr, br, moe_w_gate[layer].astype(BF16), moe_w_up[layer].astype(BF16),
                     moe_w_down[layer].astype(BF16), row(ln2_g[layer]), row(ln2_b[layer]))
    return x2.reshape(B, S, D)
```

```python
import functools

import jax
import jax.numpy as jnp
from jax import lax
from jax.experimental import pallas as pl
from jax.experimental.pallas import tpu as pltpu

D_MODEL = 1024
DEPTH = 4
CHUNK = 64
CHUNK_SHIFT = 6
ROPE_THETA = 10000.0
NEG_INF = -1e30
LN_EPS = 1e-5
RMS_EPS = 1e-6
H_MLA, D_NOPE, D_ROPE, D_V_MLA, Q_LORA, KV_LORA = 8, 64, 32, 64, 384, 256
H_DSA, D_DSA, H_IDX, D_IDX, TOPK_MAX = 8, 64, 4, 64, 256
H_FOX, D_FOX = 16, 64
N_GROUPS, EXPERTS_PER_GROUP, D_EXPERT = 4, 4, 512
N_EXPERTS = N_GROUPS * EXPERTS_PER_GROUP
DN_ALPHA = (2 * DEPTH) ** 0.25
EV_SPLITS = (Q_LORA, KV_LORA, D_ROPE, H_DSA * D_DSA, D_DSA, D_DSA, H_IDX * D_IDX, D_IDX, H_IDX)

LANES = 128
SUBLANES = 8
VMEM_LIMIT_BYTES = 56 * 1024 * 1024

TM_PROJ = 512
TE_MOE = 256
T_ATT = 512
VT_ROWS = 80
Q_DSA = 128
KC_DSA = 256

F32 = jnp.float32
BF16 = jnp.bfloat16
_NT = (((1,), (1,)), ((), ()))
_TN = (((0,), (0,)), ((), ()))


def _params(*sem):
    return pltpu.CompilerParams(dimension_semantics=sem, vmem_limit_bytes=VMEM_LIMIT_BYTES)


def _lane_iota(shape=(1, LANES)):
    return lax.broadcasted_iota(jnp.int32, shape, len(shape) - 1)


def _full_spec(shape):
    return pl.BlockSpec(shape, lambda *_: (0,) * len(shape))


def _rope_table_kernel(pos_ref, inv32_ref, sg32_ref, inv64_ref, sg64_ref,
                       c32_ref, s32_ref, c64_ref, s64_ref):
    pos = pos_ref[...]
    a32 = pos * inv32_ref[...]
    a64 = pos * inv64_ref[...]
    c32_ref[...] = jnp.cos(a32)
    s32_ref[...] = jnp.sin(a32) * sg32_ref[...]
    c64_ref[...] = jnp.cos(a64)
    s64_ref[...] = jnp.sin(a64) * sg64_ref[...]


def _rope_tables(posf):
    T = posf.shape[0]
    tm = min(T, 2048)
    lane = jnp.arange(LANES)
    inv16 = jnp.power(ROPE_THETA, -jnp.arange(0, D_ROPE, 2, dtype=F32) / D_ROPE)
    inv32 = jnp.power(ROPE_THETA, -jnp.arange(0, D_DSA, 2, dtype=F32) / D_DSA)
    in32 = (lane >= D_NOPE) & (lane < D_NOPE + D_ROPE)
    t32 = jnp.where(in32, inv16[(lane - D_NOPE) % (D_ROPE // 2)], 0.0).astype(F32)[None]
    g32 = jnp.where(in32, jnp.where(lane < D_NOPE + D_ROPE // 2, -1.0, 1.0), 0.0).astype(F32)[None]
    t64 = inv32[lane % (D_DSA // 2)].astype(F32)[None]
    g64 = jnp.where((lane % D_DSA) < D_DSA // 2, -1.0, 1.0).astype(F32)[None]
    row = pl.BlockSpec((tm, LANES), lambda i: (i, 0))
    tab = jax.ShapeDtypeStruct((T, LANES), F32)
    return pl.pallas_call(
        _rope_table_kernel,
        grid=(T // tm,),
        in_specs=[pl.BlockSpec((tm, 1), lambda i: (i, 0))] + [_full_spec((1, LANES))] * 4,
        out_specs=[row] * 4,
        out_shape=[tab] * 4,
        compiler_params=_params("parallel"),
        name="rope_tables",
    )(posf, t32, g32, t64, g64)


def _rope_block(y, c, s, half, second_half):
    partner = jnp.where(second_half, pltpu.roll(y, half, 1), pltpu.roll(y, LANES - half, 1))
    return y * c + partner * s


def _rms(x, g):
    return x * lax.rsqrt(jnp.mean(x * x, axis=-1, keepdims=True) + RMS_EPS) * g


def _layer_norm(z, g, b):
    mu = jnp.mean(z, axis=-1, keepdims=True)
    d = z - mu
    var = jnp.mean(d * d, axis=-1, keepdims=True)
    return d * lax.rsqrt(var + LN_EPS) * g + b


_E_CQ, _E_CKV, _E_KR, _E_QB, _E_KB, _E_VB, _E_QI, _E_KI, _E_WI, _E_END = (
    0, 384, 640, 768, 1280, 1408, 1536, 1792, 1920, 2048)


def _even_proj_kernel(x_ref, w1_ref, gq_ref, gkv_ref, wuq_ref, wukv_ref,
                      c32_ref, s32_ref, c64_ref, s64_ref,
                      qa_ref, kv_ref, kr_ref, qb_ref, kb_ref, va_ref, qi_ref, ki_ref, wi_ref):
    xb = x_ref[...].astype(BF16)
    lane = _lane_iota()
    sec32 = (lane >= D_NOPE + D_ROPE // 2) & (lane < D_NOPE + D_ROPE)
    sec64 = (lane % D_DSA) >= D_DSA // 2
    c32, s32, c64, s64 = c32_ref[...], s32_ref[...], c64_ref[...], s64_ref[...]

    def seg(a, b):
        return jnp.dot(xb, w1_ref[:, a:b], preferred_element_type=F32)

    def rope64_store(h, out_ref):
        for j in range(h.shape[1] // LANES):
            blk = h[:, j * LANES:(j + 1) * LANES]
            out_ref[:, j * LANES:(j + 1) * LANES] = _rope_block(
                blk, c64, s64, D_DSA // 2, sec64).astype(out_ref.dtype)

    cq = _rms(seg(_E_CQ, _E_CKV), gq_ref[...]).astype(BF16)
    q = jnp.dot(cq, wuq_ref[...], preferred_element_type=F32)
    for h in range(H_MLA):
        blk = q[:, h * LANES:(h + 1) * LANES]
        qa_ref[:, h * LANES:(h + 1) * LANES] = _rope_block(
            blk, c32, s32, D_ROPE // 2, sec32).astype(BF16)
    ckv = _rms(seg(_E_CKV, _E_KR), gkv_ref[...]).astype(BF16)
    kv_ref[...] = jnp.dot(ckv, wukv_ref[...], preferred_element_type=F32).astype(BF16)
    kr_ref[...] = _rope_block(seg(_E_KR, _E_QB), c32, s32, D_ROPE // 2, sec32).astype(BF16)
    rope64_store(seg(_E_QB, _E_KB), qb_ref)
    rope64_store(seg(_E_KB, _E_VB), kb_ref)
    va_ref[...] = jnp.where(lane < D_DSA, seg(_E_VB, _E_QI), 1.0).astype(BF16)
    rope64_store(seg(_E_QI, _E_KI), qi_ref)
    rope64_store(seg(_E_KI, _E_WI), ki_ref)
    wi_ref[...] = seg(_E_WI, _E_END)


def _even_weights(w_in, w_uq, w_ukv):
    o = [0]
    for s_ in EV_SPLITS:
        o.append(o[-1] + s_)
    cq, ckv, kr, qb, kb, vb, qi, ki, wi = (w_in[:, o[i]:o[i + 1]] for i in range(9))
    z = lambda n: jnp.zeros((D_MODEL, n), w_in.dtype)
    w1 = jnp.concatenate([
        cq, ckv,
        z(D_NOPE), kr, z(LANES - D_NOPE - D_ROPE),
        qb,
        kb, kb,
        vb, z(LANES - D_DSA),
        qi,
        ki, ki,
        wi, z(LANES - H_IDX)], axis=1).astype(BF16)
    wq = w_uq.reshape(Q_LORA, H_MLA, D_NOPE + D_ROPE)
    wq = jnp.pad(wq, ((0, 0), (0, 0), (0, LANES - D_NOPE - D_ROPE))).reshape(Q_LORA, H_MLA * LANES)
    return w1, wq.astype(BF16), w_ukv.astype(BF16)


def _even_proj(x2, w1, gq, gkv, wuq, wukv, tabs):
    T = x2.shape[0]
    tm = min(TM_PROJ, T)
    row = lambda n: pl.BlockSpec((tm, n), lambda i: (i, 0))
    outs = [(H_MLA * LANES, BF16), (H_MLA * LANES, BF16), (LANES, BF16), (H_DSA * D_DSA, BF16),
            (LANES, BF16), (LANES, BF16), (H_IDX * D_IDX, BF16), (LANES, BF16), (LANES, F32)]
    return pl.pallas_call(
        _even_proj_kernel,
        grid=(T // tm,),
        in_specs=[row(D_MODEL), _full_spec(w1.shape), _full_spec(gq.shape), _full_spec(gkv.shape),
                  _full_spec(wuq.shape), _full_spec(wukv.shape)] + [row(LANES)] * 4,
        out_specs=[row(n) for n, _ in outs],
        out_shape=[jax.ShapeDtypeStruct((T, n), d) for n, d in outs],
        compiler_params=_params("parallel"),
        name="even_proj",
    )(x2, w1, gq, gkv, wuq, wukv, *tabs)


def _odd_proj_kernel(x_ref, w_ref, bf_ref, q_ref, k_ref, v_ref, lf_ref):
    xb = x_ref[...].astype(BF16)
    n = H_FOX * D_FOX
    for j, out in enumerate((q_ref, k_ref, v_ref)):
        out[...] = jnp.dot(xb, w_ref[:, j * n:(j + 1) * n], preferred_element_type=F32).astype(BF16)
    z = jnp.dot(xb, w_ref[:, 3 * n:], preferred_element_type=F32) + bf_ref[...]
    lf_ref[...] = jnp.minimum(z, 0.0) - jnp.log1p(jnp.exp(-jnp.abs(z)))


def _odd_proj(x2, w, bf):
    T = x2.shape[0]
    tm = min(TM_PROJ, T)
    n = H_FOX * D_FOX
    row = lambda c: pl.BlockSpec((tm, c), lambda i: (i, 0))
    return pl.pallas_call(
        _odd_proj_kernel,
        grid=(T // tm,),
        in_specs=[row(D_MODEL), _full_spec(w.shape), _full_spec(bf.shape)],
        out_specs=[row(n), row(n), row(n), row(LANES)],
        out_shape=[jax.ShapeDtypeStruct((T, n), BF16)] * 3 + [jax.ShapeDtypeStruct((T, LANES), F32)],
        compiler_params=_params("parallel"),
        name="odd_proj",
    )(x2, w, bf)


def _cumsum_kernel(lf_ref, c_ref):
    a = lf_ref[...]
    rows = lax.broadcasted_iota(jnp.int32, a.shape, 0)
    k = 1
    while k < a.shape[0]:
        a = a + jnp.where(rows >= k, pltpu.roll(a, k, 0), 0.0)
        k *= 2
    c_ref[...] = a


def _cumsum_seq(lf, S):
    T = lf.shape[0]
    blk = pl.BlockSpec((S, LANES), lambda b: (b, 0))
    return pl.pallas_call(
        _cumsum_kernel, grid=(T // S,), in_specs=[blk], out_specs=blk,
        out_shape=jax.ShapeDtypeStruct((T, LANES), F32),
        compiler_params=_params("parallel"), name="forget_cumsum",
    )(lf)


def _flash_kernel(*refs, fox, scale, t, n_tiles, S):
    if fox:
        q_ref, k_ref, v_ref, cc_ref, cr_ref, o_ref, kx_ref, vt_ref, sb0_ref, sb1_ref = refs
    else:
        q_ref, kv_ref, kr_ref, o_ref, kx_ref, vt_ref, sb0_ref, sb1_ref = refs
    sb_refs = (sb0_ref, sb1_ref)
    hp, qi = pl.program_id(1), pl.program_id(2)
    lane = _lane_iota()
    lo = lane < D_FOX
    hi = lane >= D_FOX
    exp2_scale = scale * 1.4426950408889634

    @pl.when(qi == 0)
    def _():
        ones = jnp.ones((VT_ROWS - D_FOX, S), BF16)
        if fox:
            vt = jnp.transpose(v_ref[0].astype(F32)).astype(BF16)
        for hh in range(2):
            if fox:
                col = jnp.sum(jnp.where(lane == 2 * hp + hh, cc_ref[0], 0.0), axis=-1, keepdims=True)
                kx_ref[hh] = jnp.broadcast_to(col, (S, LANES))
                vt_ref[hh, :D_FOX, :] = vt[hh * D_FOX:(hh + 1) * D_FOX]
            else:
                kvh = kv_ref[0, :, hh * LANES:(hh + 1) * LANES]
                kx_ref[hh] = jnp.where(lo, kvh, kr_ref[0])
                vt_ref[hh, :D_V_MLA, :] = jnp.transpose(kvh.astype(F32))[D_NOPE:].astype(BF16)
            vt_ref[hh, D_FOX:, :] = ones

    def body(n):
        half = t // 2
        blocks = [(c * t, t, 0, False) for c in range(n)] + [(n * t, half, 0, True), (n * t + half, half, half, True)]

        def on_lanes(full, q0, part, op):
            if q0 == 0:
                return op(full, part)
            return jnp.concatenate([full[:, :q0], op(full[:, q0:], part)], axis=1)

        outs = []
        for hh in range(2):
            if fox:
                q2 = q_ref[0] * jnp.asarray(scale, BF16)
                qh = jnp.where(lo if hh == 0 else hi, q2, jnp.zeros_like(q2))
                cq = cr_ref[0, pl.ds(2 * hp + hh, 1), n * t:(n + 1) * t]
            else:
                qh = q_ref[0, :, hh * LANES:(hh + 1) * LANES]
            m8 = jnp.full((SUBLANES, t), -jnp.inf, F32)
            for k0, nk, q0, masked in blocks:
                nq = t - q0
                kh = k_ref[0, k0:k0 + nk, :] if fox else kx_ref[hh, k0:k0 + nk, :]
                s = lax.dot_general(kh, qh[q0:], _NT, preferred_element_type=F32)
                if fox:
                    s = s + cq[:, q0:] - jnp.concatenate([kx_ref[hh, k0:k0 + nk, :]] * (nq // LANES), axis=1)
                if masked:
                    k_idx = k0 + lax.broadcasted_iota(jnp.int32, (nk, 1), 0)
                    q_idx = n * t + q0 + lax.broadcasted_iota(jnp.int32, (1, nq), 1)
                    vis = (k_idx <= q_idx) if fox else ((k_idx >> CHUNK_SHIFT) <= (q_idx >> CHUNK_SHIFT))
                    s = jnp.where(vis, s, NEG_INF)
                sb_refs[hh][k0:k0 + nk, q0:] = s
                m8 = on_lanes(m8, q0, jnp.max(s.reshape(nk // SUBLANES, SUBLANES, nq), axis=0), jnp.maximum)
            m = jnp.max(m8, axis=0, keepdims=True)
            acc = jnp.zeros((VT_ROWS, t), F32)
            for k0, nk, q0, masked in blocks:
                x = sb_refs[hh][k0:k0 + nk, q0:] - m[:, q0:]
                p = jnp.exp(x) if fox else jnp.exp2(x * exp2_scale)
                pv = jnp.dot(vt_ref[hh, :, k0:k0 + nk], p.astype(BF16), preferred_element_type=F32)
                acc = on_lanes(acc, q0, pv, jnp.add)
            outs.append(acc[:D_FOX] / acc[D_FOX:D_FOX + 1])
        o_ref[0] = jnp.transpose(jnp.concatenate(outs, axis=0)).astype(BF16)

    for n in range(n_tiles):
        pl.when(qi == n)(functools.partial(body, n))


def _flash(args, *, fox, B, S, scale):
    t = min(T_ATT, S)
    n = S // t
    npairs = (H_FOX if fox else H_MLA) // 2
    args = tuple(a.reshape(B, S, a.shape[-1]) if a.ndim == 2 else a for a in args)
    qblk = lambda b, hp, qi: (b, qi, hp)
    seq = lambda b, hp, qi: (b, 0, hp)
    whole = lambda b, hp, qi: (b, 0, 0)
    scratch = [pltpu.VMEM((2, S, LANES), F32 if fox else BF16), pltpu.VMEM((2, VT_ROWS, S), BF16),
               pltpu.VMEM((S, t), F32), pltpu.VMEM((S, t), F32)]
    if fox:
        in_specs = [pl.BlockSpec((1, t, LANES), qblk), pl.BlockSpec((1, S, LANES), seq),
                    pl.BlockSpec((1, S, LANES), seq), pl.BlockSpec((1, S, LANES), whole),
                    pl.BlockSpec((1, H_FOX, S), whole)]
    else:
        in_specs = [pl.BlockSpec((1, t, 2 * LANES), qblk), pl.BlockSpec((1, S, 2 * LANES), seq),
                    pl.BlockSpec((1, S, LANES), whole)]
    out = pl.pallas_call(
        functools.partial(_flash_kernel, fox=fox, scale=scale, t=t, n_tiles=n, S=S),
        grid=(B, npairs, n),
        in_specs=in_specs,
        out_specs=pl.BlockSpec((1, t, LANES), qblk),
        out_shape=jax.ShapeDtypeStruct((B, S, npairs * LANES), BF16),
        scratch_shapes=scratch,
        compiler_params=_params("parallel", "parallel", "arbitrary"),
        name="fox_attention" if fox else "mla_attention",
    )(*args)
    return out.reshape(B * S, npairs * LANES)


def _dsa_kernel(qi_ref, qb_ref, wit_ref, ki_ref, kb_ref, va_ref, o_ref,
                qim_ref, qbm_ref, sc_ref, sb_ref, *, nk, i0, ksel):
    i = i0 + pl.program_id(1)
    nc = nk // KC_DSA
    lane = _lane_iota()
    lo = lane < D_DSA
    hi = lane >= D_DSA
    q_chunk = (i * Q_DSA + lane) >> CHUNK_SHIFT
    select = nk > ksel

    def chunk(c):
        return slice(c * KC_DSA, (c + 1) * KC_DSA)

    def admissible(c):
        k_idx = c * KC_DSA + lax.broadcasted_iota(jnp.int32, (KC_DSA, 1), 0)
        return (k_idx >> CHUNK_SHIFT) <= q_chunk

    for h in range(H_DSA):
        blk = qb_ref[:, (h // 2) * LANES:(h // 2 + 1) * LANES] * jnp.asarray(D_DSA ** -0.5, BF16)
        qbm_ref[h * Q_DSA:(h + 1) * Q_DSA, :] = jnp.where(lo if h % 2 == 0 else hi, blk, jnp.zeros_like(blk))

    if select:
        for h in range(H_IDX):
            blk = qi_ref[:, (h // 2) * LANES:(h // 2 + 1) * LANES]
            qim_ref[h * Q_DSA:(h + 1) * Q_DSA, :] = jnp.where(lo if h % 2 == 0 else hi, blk, jnp.zeros_like(blk))
        w_flat = jnp.concatenate([wit_ref[h:h + 1, :] for h in range(H_IDX)], axis=1)
        for c in range(nc):
            r = lax.dot_general(ki_ref[0, chunk(c), :], qim_ref[...], _NT, preferred_element_type=F32)
            r = jnp.maximum(r, 0.0) * w_flat
            acc = r[:, :LANES]
            for h in range(1, H_IDX):
                acc = acc + r[:, h * LANES:(h + 1) * LANES]
            sc_ref[chunk(c), :] = jnp.where(admissible(c), acc, NEG_INF)

        def count(mask):
            part = jnp.sum(mask.astype(F32).reshape(nk // 32, 32, LANES), axis=0)
            return jnp.sum(part, axis=0, keepdims=True)

        def as_float(key):
            return lax.bitcast_convert_type(jnp.where(key >= 0, key, key ^ 0x7FFFFFFF), F32)

        def bit_body(r, ans):
            cand = ans + lax.shift_left(jnp.int32(1), 31 - r)
            cnt = count(sc_ref[...] >= as_float(cand))
            return jnp.where(cnt >= ksel, cand, ans)
        ans = lax.fori_loop(0, 32, bit_body, jnp.full((1, LANES), jnp.iinfo(jnp.int32).min, jnp.int32))
        t = as_float(ans)
        s_all = sc_ref[...]
        need = ksel - count(s_all > t)
        n_ge = count(s_all >= t)
        k_all = lax.broadcasted_iota(jnp.int32, (nk, 1), 0)

        def tie_bound():
            def bit_body2(r, aj):
                cand = aj + lax.shift_left(jnp.int32(1), 11 - r)
                g = count((sc_ref[...] == t) & (k_all < cand))
                return jnp.where(g <= need, cand, aj)
            return lax.fori_loop(0, 12, bit_body2, jnp.zeros((1, LANES), jnp.int32))

        jmax = lax.cond(jnp.max(n_ge) > ksel, tie_bound,
                        lambda: jnp.full((1, LANES), 4095, jnp.int32))
        for c in range(nc):
            s = sc_ref[chunk(c), :]
            k_idx = c * KC_DSA + lax.broadcasted_iota(jnp.int32, (KC_DSA, 1), 0)
            sel = (s > t) | ((s == t) & (k_idx < jmax))
            sc_ref[chunk(c), :] = jnp.where(sel & admissible(c), 0.0, NEG_INF)
    else:
        for c in range(nc):
            sc_ref[chunk(c), :] = jnp.where(admissible(c), 0.0, NEG_INF)

    m8 = jnp.full((SUBLANES, H_DSA * LANES), -jnp.inf, F32)
    for c in range(nc):
        s = lax.dot_general(kb_ref[0, chunk(c), :], qbm_ref[...], _NT, preferred_element_type=F32)
        s = s + jnp.concatenate([sc_ref[chunk(c), :]] * H_DSA, axis=1)
        sb_ref[chunk(c), :] = s
        m8 = jnp.maximum(m8, jnp.max(s.reshape(KC_DSA // SUBLANES, SUBLANES, H_DSA * LANES), axis=0))
    m = jnp.max(m8, axis=0, keepdims=True)
    acc = jnp.zeros((H_DSA * Q_DSA, LANES), F32)
    for c in range(nc):
        p = jnp.exp(sb_ref[chunk(c), :] - m).astype(BF16)
        acc = acc + lax.dot_general(p, va_ref[0, chunk(c), :], _TN, preferred_element_type=F32)

    for hp in range(H_DSA // 2):
        a = acc[(2 * hp) * Q_DSA:(2 * hp + 1) * Q_DSA]
        b = acc[(2 * hp + 1) * Q_DSA:(2 * hp + 2) * Q_DSA]
        ra = a / pltpu.roll(a, D_DSA, 1)
        rb = b / pltpu.roll(b, D_DSA, 1)
        o_ref[:, hp * LANES:(hp + 1) * LANES] = jnp.where(lo, ra, pltpu.roll(rb, D_DSA, 1)).astype(BF16)


def _dsa(qi, qb, wit, ki2, kb2, va, *, B, S, ksel):
    nb = S // Q_DSA
    per = KC_DSA // Q_DSA
    seq3 = lambda a: a.reshape(B, S, LANES)
    ki3, kb3, va3 = seq3(ki2), seq3(kb2), seq3(va)
    outs = []
    for g in range(S // KC_DSA):
        nk, i0 = (g + 1) * KC_DSA, g * per
        qrow = lambda n, i0=i0: pl.BlockSpec((Q_DSA, n), lambda b, j: (b * nb + i0 + j, 0))
        seq = pl.BlockSpec((1, nk, LANES), lambda b, j: (b, 0, 0))
        o = pl.pallas_call(
            functools.partial(_dsa_kernel, nk=nk, i0=i0, ksel=ksel),
            grid=(B, per),
            in_specs=[qrow(H_IDX * D_IDX), qrow(H_DSA * D_DSA),
                      pl.BlockSpec((SUBLANES, Q_DSA), lambda b, j, i0=i0: (0, b * nb + i0 + j)),
                      seq, seq, seq],
            out_specs=pl.BlockSpec((Q_DSA, H_DSA * D_DSA), lambda b, j: (b * per + j, 0)),
            out_shape=jax.ShapeDtypeStruct((B * per * Q_DSA, H_DSA * D_DSA), BF16),
            scratch_shapes=[pltpu.VMEM((H_IDX * Q_DSA, LANES), BF16), pltpu.VMEM((H_DSA * Q_DSA, LANES), BF16),
                            pltpu.VMEM((nk, LANES), F32), pltpu.VMEM((nk, H_DSA * LANES), F32)],
            compiler_params=_params("parallel", "arbitrary"),
            name="dsa_attention_%d" % nk,
        )(qi, qb, wit, ki3, kb3, va3)
        outs.append(o.reshape(B, per * Q_DSA, H_DSA * D_DSA))
    return jnp.concatenate(outs, axis=1).reshape(B * S, H_DSA * D_DSA)


_R_SUB = N_GROUPS
_PAIRS = [(a, b) for a in range(EXPERTS_PER_GROUP) for b in range(a + 1, EXPERTS_PER_GROUP)]
N_CLASSES = N_GROUPS * len(_PAIRS)


def _out_ln_kernel(*refs, n_in):
    o_refs, w_refs = refs[:n_in], refs[n_in:2 * n_in]
    x_ref, g_ref, b_ref, wrt_ref, brt_ref, tri_ref, out_ref, route_ref, cnt_ref, run_ref = refs[2 * n_in:]
    y = jnp.dot(o_refs[0][...], w_refs[0][...], preferred_element_type=F32)
    for o_r, w_r in zip(o_refs[1:], w_refs[1:]):
        y = y + jnp.dot(o_r[...], w_r[...], preferred_element_type=F32)
    x_new = _layer_norm(DN_ALPHA * x_ref[...] + y, g_ref[...], b_ref[...])
    out_ref[...] = x_new
    _route_rows(x_new, wrt_ref, brt_ref, tri_ref, route_ref, cnt_ref, run_ref)


def _out_ln(os_, ws, x2, g, b, wr, br):
    T = x2.shape[0]
    tm = min(TM_PROJ, T)
    row = lambda c: pl.BlockSpec((tm, c), lambda i: (i, 0))
    tri = jnp.triu(jnp.ones((tm, tm), BF16), 1)
    wrt, brt = wr.T, br.T
    return pl.pallas_call(
        functools.partial(_out_ln_kernel, n_in=len(os_)),
        grid=(T // tm,),
        in_specs=[row(o.shape[1]) for o in os_] + [_full_spec(w.shape) for w in ws]
                 + [row(D_MODEL), _full_spec(g.shape), _full_spec(b.shape),
                    _full_spec(wrt.shape), _full_spec(brt.shape), _full_spec(tri.shape)],
        out_specs=[row(D_MODEL), pl.BlockSpec((1, SUBLANES, tm), lambda i: (i, 0, 0)), _full_spec((LANES, LANES))],
        out_shape=[jax.ShapeDtypeStruct((T, D_MODEL), F32), jax.ShapeDtypeStruct((T // tm, SUBLANES, tm), F32),
                   jax.ShapeDtypeStruct((LANES, LANES), F32)],
        scratch_shapes=[pltpu.VMEM((LANES, 1), F32)],
        compiler_params=_params("arbitrary"),
        name="out_proj_ln",
    )(*os_, *ws, x2, g, b, wrt, brt, tri)


def _route_rows(x, wrt_ref, brt_ref, tri_ref, route_ref, cnt_ref, run_ref):
    @pl.when(pl.program_id(0) == 0)
    def _():
        run_ref[...] = jnp.zeros(run_ref.shape, F32)

    tm = x.shape[0]
    lg = lax.dot_general(wrt_ref[...], x.astype(BF16), _NT, preferred_element_type=F32) + brt_ref[...]
    r = lax.broadcasted_iota(jnp.int32, (LANES, 1), 0)
    rf = r.astype(F32)
    far = float(LANES)
    grp = jnp.where(r < N_GROUPS, lg, -jnp.inf)
    mg = jnp.max(grp, axis=0, keepdims=True)
    g_star = jnp.min(jnp.where(grp == mg, rf, far), axis=0, keepdims=True)
    row_grp = ((r - _R_SUB) >> 2).astype(F32)
    in_grp = (row_grp == g_star) & (r >= _R_SUB) & (r < _R_SUB + N_EXPERTS)
    l1 = jnp.where(in_grp, lg, -jnp.inf)
    v1 = jnp.max(l1, axis=0, keepdims=True)
    j1 = jnp.min(jnp.where(l1 == v1, rf, far), axis=0, keepdims=True)
    l2 = jnp.where(in_grp & (rf != j1), lg, -jnp.inf)
    v2 = jnp.max(l2, axis=0, keepdims=True)
    j2 = jnp.min(jnp.where(l2 == v2, rf, far), axis=0, keepdims=True)
    base = _R_SUB + EXPERTS_PER_GROUP * g_star
    ja = jnp.minimum(j1, j2) - base
    jb = jnp.maximum(j1, j2) - base
    cls = g_star * len(_PAIRS) + ja * (7.0 - ja) * 0.5 + (jb - ja - 1.0)
    onehot = rf == cls
    ones = jnp.where(onehot, 1.0, 0.0)
    before = jnp.dot(ones.astype(BF16), tri_ref[...], preferred_element_type=F32)
    rank = jnp.sum(jnp.where(onehot, before + run_ref[...], 0.0), axis=0, keepdims=True)
    row8 = lax.broadcasted_iota(jnp.int32, (SUBLANES, 1), 0)
    route_ref[0] = jnp.where(row8 == 0, cls, jnp.where(row8 == 1, rank, 0.0))
    run_ref[...] += jnp.sum(ones, axis=1, keepdims=True)
    cnt_ref[...] = jnp.broadcast_to(run_ref[...], cnt_ref.shape)


def _row_copy(src_ref, dst_ref, src_row, dst_row, sem):
    return pltpu.make_async_copy(src_ref.at[pl.ds(src_row, 1), :], dst_ref.at[pl.ds(dst_row, 1), :], sem)


def _dispatch_kernel(pos_ref, x_ref, init_ref, xs_ref, sem):
    del init_ref
    tm = x_ref.shape[0]

    def start(g, carry):
        base = pl.multiple_of(g * SUBLANES, SUBLANES)
        for j in range(SUBLANES):
            _row_copy(x_ref, xs_ref, base + j, pos_ref[0, 0, base + j], sem).start()
        return carry
    lax.fori_loop(0, tm // SUBLANES, start, 0)

    def wait(t, carry):
        _row_copy(x_ref, xs_ref, 0, 0, sem).wait()
        return carry
    lax.fori_loop(0, tm, wait, 0, unroll=8)


def _collect_kernel(pos_ref, ys_ref, out_ref, sem):
    tm = out_ref.shape[0]

    def start(g, carry):
        base = pl.multiple_of(g * SUBLANES, SUBLANES)
        for j in range(SUBLANES):
            _row_copy(ys_ref, out_ref, pos_ref[0, 0, base + j], base + j, sem).start()
        return carry
    lax.fori_loop(0, tm // SUBLANES, start, 0)

    def wait(t, carry):
        _row_copy(ys_ref, out_ref, 0, 0, sem).wait()
        return carry
    lax.fori_loop(0, tm, wait, 0, unroll=8)


def _dispatch(x2, pos3, n_rows):
    T = x2.shape[0]
    tm = pos3.shape[-1]
    return pl.pallas_call(
        _dispatch_kernel,
        grid=(T // tm,),
        in_specs=[pl.BlockSpec((1, 1, tm), lambda i: (i, 0, 0), memory_space=pltpu.SMEM),
                  pl.BlockSpec((tm, D_MODEL), lambda i: (i, 0)),
                  pl.BlockSpec(memory_space=pl.ANY)],
        out_specs=pl.BlockSpec(memory_space=pl.ANY),
        out_shape=jax.ShapeDtypeStruct((n_rows, D_MODEL), F32),
        scratch_shapes=[pltpu.SemaphoreType.DMA(())],
        input_output_aliases={2: 0},
        compiler_params=_params("arbitrary"),
        name="moe_dispatch",
    )(pos3, x2, jnp.zeros((n_rows, D_MODEL), F32))


def _collect(ys, pos3):
    T = pos3.shape[0] * pos3.shape[-1]
    tm = pos3.shape[-1]
    return pl.pallas_call(
        _collect_kernel,
        grid=(T // tm,),
        in_specs=[pl.BlockSpec((1, 1, tm), lambda i: (i, 0, 0), memory_space=pltpu.SMEM),
                  pl.BlockSpec(memory_space=pl.ANY)],
        out_specs=pl.BlockSpec((tm, D_MODEL), lambda i: (i, 0)),
        out_shape=jax.ShapeDtypeStruct((T, D_MODEL), F32),
        scratch_shapes=[pltpu.SemaphoreType.DMA(())],
        compiler_params=_params("arbitrary"),
        name="moe_collect",
    )(pos3, ys)


def _expert_kernel(tg_ref, ta_ref, tb_ref, tv_ref, x_ref, wr_ref, br_ref,
                   wga_ref, wua_ref, wda_ref, wgb_ref, wub_ref, wdb_ref, g_ref, b_ref, out_ref):
    r = pl.program_id(0)

    @pl.when(tv_ref[r] == 0)
    def _():
        out_ref[...] = jnp.zeros(out_ref.shape, F32)

    @pl.when(tv_ref[r] != 0)
    def _():
        x = x_ref[...]
        xb = x.astype(BF16)
        lg = jnp.dot(xb, wr_ref[...], preferred_element_type=F32) + br_ref[...]
        lane = _lane_iota()
        is_grp = lane < N_GROUPS
        mg = jnp.max(jnp.where(is_grp, lg, -jnp.inf), axis=-1, keepdims=True)
        den = jnp.sum(jnp.where(is_grp, jnp.exp(lg - mg), 0.0), axis=-1, keepdims=True)
        pick = lambda l: jnp.sum(jnp.where(lane == l, lg, 0.0), axis=-1, keepdims=True)
        p_top = jnp.exp(pick(tg_ref[r]) - mg) / den
        va, vb = pick(_R_SUB + ta_ref[r]), pick(_R_SUB + tb_ref[r])
        e = jnp.exp(jnp.minimum(va, vb) - jnp.maximum(va, vb))
        w_top, w_oth = 1.0 / (1.0 + e), e / (1.0 + e)
        a_top = va >= vb
        gates = (jnp.where(a_top, w_top, w_oth) * p_top, jnp.where(a_top, w_oth, w_top) * p_top)
        y = jnp.zeros(x.shape, F32)
        for gate, wg, wu, wd in ((gates[0], wga_ref, wua_ref, wda_ref), (gates[1], wgb_ref, wub_ref, wdb_ref)):
            a = jnp.dot(xb, wg[0], preferred_element_type=F32)
            u = jnp.dot(xb, wu[0], preferred_element_type=F32)
            h = (a * (1.0 / (1.0 + jnp.exp(-a))) * u).astype(BF16)
            y = y + gate * jnp.dot(h, wd[0], preferred_element_type=F32)
        out_ref[...] = _layer_norm(DN_ALPHA * x + y, g_ref[...], b_ref[...])


def _experts(xs, tile_g, tile_a, tile_b, tile_v, wr, br, wg, wu, wd, g, b):
    n_tiles = tile_g.shape[0]
    te = xs.shape[0] // n_tiles
    row = pl.BlockSpec((te, D_MODEL), lambda r, *_: (r, 0))
    full = lambda shape: pl.BlockSpec(shape, lambda r, *_: (0,) * len(shape))
    up_a = pl.BlockSpec((1, D_MODEL, D_EXPERT), lambda r, tg, ta, tb, tv: (ta[r], 0, 0))
    up_b = pl.BlockSpec((1, D_MODEL, D_EXPERT), lambda r, tg, ta, tb, tv: (tb[r], 0, 0))
    dn_a = pl.BlockSpec((1, D_EXPERT, D_MODEL), lambda r, tg, ta, tb, tv: (ta[r], 0, 0))
    dn_b = pl.BlockSpec((1, D_EXPERT, D_MODEL), lambda r, tg, ta, tb, tv: (tb[r], 0, 0))
    return pl.pallas_call(
        _expert_kernel,
        grid_spec=pltpu.PrefetchScalarGridSpec(
            num_scalar_prefetch=4, grid=(n_tiles,),
            in_specs=[row, full(wr.shape), full(br.shape), up_a, up_a, dn_a, up_b, up_b, dn_b,
                      full(g.shape), full(b.shape)],
            out_specs=row),
        out_shape=jax.ShapeDtypeStruct(xs.shape, F32),
        compiler_params=_params("arbitrary"),
        name="moe_experts",
    )(tile_g, tile_a, tile_b, tile_v, xs, wr, br, wg, wu, wd, wg, wu, wd, g, b)


def _slot_kernel(route_ref, start_ref, pos_ref):
    cls, rank = route_ref[0, 0:1, :], route_ref[0, 1:2, :]
    rf = lax.broadcasted_iota(jnp.int32, (LANES, 1), 0).astype(F32)
    start = jnp.sum(jnp.where(rf == cls, start_ref[...], 0.0), axis=0, keepdims=True)
    pos_ref[0] = (start + rank).astype(jnp.int32)


def _slots(route, starts):
    n, _, tm = route.shape
    return pl.pallas_call(
        _slot_kernel,
        grid=(n,),
        in_specs=[pl.BlockSpec((1, SUBLANES, tm), lambda i: (i, 0, 0)), _full_spec(starts.shape)],
        out_specs=pl.BlockSpec((1, 1, tm), lambda i: (i, 0, 0)),
        out_shape=jax.ShapeDtypeStruct((n, 1, tm), jnp.int32),
        compiler_params=_params("parallel"),
        name="moe_slots",
    )(route, starts)


def _moe_ln(x2, route, counts, wr, br, wg, wu, wd, g, b):
    T = x2.shape[0]
    te = TE_MOE
    n_tiles = (T + N_CLASSES * (te - 1) + te - 1) // te
    cnt = counts[:N_CLASSES, 0].astype(jnp.int32)
    padded = (cnt + te - 1) // te * te
    ends = jnp.cumsum(padded)
    starts = jnp.pad((ends - padded).astype(F32), (0, LANES - N_CLASSES))[:, None]
    pos3 = _slots(route, starts)
    tile_start = jnp.arange(n_tiles, dtype=jnp.int32) * te
    tile_cls = jnp.minimum(jnp.sum(tile_start[:, None] >= ends[None, :], axis=1), N_CLASSES - 1).astype(jnp.int32)
    tile_v = (tile_start < ends[-1]).astype(jnp.int32)
    pair = jnp.asarray(_PAIRS, jnp.int32)[tile_cls % len(_PAIRS)]
    tile_g = tile_cls // len(_PAIRS)
    tile_a = tile_g * EXPERTS_PER_GROUP + pair[:, 0]
    tile_b = tile_g * EXPERTS_PER_GROUP + pair[:, 1]
    xs = _dispatch(x2, pos3, n_tiles * te)
    ys = _experts(xs, tile_g, tile_a, tile_b, tile_v, wr, br, wg, wu, wd, g, b)
    return _collect(ys, pos3)


def _router_weights(w_grp, b_grp, w_sub, b_sub):
    pad = LANES - N_GROUPS - N_EXPERTS
    wr = jnp.concatenate([w_grp, w_sub, jnp.zeros((D_MODEL, pad), w_grp.dtype)], axis=1).astype(BF16)
    br = jnp.concatenate([b_grp, b_sub, jnp.zeros((pad,), b_grp.dtype)])[None].astype(F32)
    return wr, br


def kernel(x, positions, ev_w_in, ev_g_q, ev_g_kv, ev_w_uq, ev_w_ukv, ev_w_o, od_w_in, od_b_f, od_w_o,
           moe_w_grp, moe_b_grp, moe_w_sub, moe_b_sub, moe_w_gate, moe_w_up, moe_w_down,
           ln1_g, ln1_b, ln2_g, ln2_b):
    B, S, D = x.shape
    T = B * S
    ksel = min(TOPK_MAX, S // 4)
    x2 = x.reshape(T, D)
    tabs = _rope_tables(positions.reshape(T, 1).astype(F32))
    row = lambda v: v[None].astype(F32)
    for layer in range(DEPTH):
        j = layer // 2
        wr, br = _router_weights(moe_w_grp[layer], moe_b_grp[layer], moe_w_sub[layer], moe_b_sub[layer])
        ln1 = (row(ln1_g[layer]), row(ln1_b[layer]), wr, br)
        if layer % 2 == 0:
            w1, wuq, wukv = _even_weights(ev_w_in[j], ev_w_uq[j], ev_w_ukv[j])
            qa, kv, kr, qb, kb2, va, qi, ki2, wi = _even_proj(
                x2, w1, row(ev_g_q[j]), row(ev_g_kv[j]), wuq, wukv, tabs)
            o_a = _flash((qa, kv, kr), fox=False, B=B, S=S, scale=(D_NOPE + D_ROPE) ** -0.5)
            wit = jnp.pad(wi[:, :H_IDX].T, ((0, SUBLANES - H_IDX), (0, 0)))
            o_b = _dsa(qi, qb, wit, ki2, kb2, va, B=B, S=S, ksel=ksel)
            wo = ev_w_o[j].astype(BF16)
            n_a = H_MLA * D_V_MLA
            x2, route, counts = _out_ln((o_a, o_b), (wo[:n_a], wo[n_a:]), x2, *ln1)
        else:
            n = H_FOX * D_FOX
            w = jnp.pad(od_w_in[j], ((0, 0), (0, LANES - H_FOX))).astype(BF16)
            bf = jnp.pad(od_b_f[j], (0, LANES - H_FOX))[None].astype(F32)
            q, k, v, lf = _odd_proj(x2, w, bf)
            c = _cumsum_seq(lf, S)
            c_rows = c.reshape(B, S, LANES)[:, :, :H_FOX].swapaxes(1, 2)
            o = _flash((q, k, v, c, c_rows), fox=True, B=B, S=S, scale=D_FOX ** -0.5)
            x2, route, counts = _out_ln((o,), (od_w_o[j].astype(BF16),), x2, *ln1)
        x2 = _moe_ln(x2, route, counts, wr, br, moe_w_gate[layer].astype(BF16), moe_w_up[layer].astype(BF16),
                     moe_w_down[layer].astype(BF16), row(ln2_g[layer]), row(ln2_b[layer]))
    return x2.reshape(B, S, D)
```

```python
import functools

import jax
import jax.numpy as jnp
from jax import lax
from jax.experimental import pallas as pl
from jax.experimental.pallas import tpu as pltpu

D_MODEL = 1024
DEPTH = 4
CHUNK = 64
CHUNK_SHIFT = 6
ROPE_THETA = 10000.0
NEG_INF = -1e30
LN_EPS = 1e-5
RMS_EPS = 1e-6
H_MLA, D_NOPE, D_ROPE, D_V_MLA, Q_LORA, KV_LORA = 8, 64, 32, 64, 384, 256
H_DSA, D_DSA, H_IDX, D_IDX, TOPK_MAX = 8, 64, 4, 64, 256
H_FOX, D_FOX = 16, 64
N_GROUPS, EXPERTS_PER_GROUP, D_EXPERT = 4, 4, 512
N_EXPERTS = N_GROUPS * EXPERTS_PER_GROUP
DN_ALPHA = (2 * DEPTH) ** 0.25
EV_SPLITS = (Q_LORA, KV_LORA, D_ROPE, H_DSA * D_DSA, D_DSA, D_DSA, H_IDX * D_IDX, D_IDX, H_IDX)

LANES = 128
SUBLANES = 8
VMEM_LIMIT_BYTES = 56 * 1024 * 1024

TM_PROJ = 512
TE_MOE = 256
T_ATT = 512
VT_ROWS = 80
Q_DSA = 128
KC_DSA = 256

F32 = jnp.float32
BF16 = jnp.bfloat16
_NT = (((1,), (1,)), ((), ()))
_TN = (((0,), (0,)), ((), ()))


def _params(*sem):
    return pltpu.CompilerParams(dimension_semantics=sem, vmem_limit_bytes=VMEM_LIMIT_BYTES)


def _lane_iota(shape=(1, LANES)):
    return lax.broadcasted_iota(jnp.int32, shape, len(shape) - 1)


def _full_spec(shape):
    return pl.BlockSpec(shape, lambda *_: (0,) * len(shape))


def _rope_table_kernel(pos_ref, inv32_ref, sg32_ref, inv64_ref, sg64_ref,
                       c32_ref, s32_ref, c64_ref, s64_ref):
    pos = pos_ref[...]
    a32 = pos * inv32_ref[...]
    a64 = pos * inv64_ref[...]
    c32_ref[...] = jnp.cos(a32)
    s32_ref[...] = jnp.sin(a32) * sg32_ref[...]
    c64_ref[...] = jnp.cos(a64)
    s64_ref[...] = jnp.sin(a64) * sg64_ref[...]


def _rope_tables(posf):
    T = posf.shape[0]
    tm = min(T, 2048)
    lane = jnp.arange(LANES)
    inv16 = jnp.power(ROPE_THETA, -jnp.arange(0, D_ROPE, 2, dtype=F32) / D_ROPE)
    inv32 = jnp.power(ROPE_THETA, -jnp.arange(0, D_DSA, 2, dtype=F32) / D_DSA)
    in32 = (lane >= D_NOPE) & (lane < D_NOPE + D_ROPE)
    t32 = jnp.where(in32, inv16[(lane - D_NOPE) % (D_ROPE // 2)], 0.0).astype(F32)[None]
    g32 = jnp.where(in32, jnp.where(lane < D_NOPE + D_ROPE // 2, -1.0, 1.0), 0.0).astype(F32)[None]
    t64 = inv32[lane % (D_DSA // 2)].astype(F32)[None]
    g64 = jnp.where((lane % D_DSA) < D_DSA // 2, -1.0, 1.0).astype(F32)[None]
    row = pl.BlockSpec((tm, LANES), lambda i: (i, 0))
    tab = jax.ShapeDtypeStruct((T, LANES), F32)
    return pl.pallas_call(
        _rope_table_kernel,
        grid=(T // tm,),
        in_specs=[pl.BlockSpec((tm, 1), lambda i: (i, 0))] + [_full_spec((1, LANES))] * 4,
        out_specs=[row] * 4,
        out_shape=[tab] * 4,
        compiler_params=_params("parallel"),
        name="rope_tables",
    )(posf, t32, g32, t64, g64)


def _rope_block(y, c, s, half, second_half):
    partner = jnp.where(second_half, pltpu.roll(y, half, 1), pltpu.roll(y, LANES - half, 1))
    return y * c + partner * s


def _rms(x, g):
    return x * lax.rsqrt(jnp.mean(x * x, axis=-1, keepdims=True) + RMS_EPS) * g


def _layer_norm(z, g, b):
    mu = jnp.mean(z, axis=-1, keepdims=True)
    d = z - mu
    var = jnp.mean(d * d, axis=-1, keepdims=True)
    return d * lax.rsqrt(var + LN_EPS) * g + b


_E_CQ, _E_CKV, _E_KR, _E_QB, _E_KB, _E_VB, _E_QI, _E_KI, _E_WI, _E_END = (
    0, 384, 640, 768, 1280, 1408, 1536, 1792, 1920, 2048)


def _even_proj_kernel(x_ref, w1_ref, gq_ref, gkv_ref, wuq_ref, wukv_ref,
                      c32_ref, s32_ref, c64_ref, s64_ref,
                      qa_ref, kv_ref, kr_ref, qb_ref, kb_ref, va_ref, qi_ref, ki_ref, wi_ref):
    xb = x_ref[...].astype(BF16)
    lane = _lane_iota()
    sec32 = (lane >= D_NOPE + D_ROPE // 2) & (lane < D_NOPE + D_ROPE)
    sec64 = (lane % D_DSA) >= D_DSA // 2
    c32, s32, c64, s64 = c32_ref[...], s32_ref[...], c64_ref[...], s64_ref[...]

    def seg(a, b):
        return jnp.dot(xb, w1_ref[:, a:b], preferred_element_type=F32)

    def rope64_store(h, out_ref):
        for j in range(h.shape[1] // LANES):
            blk = h[:, j * LANES:(j + 1) * LANES]
            out_ref[:, j * LANES:(j + 1) * LANES] = _rope_block(
                blk, c64, s64, D_DSA // 2, sec64).astype(out_ref.dtype)

    cq = _rms(seg(_E_CQ, _E_CKV), gq_ref[...]).astype(BF16)
    q = jnp.dot(cq, wuq_ref[...], preferred_element_type=F32)
    for h in range(H_MLA):
        blk = q[:, h * LANES:(h + 1) * LANES]
        qa_ref[:, h * LANES:(h + 1) * LANES] = _rope_block(
            blk, c32, s32, D_ROPE // 2, sec32).astype(BF16)
    ckv = _rms(seg(_E_CKV, _E_KR), gkv_ref[...]).astype(BF16)
    kv_ref[...] = jnp.dot(ckv, wukv_ref[...], preferred_element_type=F32).astype(BF16)
    kr_ref[...] = _rope_block(seg(_E_KR, _E_QB), c32, s32, D_ROPE // 2, sec32).astype(BF16)
    rope64_store(seg(_E_QB, _E_KB), qb_ref)
    rope64_store(seg(_E_KB, _E_VB), kb_ref)
    va_ref[...] = jnp.where(lane < D_DSA, seg(_E_VB, _E_QI), 1.0).astype(BF16)
    rope64_store(seg(_E_QI, _E_KI), qi_ref)
    rope64_store(seg(_E_KI, _E_WI), ki_ref)
    wi_ref[...] = seg(_E_WI, _E_END)


def _even_weights(w_in, w_uq, w_ukv):
    o = [0]
    for s_ in EV_SPLITS:
        o.append(o[-1] + s_)
    cq, ckv, kr, qb, kb, vb, qi, ki, wi = (w_in[:, o[i]:o[i + 1]] for i in range(9))
    z = lambda n: jnp.zeros((D_MODEL, n), w_in.dtype)
    w1 = jnp.concatenate([
        cq, ckv,
        z(D_NOPE), kr, z(LANES - D_NOPE - D_ROPE),
        qb,
        kb, kb,
        vb, z(LANES - D_DSA),
        qi,
        ki, ki,
        wi, z(LANES - H_IDX)], axis=1).astype(BF16)
    wq = w_uq.reshape(Q_LORA, H_MLA, D_NOPE + D_ROPE)
    wq = jnp.pad(wq, ((0, 0), (0, 0), (0, LANES - D_NOPE - D_ROPE))).reshape(Q_LORA, H_MLA * LANES)
    return w1, wq.astype(BF16), w_ukv.astype(BF16)


def _even_proj(x2, w1, gq, gkv, wuq, wukv, tabs):
    T = x2.shape[0]
    tm = min(TM_PROJ, T)
    row = lambda n: pl.BlockSpec((tm, n), lambda i: (i, 0))
    outs = [(H_MLA * LANES, BF16), (H_MLA * LANES, BF16), (LANES, BF16), (H_DSA * D_DSA, BF16),
            (LANES, BF16), (LANES, BF16), (H_IDX * D_IDX, BF16), (LANES, BF16), (LANES, F32)]
    return pl.pallas_call(
        _even_proj_kernel,
        grid=(T // tm,),
        in_specs=[row(D_MODEL), _full_spec(w1.shape), _full_spec(gq.shape), _full_spec(gkv.shape),
                  _full_spec(wuq.shape), _full_spec(wukv.shape)] + [row(LANES)] * 4,
        out_specs=[row(n) for n, _ in outs],
        out_shape=[jax.ShapeDtypeStruct((T, n), d) for n, d in outs],
        compiler_params=_params("parallel"),
        name="even_proj",
    )(x2, w1, gq, gkv, wuq, wukv, *tabs)


def _odd_proj_kernel(x_ref, w_ref, bf_ref, q_ref, k_ref, v_ref, lf_ref):
    xb = x_ref[...].astype(BF16)
    n = H_FOX * D_FOX
    for j, out in enumerate((q_ref, k_ref, v_ref)):
        out[...] = jnp.dot(xb, w_ref[:, j * n:(j + 1) * n], preferred_element_type=F32).astype(BF16)
    z = jnp.dot(xb, w_ref[:, 3 * n:], preferred_element_type=F32) + bf_ref[...]
    lf_ref[...] = jnp.minimum(z, 0.0) - jnp.log1p(jnp.exp(-jnp.abs(z)))


def _odd_proj(x2, w, bf):
    T = x2.shape[0]
    tm = min(TM_PROJ, T)
    n = H_FOX * D_FOX
    row = lambda c: pl.BlockSpec((tm, c), lambda i: (i, 0))
    return pl.pallas_call(
        _odd_proj_kernel,
        grid=(T // tm,),
        in_specs=[row(D_MODEL), _full_spec(w.shape), _full_spec(bf.shape)],
        out_specs=[row(n), row(n), row(n), row(LANES)],
        out_shape=[jax.ShapeDtypeStruct((T, n), BF16)] * 3 + [jax.ShapeDtypeStruct((T, LANES), F32)],
        compiler_params=_params("parallel"),
        name="odd_proj",
    )(x2, w, bf)


def _cumsum_kernel(lf_ref, c_ref):
    a = lf_ref[...]
    rows = lax.broadcasted_iota(jnp.int32, a.shape, 0)
    k = 1
    while k < a.shape[0]:
        a = a + jnp.where(rows >= k, pltpu.roll(a, k, 0), 0.0)
        k *= 2
    c_ref[...] = a


def _cumsum_seq(lf, S):
    T = lf.shape[0]
    blk = pl.BlockSpec((S, LANES), lambda b: (b, 0))
    return pl.pallas_call(
        _cumsum_kernel, grid=(T // S,), in_specs=[blk], out_specs=blk,
        out_shape=jax.ShapeDtypeStruct((T, LANES), F32),
        compiler_params=_params("parallel"), name="forget_cumsum",
    )(lf)


def _flash_kernel(*refs, fox, scale, t, n_tiles, S):
    if fox:
        q_ref, k_ref, v_ref, cc_ref, cr_ref, o_ref, kx_ref, vt_ref, sb0_ref, sb1_ref = refs
    else:
        q_ref, kv_ref, kr_ref, o_ref, kx_ref, vt_ref, sb0_ref, sb1_ref = refs
    sb_refs = (sb0_ref, sb1_ref)
    hp = pl.program_id(1)
    lane = _lane_iota()
    lo = lane < D_FOX
    hi = lane >= D_FOX
    exp2_scale = scale * 1.4426950408889634

    ones = jnp.ones((VT_ROWS - D_FOX, S), BF16)
    if fox:
        vt = jnp.transpose(v_ref[0].astype(F32)).astype(BF16)
    for hh in range(2):
        if fox:
            col = jnp.sum(jnp.where(lane == 2 * hp + hh, cc_ref[0], 0.0), axis=-1, keepdims=True)
            kx_ref[hh] = jnp.broadcast_to(col, (S, LANES))
            vt_ref[hh, :D_FOX, :] = vt[hh * D_FOX:(hh + 1) * D_FOX]
        else:
            kvh = kv_ref[0, :, hh * LANES:(hh + 1) * LANES]
            kx_ref[hh] = jnp.where(lo, kvh, kr_ref[0])
            vt_ref[hh, :D_V_MLA, :] = jnp.transpose(kvh.astype(F32))[D_NOPE:].astype(BF16)
        vt_ref[hh, D_FOX:, :] = ones

    def body(n):
        qrows = slice(n * t, (n + 1) * t)
        half = t // 2
        blocks = [(c * t, t, 0, False) for c in range(n)] + [(n * t, half, 0, True), (n * t + half, half, half, True)]

        def on_lanes(full, q0, part, op):
            if q0 == 0:
                return op(full, part)
            return jnp.concatenate([full[:, :q0], op(full[:, q0:], part)], axis=1)

        outs = []
        for hh in range(2):
            if fox:
                q2 = q_ref[0, qrows, :] * jnp.asarray(scale, BF16)
                qh = jnp.where(lo if hh == 0 else hi, q2, jnp.zeros_like(q2))
                cq = cr_ref[0, pl.ds(2 * hp + hh, 1), qrows]
            else:
                qh = q_ref[0, qrows, hh * LANES:(hh + 1) * LANES]
            m8 = jnp.full((SUBLANES, t), -jnp.inf, F32)
            for k0, nk, q0, masked in blocks:
                nq = t - q0
                kh = k_ref[0, k0:k0 + nk, :] if fox else kx_ref[hh, k0:k0 + nk, :]
                s = lax.dot_general(kh, qh[q0:], _NT, preferred_element_type=F32)
                if fox:
                    s = s + cq[:, q0:] - jnp.concatenate([kx_ref[hh, k0:k0 + nk, :]] * (nq // LANES), axis=1)
                if masked:
                    k_idx = k0 + lax.broadcasted_iota(jnp.int32, (nk, 1), 0)
                    q_idx = n * t + q0 + lax.broadcasted_iota(jnp.int32, (1, nq), 1)
                    vis = (k_idx <= q_idx) if fox else ((k_idx >> CHUNK_SHIFT) <= (q_idx >> CHUNK_SHIFT))
                    s = jnp.where(vis, s, NEG_INF)
                sb_refs[hh][k0:k0 + nk, q0:] = s
                m8 = on_lanes(m8, q0, jnp.max(s.reshape(nk // SUBLANES, SUBLANES, nq), axis=0), jnp.maximum)
            m = jnp.max(m8, axis=0, keepdims=True)
            acc = jnp.zeros((VT_ROWS, t), F32)
            for k0, nk, q0, masked in blocks:
                x = sb_refs[hh][k0:k0 + nk, q0:] - m[:, q0:]
                p = jnp.exp(x) if fox else jnp.exp2(x * exp2_scale)
                pv = jnp.dot(vt_ref[hh, :, k0:k0 + nk], p.astype(BF16), preferred_element_type=F32)
                acc = on_lanes(acc, q0, pv, jnp.add)
            outs.append(acc[:D_FOX] / acc[D_FOX:D_FOX + 1])
        o_ref[0, qrows, :] = jnp.transpose(jnp.concatenate(outs, axis=0)).astype(BF16)

    for n in range(n_tiles):
        body(n)


def _flash(args, *, fox, B, S, scale):
    t = min(T_ATT, S)
    n = S // t
    npairs = (H_FOX if fox else H_MLA) // 2
    args = tuple(a.reshape(B, S, a.shape[-1]) if a.ndim == 2 else a for a in args)
    seq = lambda b, hp: (b, 0, hp)
    whole = lambda b, hp: (b, 0, 0)
    scratch = [pltpu.VMEM((2, S, LANES), F32 if fox else BF16), pltpu.VMEM((2, VT_ROWS, S), BF16),
               pltpu.VMEM((S, t), F32), pltpu.VMEM((S, t), F32)]
    if fox:
        in_specs = [pl.BlockSpec((1, S, LANES), seq), pl.BlockSpec((1, S, LANES), seq),
                    pl.BlockSpec((1, S, LANES), seq), pl.BlockSpec((1, S, LANES), whole),
                    pl.BlockSpec((1, H_FOX, S), whole)]
    else:
        in_specs = [pl.BlockSpec((1, S, 2 * LANES), seq), pl.BlockSpec((1, S, 2 * LANES), seq),
                    pl.BlockSpec((1, S, LANES), whole)]
    out = pl.pallas_call(
        functools.partial(_flash_kernel, fox=fox, scale=scale, t=t, n_tiles=n, S=S),
        grid=(B, npairs),
        in_specs=in_specs,
        out_specs=pl.BlockSpec((1, S, LANES), seq),
        out_shape=jax.ShapeDtypeStruct((B, S, npairs * LANES), BF16),
        scratch_shapes=scratch,
        compiler_params=_params("parallel", "parallel"),
        name="fox_attention" if fox else "mla_attention",
    )(*args)
    return out.reshape(B * S, npairs * LANES)


def _dsa_kernel(qi_ref, qb_ref, wit_ref, ki_ref, kb_ref, va_ref, o_ref,
                qim_ref, qbm_ref, sc_ref, sb_ref, *, nk, i0, ksel):
    i = i0 + pl.program_id(1)
    chunks = [slice(k0, min(k0 + KC_DSA, nk)) for k0 in range(0, nk, KC_DSA)]
    lane = _lane_iota()
    lo = lane < D_DSA
    hi = lane >= D_DSA
    q_chunk = (i * Q_DSA + lane) >> CHUNK_SHIFT
    select = nk > ksel

    def key_idx(rows):
        return rows.start + lax.broadcasted_iota(jnp.int32, (rows.stop - rows.start, 1), 0)

    def admissible(rows):
        return (key_idx(rows) >> CHUNK_SHIFT) <= q_chunk

    def fold_max(s):
        return jnp.max(s.reshape(s.shape[0] // SUBLANES, SUBLANES, s.shape[1]), axis=0)

    for h in range(H_DSA):
        blk = qb_ref[:, (h // 2) * LANES:(h // 2 + 1) * LANES] * jnp.asarray(D_DSA ** -0.5, BF16)
        qbm_ref[h * Q_DSA:(h + 1) * Q_DSA, :] = jnp.where(lo if h % 2 == 0 else hi, blk, jnp.zeros_like(blk))

    if select:
        for h in range(H_IDX):
            blk = qi_ref[:, (h // 2) * LANES:(h // 2 + 1) * LANES]
            qim_ref[h * Q_DSA:(h + 1) * Q_DSA, :] = jnp.where(lo if h % 2 == 0 else hi, blk, jnp.zeros_like(blk))
        w_flat = jnp.concatenate([wit_ref[h:h + 1, :] for h in range(H_IDX)], axis=1)
        for rows in chunks:
            r = lax.dot_general(ki_ref[0, rows, :], qim_ref[...], _NT, preferred_element_type=F32)
            r = jnp.maximum(r, 0.0) * w_flat
            acc = r[:, :LANES]
            for h in range(1, H_IDX):
                acc = acc + r[:, h * LANES:(h + 1) * LANES]
            sc_ref[rows, :] = jnp.where(admissible(rows), acc, NEG_INF)

        def count(mask):
            part = jnp.sum(mask.astype(F32).reshape(nk // 32, 32, LANES), axis=0)
            return jnp.sum(part, axis=0, keepdims=True)

        def as_float(key):
            return lax.bitcast_convert_type(jnp.where(key >= 0, key, key ^ 0x7FFFFFFF), F32)

        def bit_body(r, ans):
            cand = ans + lax.shift_left(jnp.int32(1), 31 - r)
            cnt = count(sc_ref[...] >= as_float(cand))
            return jnp.where(cnt >= ksel, cand, ans)
        ans = lax.fori_loop(0, 32, bit_body, jnp.full((1, LANES), jnp.iinfo(jnp.int32).min, jnp.int32))
        t = as_float(ans)
        s_all = sc_ref[...]
        need = ksel - count(s_all > t)
        n_ge = count(s_all >= t)
        k_all = lax.broadcasted_iota(jnp.int32, (nk, 1), 0)

        def tie_bound():
            def bit_body2(r, aj):
                cand = aj + lax.shift_left(jnp.int32(1), 11 - r)
                g = count((sc_ref[...] == t) & (k_all < cand))
                return jnp.where(g <= need, cand, aj)
            return lax.fori_loop(0, 12, bit_body2, jnp.zeros((1, LANES), jnp.int32))

        jmax = lax.cond(jnp.max(n_ge) > ksel, tie_bound,
                        lambda: jnp.full((1, LANES), 4095, jnp.int32))
        for rows in chunks:
            s = sc_ref[rows, :]
            sel = (s > t) | ((s == t) & (key_idx(rows) < jmax))
            sc_ref[rows, :] = jnp.where(sel & admissible(rows), 0.0, NEG_INF)
    else:
        for rows in chunks:
            sc_ref[rows, :] = jnp.where(admissible(rows), 0.0, NEG_INF)

    m8 = jnp.full((SUBLANES, H_DSA * LANES), -jnp.inf, F32)
    for rows in chunks:
        s = lax.dot_general(kb_ref[0, rows, :], qbm_ref[...], _NT, preferred_element_type=F32)
        s = s + jnp.concatenate([sc_ref[rows, :]] * H_DSA, axis=1)
        sb_ref[rows, :] = s
        m8 = jnp.maximum(m8, fold_max(s))
    m = jnp.max(m8, axis=0, keepdims=True)
    acc = jnp.zeros((H_DSA * Q_DSA, LANES), F32)
    for rows in chunks:
        p = jnp.exp(sb_ref[rows, :] - m).astype(BF16)
        acc = acc + lax.dot_general(p, va_ref[0, rows, :], _TN, preferred_element_type=F32)

    for hp in range(H_DSA // 2):
        a = acc[(2 * hp) * Q_DSA:(2 * hp + 1) * Q_DSA]
        b = acc[(2 * hp + 1) * Q_DSA:(2 * hp + 2) * Q_DSA]
        ra = a / pltpu.roll(a, D_DSA, 1)
        rb = b / pltpu.roll(b, D_DSA, 1)
        o_ref[:, hp * LANES:(hp + 1) * LANES] = jnp.where(lo, ra, pltpu.roll(rb, D_DSA, 1)).astype(BF16)


def _dsa(qi, qb, wit, ki2, kb2, va, *, B, S, ksel):
    nb = S // Q_DSA
    per = KC_DSA // Q_DSA
    seq3 = lambda a: a.reshape(B, S, LANES)
    ki3, kb3, va3 = seq3(ki2), seq3(kb2), seq3(va)
    outs = []
    for g in range(nb // per):
        nk, i0 = (g + 1) * per * Q_DSA, g * per
        qrow = lambda n, i0=i0: pl.BlockSpec((Q_DSA, n), lambda b, j: (b * nb + i0 + j, 0))
        seq = pl.BlockSpec((1, nk, LANES), lambda b, j: (b, 0, 0))
        o = pl.pallas_call(
            functools.partial(_dsa_kernel, nk=nk, i0=i0, ksel=ksel),
            grid=(B, per),
            in_specs=[qrow(H_IDX * D_IDX), qrow(H_DSA * D_DSA),
                      pl.BlockSpec((SUBLANES, Q_DSA), lambda b, j, i0=i0: (0, b * nb + i0 + j)),
                      seq, seq, seq],
            out_specs=pl.BlockSpec((Q_DSA, H_DSA * D_DSA), lambda b, j: (b * per + j, 0)),
            out_shape=jax.ShapeDtypeStruct((B * per * Q_DSA, H_DSA * D_DSA), BF16),
            scratch_shapes=[pltpu.VMEM((H_IDX * Q_DSA, LANES), BF16), pltpu.VMEM((H_DSA * Q_DSA, LANES), BF16),
                            pltpu.VMEM((nk, LANES), F32), pltpu.VMEM((nk, H_DSA * LANES), F32)],
            compiler_params=_params("parallel", "arbitrary"),
            name="dsa_attention_%d" % nk,
        )(qi, qb, wit, ki3, kb3, va3)
        outs.append(o.reshape(B, per * Q_DSA, H_DSA * D_DSA))
    return jnp.concatenate(outs, axis=1).reshape(B * S, H_DSA * D_DSA)


_R_SUB = N_GROUPS
_PAIRS = [(a, b) for a in range(EXPERTS_PER_GROUP) for b in range(a + 1, EXPERTS_PER_GROUP)]
N_CLASSES = N_GROUPS * len(_PAIRS)


def _out_ln_kernel(*refs, n_in):
    o_refs, w_refs = refs[:n_in], refs[n_in:2 * n_in]
    x_ref, g_ref, b_ref, wrt_ref, brt_ref, tri_ref, out_ref, route_ref, cnt_ref, run_ref = refs[2 * n_in:]
    y = jnp.dot(o_refs[0][...], w_refs[0][...], preferred_element_type=F32)
    for o_r, w_r in zip(o_refs[1:], w_refs[1:]):
        y = y + jnp.dot(o_r[...], w_r[...], preferred_element_type=F32)
    x_new = _layer_norm(DN_ALPHA * x_ref[...] + y, g_ref[...], b_ref[...])
    out_ref[...] = x_new
    _route_rows(x_new, wrt_ref, brt_ref, tri_ref, route_ref, cnt_ref, run_ref)


def _out_ln(os_, ws, x2, g, b, wr, br):
    T = x2.shape[0]
    tm = min(TM_PROJ, T)
    row = lambda c: pl.BlockSpec((tm, c), lambda i: (i, 0))
    tri = jnp.triu(jnp.ones((tm, tm), BF16), 1)
    wrt, brt = wr.T, br.T
    return pl.pallas_call(
        functools.partial(_out_ln_kernel, n_in=len(os_)),
        grid=(T // tm,),
        in_specs=[row(o.shape[1]) for o in os_] + [_full_spec(w.shape) for w in ws]
                 + [row(D_MODEL), _full_spec(g.shape), _full_spec(b.shape),
                    _full_spec(wrt.shape), _full_spec(brt.shape), _full_spec(tri.shape)],
        out_specs=[row(D_MODEL), pl.BlockSpec((1, SUBLANES, tm), lambda i: (i, 0, 0)), _full_spec((LANES, LANES))],
        out_shape=[jax.ShapeDtypeStruct((T, D_MODEL), F32), jax.ShapeDtypeStruct((T // tm, SUBLANES, tm), F32),
                   jax.ShapeDtypeStruct((LANES, LANES), F32)],
        scratch_shapes=[pltpu.VMEM((LANES, 1), F32)],
        compiler_params=_params("arbitrary"),
        name="out_proj_ln",
    )(*os_, *ws, x2, g, b, wrt, brt, tri)


def _route_rows(x, wrt_ref, brt_ref, tri_ref, route_ref, cnt_ref, run_ref):
    @pl.when(pl.program_id(0) == 0)
    def _():
        run_ref[...] = jnp.zeros(run_ref.shape, F32)

    tm = x.shape[0]
    lg = lax.dot_general(wrt_ref[...], x.astype(BF16), _NT, preferred_element_type=F32) + brt_ref[...]
    r = lax.broadcasted_iota(jnp.int32, (LANES, 1), 0)
    rf = r.astype(F32)
    far = float(LANES)
    grp = jnp.where(r < N_GROUPS, lg, -jnp.inf)
    mg = jnp.max(grp, axis=0, keepdims=True)
    g_star = jnp.min(jnp.where(grp == mg, rf, far), axis=0, keepdims=True)
    row_grp = ((r - _R_SUB) >> 2).astype(F32)
    in_grp = (row_grp == g_star) & (r >= _R_SUB) & (r < _R_SUB + N_EXPERTS)
    l1 = jnp.where(in_grp, lg, -jnp.inf)
    v1 = jnp.max(l1, axis=0, keepdims=True)
    j1 = jnp.min(jnp.where(l1 == v1, rf, far), axis=0, keepdims=True)
    l2 = jnp.where(in_grp & (rf != j1), lg, -jnp.inf)
    v2 = jnp.max(l2, axis=0, keepdims=True)
    j2 = jnp.min(jnp.where(l2 == v2, rf, far), axis=0, keepdims=True)
    base = _R_SUB + EXPERTS_PER_GROUP * g_star
    ja = jnp.minimum(j1, j2) - base
    jb = jnp.maximum(j1, j2) - base
    cls = g_star * len(_PAIRS) + ja * (7.0 - ja) * 0.5 + (jb - ja - 1.0)
    onehot = rf == cls
    ones = jnp.where(onehot, 1.0, 0.0)
    before = jnp.dot(ones.astype(BF16), tri_ref[...], preferred_element_type=F32)
    rank = jnp.sum(jnp.where(onehot, before + run_ref[...], 0.0), axis=0, keepdims=True)
    row8 = lax.broadcasted_iota(jnp.int32, (SUBLANES, 1), 0)
    route_ref[0] = jnp.where(row8 == 0, cls, jnp.where(row8 == 1, rank, 0.0))
    run_ref[...] += jnp.sum(ones, axis=1, keepdims=True)
    cnt_ref[...] = jnp.broadcast_to(run_ref[...], cnt_ref.shape)


def _row_copy(src_ref, dst_ref, src_row, dst_row, sem):
    return pltpu.make_async_copy(src_ref.at[pl.ds(src_row, 1), :], dst_ref.at[pl.ds(dst_row, 1), :], sem)


def _dispatch_kernel(pos_ref, x_ref, init_ref, xs_ref, sem):
    del init_ref
    tm = x_ref.shape[0]

    def start(g, carry):
        base = pl.multiple_of(g * SUBLANES, SUBLANES)
        for j in range(SUBLANES):
            _row_copy(x_ref, xs_ref, base + j, pos_ref[0, 0, base + j], sem).start()
        return carry
    lax.fori_loop(0, tm // SUBLANES, start, 0)

    def wait(t, carry):
        _row_copy(x_ref, xs_ref, 0, 0, sem).wait()
        return carry
    lax.fori_loop(0, tm, wait, 0, unroll=8)


def _collect_kernel(pos_ref, ys_ref, out_ref, sem):
    tm = out_ref.shape[0]

    def start(g, carry):
        base = pl.multiple_of(g * SUBLANES, SUBLANES)
        for j in range(SUBLANES):
            _row_copy(ys_ref, out_ref, pos_ref[0, 0, base + j], base + j, sem).start()
        return carry
    lax.fori_loop(0, tm // SUBLANES, start, 0)

    def wait(t, carry):
        _row_copy(ys_ref, out_ref, 0, 0, sem).wait()
        return carry
    lax.fori_loop(0, tm, wait, 0, unroll=8)


def _dispatch(x2, pos3, n_rows):
    T = x2.shape[0]
    tm = pos3.shape[-1]
    return pl.pallas_call(
        _dispatch_kernel,
        grid=(T // tm,),
        in_specs=[pl.BlockSpec((1, 1, tm), lambda i: (i, 0, 0), memory_space=pltpu.SMEM),
                  pl.BlockSpec((tm, D_MODEL), lambda i: (i, 0)),
                  pl.BlockSpec(memory_space=pl.ANY)],
        out_specs=pl.BlockSpec(memory_space=pl.ANY),
        out_shape=jax.ShapeDtypeStruct((n_rows, D_MODEL), F32),
        scratch_shapes=[pltpu.SemaphoreType.DMA(())],
        input_output_aliases={2: 0},
        compiler_params=_params("arbitrary"),
        name="moe_dispatch",
    )(pos3, x2, jnp.zeros((n_rows, D_MODEL), F32))


def _collect(ys, pos3):
    T = pos3.shape[0] * pos3.shape[-1]
    tm = pos3.shape[-1]
    return pl.pallas_call(
        _collect_kernel,
        grid=(T // tm,),
        in_specs=[pl.BlockSpec((1, 1, tm), lambda i: (i, 0, 0), memory_space=pltpu.SMEM),
                  pl.BlockSpec(memory_space=pl.ANY)],
        out_specs=pl.BlockSpec((tm, D_MODEL), lambda i: (i, 0)),
        out_shape=jax.ShapeDtypeStruct((T, D_MODEL), F32),
        scratch_shapes=[pltpu.SemaphoreType.DMA(())],
        compiler_params=_params("arbitrary"),
        name="moe_collect",
    )(pos3, ys)


def _expert_kernel(tg_ref, ta_ref, tb_ref, tv_ref, x_ref, wr_ref, br_ref,
                   wga_ref, wua_ref, wda_ref, wgb_ref, wub_ref, wdb_ref, g_ref, b_ref, out_ref):
    r = pl.program_id(0)

    @pl.when(tv_ref[r] == 0)
    def _():
        out_ref[...] = jnp.zeros(out_ref.shape, F32)

    @pl.when(tv_ref[r] != 0)
    def _():
        x = x_ref[...]
        xb = x.astype(BF16)
        lg = jnp.dot(xb, wr_ref[...], preferred_element_type=F32) + br_ref[...]
        lane = _lane_iota()
        is_grp = lane < N_GROUPS
        mg = jnp.max(jnp.where(is_grp, lg, -jnp.inf), axis=-1, keepdims=True)
        den = jnp.sum(jnp.where(is_grp, jnp.exp(lg - mg), 0.0), axis=-1, keepdims=True)
        pick = lambda l: jnp.sum(jnp.where(lane == l, lg, 0.0), axis=-1, keepdims=True)
        p_top = jnp.exp(pick(tg_ref[r]) - mg) / den
        va, vb = pick(_R_SUB + ta_ref[r]), pick(_R_SUB + tb_ref[r])
        e = jnp.exp(jnp.minimum(va, vb) - jnp.maximum(va, vb))
        w_top, w_oth = 1.0 / (1.0 + e), e / (1.0 + e)
        a_top = va >= vb
        gates = (jnp.where(a_top, w_top, w_oth) * p_top, jnp.where(a_top, w_oth, w_top) * p_top)
        y = jnp.zeros(x.shape, F32)
        for gate, wg, wu, wd in ((gates[0], wga_ref, wua_ref, wda_ref), (gates[1], wgb_ref, wub_ref, wdb_ref)):
            a = jnp.dot(xb, wg[0], preferred_element_type=F32)
            u = jnp.dot(xb, wu[0], preferred_element_type=F32)
            h = (a * (1.0 / (1.0 + jnp.exp(-a))) * u).astype(BF16)
            y = y + gate * jnp.dot(h, wd[0], preferred_element_type=F32)
        out_ref[...] = _layer_norm(DN_ALPHA * x + y, g_ref[...], b_ref[...])


def _experts(xs, tile_g, tile_a, tile_b, tile_v, wr, br, wg, wu, wd, g, b):
    n_tiles = tile_g.shape[0]
    te = xs.shape[0] // n_tiles
    row = pl.BlockSpec((te, D_MODEL), lambda r, *_: (r, 0))
    full = lambda shape: pl.BlockSpec(shape, lambda r, *_: (0,) * len(shape))
    up_a = pl.BlockSpec((1, D_MODEL, D_EXPERT), lambda r, tg, ta, tb, tv: (ta[r], 0, 0))
    up_b = pl.BlockSpec((1, D_MODEL, D_EXPERT), lambda r, tg, ta, tb, tv: (tb[r], 0, 0))
    dn_a = pl.BlockSpec((1, D_EXPERT, D_MODEL), lambda r, tg, ta, tb, tv: (ta[r], 0, 0))
    dn_b = pl.BlockSpec((1, D_EXPERT, D_MODEL), lambda r, tg, ta, tb, tv: (tb[r], 0, 0))
    return pl.pallas_call(
        _expert_kernel,
        grid_spec=pltpu.PrefetchScalarGridSpec(
            num_scalar_prefetch=4, grid=(n_tiles,),
            in_specs=[row, full(wr.shape), full(br.shape), up_a, up_a, dn_a, up_b, up_b, dn_b,
                      full(g.shape), full(b.shape)],
            out_specs=row),
        out_shape=jax.ShapeDtypeStruct(xs.shape, F32),
        compiler_params=_params("arbitrary"),
        name="moe_experts",
    )(tile_g, tile_a, tile_b, tile_v, xs, wr, br, wg, wu, wd, wg, wu, wd, g, b)


def _slot_kernel(route_ref, start_ref, pos_ref):
    cls, rank = route_ref[0, 0:1, :], route_ref[0, 1:2, :]
    rf = lax.broadcasted_iota(jnp.int32, (LANES, 1), 0).astype(F32)
    start = jnp.sum(jnp.where(rf == cls, start_ref[...], 0.0), axis=0, keepdims=True)
    pos_ref[0] = (start + rank).astype(jnp.int32)


def _slots(route, starts):
    n, _, tm = route.shape
    return pl.pallas_call(
        _slot_kernel,
        grid=(n,),
        in_specs=[pl.BlockSpec((1, SUBLANES, tm), lambda i: (i, 0, 0)), _full_spec(starts.shape)],
        out_specs=pl.BlockSpec((1, 1, tm), lambda i: (i, 0, 0)),
        out_shape=jax.ShapeDtypeStruct((n, 1, tm), jnp.int32),
        compiler_params=_params("parallel"),
        name="moe_slots",
    )(route, starts)


def _moe_ln(x2, route, counts, wr, br, wg, wu, wd, g, b):
    T = x2.shape[0]
    te = TE_MOE
    n_tiles = (T + N_CLASSES * (te - 1) + te - 1) // te
    cnt = counts[:N_CLASSES, 0].astype(jnp.int32)
    padded = (cnt + te - 1) // te * te
    ends = jnp.cumsum(padded)
    starts = jnp.pad((ends - padded).astype(F32), (0, LANES - N_CLASSES))[:, None]
    pos3 = _slots(route, starts)
    tile_start = jnp.arange(n_tiles, dtype=jnp.int32) * te
    tile_cls = jnp.minimum(jnp.sum(tile_start[:, None] >= ends[None, :], axis=1), N_CLASSES - 1).astype(jnp.int32)
    tile_v = (tile_start < ends[-1]).astype(jnp.int32)
    pair = jnp.asarray(_PAIRS, jnp.int32)[tile_cls % len(_PAIRS)]
    tile_g = tile_cls // len(_PAIRS)
    tile_a = tile_g * EXPERTS_PER_GROUP + pair[:, 0]
    tile_b = tile_g * EXPERTS_PER_GROUP + pair[:, 1]
    xs = _dispatch(x2, pos3, n_tiles * te)
    ys = _experts(xs, tile_g, tile_a, tile_b, tile_v, wr, br, wg, wu, wd, g, b)
    return _collect(ys, pos3)


def _router_weights(w_grp, b_grp, w_sub, b_sub):
    pad = LANES - N_GROUPS - N_EXPERTS
    wr = jnp.concatenate([w_grp, w_sub, jnp.zeros((D_MODEL, pad), w_grp.dtype)], axis=1).astype(BF16)
    br = jnp.concatenate([b_grp, b_sub, jnp.zeros((pad,), b_grp.dtype)])[None].astype(F32)
    return wr, br


def kernel(x, positions, ev_w_in, ev_g_q, ev_g_kv, ev_w_uq, ev_w_ukv, ev_w_o, od_w_in, od_b_f, od_w_o,
           moe_w_grp, moe_b_grp, moe_w_sub, moe_b_sub, moe_w_gate, moe_w_up, moe_w_down,
           ln1_g, ln1_b, ln2_g, ln2_b):
    B, S, D = x.shape
    T = B * S
    ksel = min(TOPK_MAX, S // 4)
    x2 = x.reshape(T, D)
    tabs = _rope_tables(positions.reshape(T, 1).astype(F32))
    row = lambda v: v[None].astype(F32)
    for layer in range(DEPTH):
        j = layer // 2
        wr, br = _router_weights(moe_w_grp[layer], moe_b_grp[layer], moe_w_sub[layer], moe_b_sub[layer])
        ln1 = (row(ln1_g[layer]), row(ln1_b[layer]), wr, br)
        if layer % 2 == 0:
            w1, wuq, wukv = _even_weights(ev_w_in[j], ev_w_uq[j], ev_w_ukv[j])
            qa, kv, kr, qb, kb2, va, qi, ki2, wi = _even_proj(
                x2, w1, row(ev_g_q[j]), row(ev_g_kv[j]), wuq, wukv, tabs)
            o_a = _flash((qa, kv, kr), fox=False, B=B, S=S, scale=(D_NOPE + D_ROPE) ** -0.5)
            wit = jnp.pad(wi[:, :H_IDX].T, ((0, SUBLANES - H_IDX), (0, 0)))
            o_b = _dsa(qi, qb, wit, ki2, kb2, va, B=B, S=S, ksel=ksel)
            wo = ev_w_o[j].astype(BF16)
            n_a = H_MLA * D_V_MLA
            x2, route, counts = _out_ln((o_a, o_b), (wo[:n_a], wo[n_a:]), x2, *ln1)
        else:
            n = H_FOX * D_FOX
            w = jnp.pad(od_w_in[j], ((0, 0), (0, LANES - H_FOX))).astype(BF16)
            bf = jnp.pad(od_b_f[j], (0, LANES - H_FOX))[None].astype(F32)
            q, k, v, lf = _odd_proj(x2, w, bf)
            c = _cumsum_seq(lf, S)
            c_rows = c.reshape(B, S, LANES)[:, :, :H_FOX].swapaxes(1, 2)
            o = _flash((q, k, v, c, c_rows), fox=True, B=B, S=S, scale=D_FOX ** -0.5)
            x2, route, counts = _out_ln((o,), (od_w_o[j].astype(BF16),), x2, *ln1)
        x2 = _moe_ln(x2, route, counts, wr, br, moe_w_gate[layer].astype(BF16), moe_w_up[layer].astype(BF16),
                     moe_w_down[layer].astype(BF16), row(ln2_g[layer]), row(ln2_b[layer]))
    return x2.reshape(B, S, D)
```

```python
import functools

import jax
import jax.numpy as jnp
from jax import lax
from jax.experimental import pallas as pl
from jax.experimental.pallas import tpu as pltpu

D_MODEL = 1024
DEPTH = 4
CHUNK = 64
CHUNK_SHIFT = 6
ROPE_THETA = 10000.0
NEG_INF = -1e30
LN_EPS = 1e-5
RMS_EPS = 1e-6
H_MLA, D_NOPE, D_ROPE, D_V_MLA, Q_LORA, KV_LORA = 8, 64, 32, 64, 384, 256
H_DSA, D_DSA, H_IDX, D_IDX, TOPK_MAX = 8, 64, 4, 64, 256
H_FOX, D_FOX = 16, 64
N_GROUPS, EXPERTS_PER_GROUP, D_EXPERT = 4, 4, 512
N_EXPERTS = N_GROUPS * EXPERTS_PER_GROUP
DN_ALPHA = (2 * DEPTH) ** 0.25
EV_SPLITS = (Q_LORA, KV_LORA, D_ROPE, H_DSA * D_DSA, D_DSA, D_DSA, H_IDX * D_IDX, D_IDX, H_IDX)

LANES = 128
SUBLANES = 8
VMEM_LIMIT_BYTES = 56 * 1024 * 1024

TM_PROJ = 512
TE_MOE = 256
T_ATT = 512
VT_ROWS = 80
Q_DSA = 128
KC_DSA = 256

F32 = jnp.float32
BF16 = jnp.bfloat16
_NT = (((1,), (1,)), ((), ()))
_TN = (((0,), (0,)), ((), ()))


def _params(*sem):
    return pltpu.CompilerParams(dimension_semantics=sem, vmem_limit_bytes=VMEM_LIMIT_BYTES)


def _lane_iota(shape=(1, LANES)):
    return lax.broadcasted_iota(jnp.int32, shape, len(shape) - 1)


def _full_spec(shape):
    return pl.BlockSpec(shape, lambda *_: (0,) * len(shape))


def _rope_table_kernel(pos_ref, inv32_ref, sg32_ref, inv64_ref, sg64_ref,
                       c32_ref, s32_ref, c64_ref, s64_ref):
    pos = pos_ref[...]
    a32 = pos * inv32_ref[...]
    a64 = pos * inv64_ref[...]
    c32_ref[...] = jnp.cos(a32)
    s32_ref[...] = jnp.sin(a32) * sg32_ref[...]
    c64_ref[...] = jnp.cos(a64)
    s64_ref[...] = jnp.sin(a64) * sg64_ref[...]


def _rope_tables(posf):
    T = posf.shape[0]
    tm = min(T, 2048)
    lane = jnp.arange(LANES)
    inv16 = jnp.power(ROPE_THETA, -jnp.arange(0, D_ROPE, 2, dtype=F32) / D_ROPE)
    inv32 = jnp.power(ROPE_THETA, -jnp.arange(0, D_DSA, 2, dtype=F32) / D_DSA)
    in32 = (lane >= D_NOPE) & (lane < D_NOPE + D_ROPE)
    t32 = jnp.where(in32, inv16[(lane - D_NOPE) % (D_ROPE // 2)], 0.0).astype(F32)[None]
    g32 = jnp.where(in32, jnp.where(lane < D_NOPE + D_ROPE // 2, -1.0, 1.0), 0.0).astype(F32)[None]
    t64 = inv32[lane % (D_DSA // 2)].astype(F32)[None]
    g64 = jnp.where((lane % D_DSA) < D_DSA // 2, -1.0, 1.0).astype(F32)[None]
    row = pl.BlockSpec((tm, LANES), lambda i: (i, 0))
    tab = jax.ShapeDtypeStruct((T, LANES), F32)
    return pl.pallas_call(
        _rope_table_kernel,
        grid=(T // tm,),
        in_specs=[pl.BlockSpec((tm, 1), lambda i: (i, 0))] + [_full_spec((1, LANES))] * 4,
        out_specs=[row] * 4,
        out_shape=[tab] * 4,
        compiler_params=_params("parallel"),
        name="rope_tables",
    )(posf, t32, g32, t64, g64)


def _rope_block(y, c, s, half, second_half):
    partner = jnp.where(second_half, pltpu.roll(y, half, 1), pltpu.roll(y, LANES - half, 1))
    return y * c + partner * s


def _rms(x, g):
    return x * lax.rsqrt(jnp.mean(x * x, axis=-1, keepdims=True) + RMS_EPS) * g


def _layer_norm(z, g, b):
    mu = jnp.mean(z, axis=-1, keepdims=True)
    d = z - mu
    var = jnp.mean(d * d, axis=-1, keepdims=True)
    return d * lax.rsqrt(var + LN_EPS) * g + b


_E_CQ, _E_CKV, _E_KR, _E_QB, _E_KB, _E_VB, _E_QI, _E_KI, _E_WI, _E_END = (
    0, 384, 640, 768, 1280, 1408, 1536, 1792, 1920, 2048)


def _even_proj_kernel(x_ref, w1_ref, gq_ref, gkv_ref, wuq_ref, wukv_ref,
                      c32_ref, s32_ref, c64_ref, s64_ref,
                      qa_ref, kv_ref, kr_ref, qb_ref, kb_ref, va_ref, qi_ref, ki_ref, wi_ref):
    xb = x_ref[...].astype(BF16)
    lane = _lane_iota()
    sec32 = (lane >= D_NOPE + D_ROPE // 2) & (lane < D_NOPE + D_ROPE)
    sec64 = (lane % D_DSA) >= D_DSA // 2
    c32, s32, c64, s64 = c32_ref[...], s32_ref[...], c64_ref[...], s64_ref[...]

    def seg(a, b):
        return jnp.dot(xb, w1_ref[:, a:b], preferred_element_type=F32)

    def rope64_store(h, out_ref):
        for j in range(h.shape[1] // LANES):
            blk = h[:, j * LANES:(j + 1) * LANES]
            out_ref[:, j * LANES:(j + 1) * LANES] = _rope_block(
                blk, c64, s64, D_DSA // 2, sec64).astype(out_ref.dtype)

    cq = _rms(seg(_E_CQ, _E_CKV), gq_ref[...]).astype(BF16)
    q = jnp.dot(cq, wuq_ref[...], preferred_element_type=F32)
    for h in range(H_MLA):
        blk = q[:, h * LANES:(h + 1) * LANES]
        qa_ref[:, h * LANES:(h + 1) * LANES] = _rope_block(
            blk, c32, s32, D_ROPE // 2, sec32).astype(BF16)
    ckv = _rms(seg(_E_CKV, _E_KR), gkv_ref[...]).astype(BF16)
    kv_ref[...] = jnp.dot(ckv, wukv_ref[...], preferred_element_type=F32).astype(BF16)
    kr_ref[...] = _rope_block(seg(_E_KR, _E_QB), c32, s32, D_ROPE // 2, sec32).astype(BF16)
    rope64_store(seg(_E_QB, _E_KB), qb_ref)
    rope64_store(seg(_E_KB, _E_VB), kb_ref)
    va_ref[...] = jnp.where(lane < D_DSA, seg(_E_VB, _E_QI), 1.0).astype(BF16)
    rope64_store(seg(_E_QI, _E_KI), qi_ref)
    rope64_store(seg(_E_KI, _E_WI), ki_ref)
    wi_ref[...] = seg(_E_WI, _E_END)


def _even_weights(w_in, w_uq, w_ukv):
    o = [0]
    for s_ in EV_SPLITS:
        o.append(o[-1] + s_)
    cq, ckv, kr, qb, kb, vb, qi, ki, wi = (w_in[:, o[i]:o[i + 1]] for i in range(9))
    z = lambda n: jnp.zeros((D_MODEL, n), w_in.dtype)
    w1 = jnp.concatenate([
        cq, ckv,
        z(D_NOPE), kr, z(LANES - D_NOPE - D_ROPE),
        qb,
        kb, kb,
        vb, z(LANES - D_DSA),
        qi,
        ki, ki,
        wi, z(LANES - H_IDX)], axis=1).astype(BF16)
    wq = w_uq.reshape(Q_LORA, H_MLA, D_NOPE + D_ROPE)
    wq = jnp.pad(wq, ((0, 0), (0, 0), (0, LANES - D_NOPE - D_ROPE))).reshape(Q_LORA, H_MLA * LANES)
    return w1, wq.astype(BF16), w_ukv.astype(BF16)


def _even_proj(x2, w1, gq, gkv, wuq, wukv, tabs):
    T = x2.shape[0]
    tm = min(TM_PROJ, T)
    row = lambda n: pl.BlockSpec((tm, n), lambda i: (i, 0))
    outs = [(H_MLA * LANES, BF16), (H_MLA * LANES, BF16), (LANES, BF16), (H_DSA * D_DSA, BF16),
            (LANES, BF16), (LANES, BF16), (H_IDX * D_IDX, BF16), (LANES, BF16), (LANES, F32)]
    return pl.pallas_call(
        _even_proj_kernel,
        grid=(T // tm,),
        in_specs=[row(D_MODEL), _full_spec(w1.shape), _full_spec(gq.shape), _full_spec(gkv.shape),
                  _full_spec(wuq.shape), _full_spec(wukv.shape)] + [row(LANES)] * 4,
        out_specs=[row(n) for n, _ in outs],
        out_shape=[jax.ShapeDtypeStruct((T, n), d) for n, d in outs],
        compiler_params=_params("parallel"),
        name="even_proj",
    )(x2, w1, gq, gkv, wuq, wukv, *tabs)


def _odd_proj_kernel(x_ref, w_ref, bf_ref, q_ref, k_ref, v_ref, lf_ref):
    xb = x_ref[...].astype(BF16)
    n = H_FOX * D_FOX
    for j, out in enumerate((q_ref, k_ref, v_ref)):
        out[...] = jnp.dot(xb, w_ref[:, j * n:(j + 1) * n], preferred_element_type=F32).astype(BF16)
    z = jnp.dot(xb, w_ref[:, 3 * n:], preferred_element_type=F32) + bf_ref[...]
    lf_ref[...] = jnp.minimum(z, 0.0) - jnp.log1p(jnp.exp(-jnp.abs(z)))


def _odd_proj(x2, w, bf):
    T = x2.shape[0]
    tm = min(TM_PROJ, T)
    n = H_FOX * D_FOX
    row = lambda c: pl.BlockSpec((tm, c), lambda i: (i, 0))
    return pl.pallas_call(
        _odd_proj_kernel,
        grid=(T // tm,),
        in_specs=[row(D_MODEL), _full_spec(w.shape), _full_spec(bf.shape)],
        out_specs=[row(n), row(n), row(n), row(LANES)],
        out_shape=[jax.ShapeDtypeStruct((T, n), BF16)] * 3 + [jax.ShapeDtypeStruct((T, LANES), F32)],
        compiler_params=_params("parallel"),
        name="odd_proj",
    )(x2, w, bf)


def _cumsum_kernel(lf_ref, c_ref):
    a = lf_ref[...]
    rows = lax.broadcasted_iota(jnp.int32, a.shape, 0)
    k = 1
    while k < a.shape[0]:
        a = a + jnp.where(rows >= k, pltpu.roll(a, k, 0), 0.0)
        k *= 2
    c_ref[...] = a


def _cumsum_seq(lf, S):
    T = lf.shape[0]
    blk = pl.BlockSpec((S, LANES), lambda b: (b, 0))
    return pl.pallas_call(
        _cumsum_kernel, grid=(T // S,), in_specs=[blk], out_specs=blk,
        out_shape=jax.ShapeDtypeStruct((T, LANES), F32),
        compiler_params=_params("parallel"), name="forget_cumsum",
    )(lf)


def _flash_kernel(*refs, fox, scale, t, n_tiles, S):
    if fox:
        q_ref, k_ref, v_ref, cc_ref, cr_ref, o_ref, kx_ref, vt_ref, sb0_ref, sb1_ref = refs
    else:
        q_ref, kv_ref, kr_ref, o_ref, kx_ref, vt_ref, sb0_ref, sb1_ref = refs
    sb_refs = (sb0_ref, sb1_ref)
    hp = pl.program_id(1)
    lane = _lane_iota()
    lo = lane < D_FOX
    hi = lane >= D_FOX
    exp2_scale = scale * 1.4426950408889634

    ones = jnp.ones((VT_ROWS - D_FOX, S), BF16)
    if fox:
        vt = jnp.transpose(v_ref[0].astype(F32)).astype(BF16)
    for hh in range(2):
        if fox:
            col = jnp.sum(jnp.where(lane == 2 * hp + hh, cc_ref[0], 0.0), axis=-1, keepdims=True)
            kx_ref[hh] = jnp.broadcast_to(col, (S, LANES))
            vt_ref[hh, :D_FOX, :] = vt[hh * D_FOX:(hh + 1) * D_FOX]
        else:
            kvh = kv_ref[0, :, hh * LANES:(hh + 1) * LANES]
            kx_ref[hh] = jnp.where(lo, kvh, kr_ref[0])
            vt_ref[hh, :D_V_MLA, :] = jnp.transpose(kvh.astype(F32))[D_NOPE:].astype(BF16)
        vt_ref[hh, D_FOX:, :] = ones

    def body(n):
        qrows = slice(n * t, (n + 1) * t)
        half = t // 2
        blocks = [(c * t, t, 0, False) for c in range(n)] + [(n * t, half, 0, True), (n * t + half, half, half, True)]

        def on_lanes(full, q0, part, op):
            if q0 == 0:
                return op(full, part)
            return jnp.concatenate([full[:, :q0], op(full[:, q0:], part)], axis=1)

        outs = []
        for hh in range(2):
            if fox:
                q2 = q_ref[0, qrows, :] * jnp.asarray(scale, BF16)
                qh = jnp.where(lo if hh == 0 else hi, q2, jnp.zeros_like(q2))
                cq = cr_ref[0, pl.ds(2 * hp + hh, 1), qrows]
            else:
                qh = q_ref[0, qrows, hh * LANES:(hh + 1) * LANES]
            m8 = jnp.full((SUBLANES, t), -jnp.inf, F32)
            for k0, nk, q0, masked in blocks:
                nq = t - q0
                kh = k_ref[0, k0:k0 + nk, :] if fox else kx_ref[hh, k0:k0 + nk, :]
                s = lax.dot_general(kh, qh[q0:], _NT, preferred_element_type=F32)
                if fox:
                    s = s + cq[:, q0:] - jnp.concatenate([kx_ref[hh, k0:k0 + nk, :]] * (nq // LANES), axis=1)
                if masked:
                    k_idx = k0 + lax.broadcasted_iota(jnp.int32, (nk, 1), 0)
                    q_idx = n * t + q0 + lax.broadcasted_iota(jnp.int32, (1, nq), 1)
                    vis = (k_idx <= q_idx) if fox else ((k_idx >> CHUNK_SHIFT) <= (q_idx >> CHUNK_SHIFT))
                    s = jnp.where(vis, s, NEG_INF)
                sb_refs[hh][k0:k0 + nk, q0:] = s
                m8 = on_lanes(m8, q0, jnp.max(s.reshape(nk // SUBLANES, SUBLANES, nq), axis=0), jnp.maximum)
            m = jnp.max(m8, axis=0, keepdims=True)
            acc = jnp.zeros((VT_ROWS, t), F32)
            for k0, nk, q0, masked in blocks:
                x = sb_refs[hh][k0:k0 + nk, q0:] - m[:, q0:]
                p = jnp.exp(x) if fox else jnp.exp2(x * exp2_scale)
                pv = jnp.dot(vt_ref[hh, :, k0:k0 + nk], p.astype(BF16), preferred_element_type=F32)
                acc = on_lanes(acc, q0, pv, jnp.add)
            outs.append(acc[:D_FOX] / acc[D_FOX:D_FOX + 1])
        o_ref[0, qrows, :] = jnp.transpose(jnp.concatenate(outs, axis=0)).astype(BF16)

    for n in range(n_tiles):
        body(n)


def _flash(args, *, fox, B, S, scale):
    t = min(T_ATT, S)
    n = S // t
    npairs = (H_FOX if fox else H_MLA) // 2
    args = tuple(a.reshape(B, S, a.shape[-1]) if a.ndim == 2 else a for a in args)
    seq = lambda b, hp: (b, 0, hp)
    whole = lambda b, hp: (b, 0, 0)
    scratch = [pltpu.VMEM((2, S, LANES), F32 if fox else BF16), pltpu.VMEM((2, VT_ROWS, S), BF16),
               pltpu.VMEM((S, t), F32), pltpu.VMEM((S, t), F32)]
    if fox:
        in_specs = [pl.BlockSpec((1, S, LANES), seq), pl.BlockSpec((1, S, LANES), seq),
                    pl.BlockSpec((1, S, LANES), seq), pl.BlockSpec((1, S, LANES), whole),
                    pl.BlockSpec((1, H_FOX, S), whole)]
    else:
        in_specs = [pl.BlockSpec((1, S, 2 * LANES), seq), pl.BlockSpec((1, S, 2 * LANES), seq),
                    pl.BlockSpec((1, S, LANES), whole)]
    out = pl.pallas_call(
        functools.partial(_flash_kernel, fox=fox, scale=scale, t=t, n_tiles=n, S=S),
        grid=(B, npairs),
        in_specs=in_specs,
        out_specs=pl.BlockSpec((1, S, LANES), seq),
        out_shape=jax.ShapeDtypeStruct((B, S, npairs * LANES), BF16),
        scratch_shapes=scratch,
        compiler_params=_params("parallel", "parallel"),
        name="fox_attention" if fox else "mla_attention",
    )(*args)
    return out.reshape(B * S, npairs * LANES)


def _dsa_kernel(qi_ref, qb_ref, wit_ref, ki_ref, kb_ref, va_ref, o_ref,
                qim_ref, qbm_ref, sc_ref, sb_ref, *, nk, i0, ksel):
    i = i0 + pl.program_id(1)
    chunks = [slice(k0, min(k0 + KC_DSA, nk)) for k0 in range(0, nk, KC_DSA)]
    lane = _lane_iota()
    lo = lane < D_DSA
    hi = lane >= D_DSA
    q_chunk = (i * Q_DSA + lane) >> CHUNK_SHIFT
    select = nk > ksel

    def key_idx(rows):
        return rows.start + lax.broadcasted_iota(jnp.int32, (rows.stop - rows.start, 1), 0)

    def visible(rows, x):
        if rows.stop <= i0 * Q_DSA:
            return x
        return jnp.where((key_idx(rows) >> CHUNK_SHIFT) <= q_chunk, x, NEG_INF)

    def fold_max(s):
        return jnp.max(s.reshape(s.shape[0] // SUBLANES, SUBLANES, s.shape[1]), axis=0)

    for h in range(H_DSA):
        blk = qb_ref[:, (h // 2) * LANES:(h // 2 + 1) * LANES] * jnp.asarray(D_DSA ** -0.5, BF16)
        qbm_ref[h * Q_DSA:(h + 1) * Q_DSA, :] = jnp.where(lo if h % 2 == 0 else hi, blk, jnp.zeros_like(blk))

    if select:
        for h in range(H_IDX):
            blk = qi_ref[:, (h // 2) * LANES:(h // 2 + 1) * LANES]
            qim_ref[h * Q_DSA:(h + 1) * Q_DSA, :] = jnp.where(lo if h % 2 == 0 else hi, blk, jnp.zeros_like(blk))
        w_flat = jnp.concatenate([wit_ref[h:h + 1, :] for h in range(H_IDX)], axis=1)
        for rows in chunks:
            r = lax.dot_general(ki_ref[0, rows, :], qim_ref[...], _NT, preferred_element_type=F32)
            r = jnp.maximum(r, 0.0) * w_flat
            acc = r[:, :LANES]
            for h in range(1, H_IDX):
                acc = acc + r[:, h * LANES:(h + 1) * LANES]
            sc_ref[rows, :] = visible(rows, acc)

        def count(mask):
            part = jnp.sum(mask.astype(F32).reshape(nk // 32, 32, LANES), axis=0)
            return jnp.sum(part, axis=0, keepdims=True)

        def as_float(key):
            return lax.bitcast_convert_type(jnp.where(key >= 0, key, key ^ 0x7FFFFFFF), F32)

        def bit_body(r, ans):
            cand = ans + lax.shift_left(jnp.int32(1), 31 - r)
            cnt = count(sc_ref[...] >= as_float(cand))
            return jnp.where(cnt >= ksel, cand, ans)
        ans = lax.fori_loop(0, 32, bit_body, jnp.full((1, LANES), jnp.iinfo(jnp.int32).min, jnp.int32))
        t = as_float(ans)
        has_ties = jnp.max(count(sc_ref[...] >= t)) > ksel

        @pl.when(jnp.logical_not(has_ties))
        def _():
            for rows in chunks:
                sc_ref[rows, :] = visible(rows, jnp.where(sc_ref[rows, :] >= t, 0.0, NEG_INF))

        @pl.when(has_ties)
        def _():
            need = ksel - count(sc_ref[...] > t)
            k_all = lax.broadcasted_iota(jnp.int32, (nk, 1), 0)

            def bit_body2(r, aj):
                cand = aj + lax.shift_left(jnp.int32(1), 11 - r)
                g = count((sc_ref[...] == t) & (k_all < cand))
                return jnp.where(g <= need, cand, aj)
            jmax = lax.fori_loop(0, 12, bit_body2, jnp.zeros((1, LANES), jnp.int32))
            for rows in chunks:
                s = sc_ref[rows, :]
                sel = (s > t) | ((s == t) & (key_idx(rows) < jmax))
                sc_ref[rows, :] = visible(rows, jnp.where(sel, 0.0, NEG_INF))
    else:
        for rows in chunks:
            sc_ref[rows, :] = visible(rows, jnp.zeros((rows.stop - rows.start, LANES), F32))

    m8 = jnp.full((SUBLANES, H_DSA * LANES), -jnp.inf, F32)
    for rows in chunks:
        s = lax.dot_general(kb_ref[0, rows, :], qbm_ref[...], _NT, preferred_element_type=F32)
        s = s + jnp.concatenate([sc_ref[rows, :]] * H_DSA, axis=1)
        sb_ref[rows, :] = s
        m8 = jnp.maximum(m8, fold_max(s))
    m = jnp.max(m8, axis=0, keepdims=True)
    acc = jnp.zeros((H_DSA * Q_DSA, LANES), F32)
    for rows in chunks:
        p = jnp.exp(sb_ref[rows, :] - m).astype(BF16)
        acc = acc + lax.dot_general(p, va_ref[0, rows, :], _TN, preferred_element_type=F32)

    for hp in range(H_DSA // 2):
        a = acc[(2 * hp) * Q_DSA:(2 * hp + 1) * Q_DSA]
        b = acc[(2 * hp + 1) * Q_DSA:(2 * hp + 2) * Q_DSA]
        ra = a / pltpu.roll(a, D_DSA, 1)
        rb = b / pltpu.roll(b, D_DSA, 1)
        o_ref[:, hp * LANES:(hp + 1) * LANES] = jnp.where(lo, ra, pltpu.roll(rb, D_DSA, 1)).astype(BF16)


def _dsa(qi, qb, wit, ki2, kb2, va, *, B, S, ksel):
    nb = S // Q_DSA
    per = KC_DSA // Q_DSA
    seq3 = lambda a: a.reshape(B, S, LANES)
    ki3, kb3, va3 = seq3(ki2), seq3(kb2), seq3(va)
    outs = []
    for g in range(nb // per):
        nk, i0 = (g + 1) * per * Q_DSA, g * per
        qrow = lambda n, i0=i0: pl.BlockSpec((Q_DSA, n), lambda b, j: (b * nb + i0 + j, 0))
        seq = pl.BlockSpec((1, nk, LANES), lambda b, j: (b, 0, 0))
        o = pl.pallas_call(
            functools.partial(_dsa_kernel, nk=nk, i0=i0, ksel=ksel),
            grid=(B, per),
            in_specs=[qrow(H_IDX * D_IDX), qrow(H_DSA * D_DSA),
                      pl.BlockSpec((SUBLANES, Q_DSA), lambda b, j, i0=i0: (0, b * nb + i0 + j)),
                      seq, seq, seq],
            out_specs=pl.BlockSpec((Q_DSA, H_DSA * D_DSA), lambda b, j: (b * per + j, 0)),
            out_shape=jax.ShapeDtypeStruct((B * per * Q_DSA, H_DSA * D_DSA), BF16),
            scratch_shapes=[pltpu.VMEM((H_IDX * Q_DSA, LANES), BF16), pltpu.VMEM((H_DSA * Q_DSA, LANES), BF16),
                            pltpu.VMEM((nk, LANES), F32), pltpu.VMEM((nk, H_DSA * LANES), F32)],
            compiler_params=_params("parallel", "arbitrary"),
            name="dsa_attention_%d" % nk,
        )(qi, qb, wit, ki3, kb3, va3)
        outs.append(o.reshape(B, per * Q_DSA, H_DSA * D_DSA))
    return jnp.concatenate(outs, axis=1).reshape(B * S, H_DSA * D_DSA)


_R_SUB = N_GROUPS
_PAIRS = [(a, b) for a in range(EXPERTS_PER_GROUP) for b in range(a + 1, EXPERTS_PER_GROUP)]
N_CLASSES = N_GROUPS * len(_PAIRS)


def _out_ln_kernel(*refs, n_in):
    o_refs, w_refs = refs[:n_in], refs[n_in:2 * n_in]
    x_ref, g_ref, b_ref, wrt_ref, brt_ref, tri_ref, out_ref, route_ref, cnt_ref, run_ref = refs[2 * n_in:]
    y = jnp.dot(o_refs[0][...], w_refs[0][...], preferred_element_type=F32)
    for o_r, w_r in zip(o_refs[1:], w_refs[1:]):
        y = y + jnp.dot(o_r[...], w_r[...], preferred_element_type=F32)
    x_new = _layer_norm(DN_ALPHA * x_ref[...] + y, g_ref[...], b_ref[...])
    out_ref[...] = x_new
    _route_rows(x_new, wrt_ref, brt_ref, tri_ref, route_ref, cnt_ref, run_ref)


def _out_ln(os_, ws, x2, g, b, wr, br):
    T = x2.shape[0]
    tm = min(TM_PROJ, T)
    row = lambda c: pl.BlockSpec((tm, c), lambda i: (i, 0))
    tri = jnp.triu(jnp.ones((tm, tm), BF16), 1)
    wrt, brt = wr.T, br.T
    return pl.pallas_call(
        functools.partial(_out_ln_kernel, n_in=len(os_)),
        grid=(T // tm,),
        in_specs=[row(o.shape[1]) for o in os_] + [_full_spec(w.shape) for w in ws]
                 + [row(D_MODEL), _full_spec(g.shape), _full_spec(b.shape),
                    _full_spec(wrt.shape), _full_spec(brt.shape), _full_spec(tri.shape)],
        out_specs=[row(D_MODEL), pl.BlockSpec((1, SUBLANES, tm), lambda i: (i, 0, 0)), _full_spec((LANES, LANES))],
        out_shape=[jax.ShapeDtypeStruct((T, D_MODEL), F32), jax.ShapeDtypeStruct((T // tm, SUBLANES, tm), F32),
                   jax.ShapeDtypeStruct((LANES, LANES), F32)],
        scratch_shapes=[pltpu.VMEM((LANES, 1), F32)],
        compiler_params=_params("arbitrary"),
        name="out_proj_ln",
    )(*os_, *ws, x2, g, b, wrt, brt, tri)


def _route_rows(x, wrt_ref, brt_ref, tri_ref, route_ref, cnt_ref, run_ref):
    @pl.when(pl.program_id(0) == 0)
    def _():
        run_ref[...] = jnp.zeros(run_ref.shape, F32)

    tm = x.shape[0]
    lg = lax.dot_general(wrt_ref[...], x.astype(BF16), _NT, preferred_element_type=F32) + brt_ref[...]
    r = lax.broadcasted_iota(jnp.int32, (LANES, 1), 0)
    rf = r.astype(F32)
    far = float(LANES)
    grp = jnp.where(r < N_GROUPS, lg, -jnp.inf)
    mg = jnp.max(grp, axis=0, keepdims=True)
    g_star = jnp.min(jnp.where(grp == mg, rf, far), axis=0, keepdims=True)
    row_grp = ((r - _R_SUB) >> 2).astype(F32)
    in_grp = (row_grp == g_star) & (r >= _R_SUB) & (r < _R_SUB + N_EXPERTS)
    l1 = jnp.where(in_grp, lg, -jnp.inf)
    v1 = jnp.max(l1, axis=0, keepdims=True)
    j1 = jnp.min(jnp.where(l1 == v1, rf, far), axis=0, keepdims=True)
    l2 = jnp.where(in_grp & (rf != j1), lg, -jnp.inf)
    v2 = jnp.max(l2, axis=0, keepdims=True)
    j2 = jnp.min(jnp.where(l2 == v2, rf, far), axis=0, keepdims=True)
    base = _R_SUB + EXPERTS_PER_GROUP * g_star
    ja = jnp.minimum(j1, j2) - base
    jb = jnp.maximum(j1, j2) - base
    cls = g_star * len(_PAIRS) + ja * (7.0 - ja) * 0.5 + (jb - ja - 1.0)
    onehot = rf == cls
    ones = jnp.where(onehot, 1.0, 0.0)
    before = jnp.dot(ones.astype(BF16), tri_ref[...], preferred_element_type=F32)
    rank = jnp.sum(jnp.where(onehot, before + run_ref[...], 0.0), axis=0, keepdims=True)
    row8 = lax.broadcasted_iota(jnp.int32, (SUBLANES, 1), 0)
    route_ref[0] = jnp.where(row8 == 0, cls, jnp.where(row8 == 1, rank, 0.0))
    run_ref[...] += jnp.sum(ones, axis=1, keepdims=True)
    cnt_ref[...] = jnp.broadcast_to(run_ref[...], cnt_ref.shape)


def _row_copy(src_ref, dst_ref, src_row, dst_row, sem):
    return pltpu.make_async_copy(src_ref.at[pl.ds(src_row, 1), :], dst_ref.at[pl.ds(dst_row, 1), :], sem)


def _dispatch_kernel(pos_ref, x_ref, init_ref, xs_ref, sem):
    del init_ref
    tm = x_ref.shape[0]

    def start(g, carry):
        base = pl.multiple_of(g * SUBLANES, SUBLANES)
        for j in range(SUBLANES):
            _row_copy(x_ref, xs_ref, base + j, pos_ref[0, 0, base + j], sem).start()
        return carry
    lax.fori_loop(0, tm // SUBLANES, start, 0)

    def wait(t, carry):
        _row_copy(x_ref, xs_ref, 0, 0, sem).wait()
        return carry
    lax.fori_loop(0, tm, wait, 0, unroll=8)


def _collect_kernel(pos_ref, ys_ref, out_ref, sem):
    tm = out_ref.shape[0]

    def start(g, carry):
        base = pl.multiple_of(g * SUBLANES, SUBLANES)
        for j in range(SUBLANES):
            _row_copy(ys_ref, out_ref, pos_ref[0, 0, base + j], base + j, sem).start()
        return carry
    lax.fori_loop(0, tm // SUBLANES, start, 0)

    def wait(t, carry):
        _row_copy(ys_ref, out_ref, 0, 0, sem).wait()
        return carry
    lax.fori_loop(0, tm, wait, 0, unroll=8)


def _dispatch(x2, pos3, n_rows):
    T = x2.shape[0]
    tm = pos3.shape[-1]
    return pl.pallas_call(
        _dispatch_kernel,
        grid=(T // tm,),
        in_specs=[pl.BlockSpec((1, 1, tm), lambda i: (i, 0, 0), memory_space=pltpu.SMEM),
                  pl.BlockSpec((tm, D_MODEL), lambda i: (i, 0)),
                  pl.BlockSpec(memory_space=pl.ANY)],
        out_specs=pl.BlockSpec(memory_space=pl.ANY),
        out_shape=jax.ShapeDtypeStruct((n_rows, D_MODEL), F32),
        scratch_shapes=[pltpu.SemaphoreType.DMA(())],
        input_output_aliases={2: 0},
        compiler_params=_params("arbitrary"),
        name="moe_dispatch",
    )(pos3, x2, jnp.zeros((n_rows, D_MODEL), F32))


def _collect(ys, pos3):
    T = pos3.shape[0] * pos3.shape[-1]
    tm = pos3.shape[-1]
    return pl.pallas_call(
        _collect_kernel,
        grid=(T // tm,),
        in_specs=[pl.BlockSpec((1, 1, tm), lambda i: (i, 0, 0), memory_space=pltpu.SMEM),
                  pl.BlockSpec(memory_space=pl.ANY)],
        out_specs=pl.BlockSpec((tm, D_MODEL), lambda i: (i, 0)),
        out_shape=jax.ShapeDtypeStruct((T, D_MODEL), F32),
        scratch_shapes=[pltpu.SemaphoreType.DMA(())],
        compiler_params=_params("arbitrary"),
        name="moe_collect",
    )(pos3, ys)


def _expert_kernel(tg_ref, ta_ref, tb_ref, tv_ref, x_ref, wr_ref, br_ref,
                   wga_ref, wua_ref, wda_ref, wgb_ref, wub_ref, wdb_ref, g_ref, b_ref, out_ref):
    r = pl.program_id(0)

    @pl.when(tv_ref[r] == 0)
    def _():
        out_ref[...] = jnp.zeros(out_ref.shape, F32)

    @pl.when(tv_ref[r] != 0)
    def _():
        x = x_ref[...]
        xb = x.astype(BF16)
        lg = jnp.dot(xb, wr_ref[...], preferred_element_type=F32) + br_ref[...]
        lane = _lane_iota()
        is_grp = lane < N_GROUPS
        mg = jnp.max(jnp.where(is_grp, lg, -jnp.inf), axis=-1, keepdims=True)
        den = jnp.sum(jnp.where(is_grp, jnp.exp(lg - mg), 0.0), axis=-1, keepdims=True)
        pick = lambda l: jnp.sum(jnp.where(lane == l, lg, 0.0), axis=-1, keepdims=True)
        p_top = jnp.exp(pick(tg_ref[r]) - mg) / den
        va, vb = pick(_R_SUB + ta_ref[r]), pick(_R_SUB + tb_ref[r])
        e = jnp.exp(jnp.minimum(va, vb) - jnp.maximum(va, vb))
        w_top, w_oth = 1.0 / (1.0 + e), e / (1.0 + e)
        a_top = va >= vb
        gates = (jnp.where(a_top, w_top, w_oth) * p_top, jnp.where(a_top, w_oth, w_top) * p_top)
        ys = []
        for gate, wg, wu, wd in ((gates[0], wga_ref, wua_ref, wda_ref), (gates[1], wgb_ref, wub_ref, wdb_ref)):
            a = jnp.dot(xb, wg[0], preferred_element_type=F32)
            u = jnp.dot(xb, wu[0], preferred_element_type=F32)
            h = (a * (1.0 / (1.0 + jnp.exp(-a))) * u).astype(BF16)
            ys.append(gate * jnp.dot(h, wd[0], preferred_element_type=F32))
        out_ref[...] = _layer_norm(DN_ALPHA * x + (ys[0] + ys[1]), g_ref[...], b_ref[...])


def _experts(xs, tile_g, tile_a, tile_b, tile_v, wr, br, wg, wu, wd, g, b):
    n_tiles = tile_g.shape[0]
    te = xs.shape[0] // n_tiles
    row = pl.BlockSpec((te, D_MODEL), lambda r, *_: (r, 0))
    full = lambda shape: pl.BlockSpec(shape, lambda r, *_: (0,) * len(shape))
    up_a = pl.BlockSpec((1, D_MODEL, D_EXPERT), lambda r, tg, ta, tb, tv: (ta[r], 0, 0))
    up_b = pl.BlockSpec((1, D_MODEL, D_EXPERT), lambda r, tg, ta, tb, tv: (tb[r], 0, 0))
    dn_a = pl.BlockSpec((1, D_EXPERT, D_MODEL), lambda r, tg, ta, tb, tv: (ta[r], 0, 0))
    dn_b = pl.BlockSpec((1, D_EXPERT, D_MODEL), lambda r, tg, ta, tb, tv: (tb[r], 0, 0))
    return pl.pallas_call(
        _expert_kernel,
        grid_spec=pltpu.PrefetchScalarGridSpec(
            num_scalar_prefetch=4, grid=(n_tiles,),
            in_specs=[row, full(wr.shape), full(br.shape), up_a, up_a, dn_a, up_b, up_b, dn_b,
                      full(g.shape), full(b.shape)],
            out_specs=row),
        out_shape=jax.ShapeDtypeStruct(xs.shape, F32),
        compiler_params=_params("arbitrary"),
        name="moe_experts",
    )(tile_g, tile_a, tile_b, tile_v, xs, wr, br, wg, wu, wd, wg, wu, wd, g, b)


def _slot_kernel(route_ref, start_ref, pos_ref):
    cls, rank = route_ref[0, 0:1, :], route_ref[0, 1:2, :]
    rf = lax.broadcasted_iota(jnp.int32, (LANES, 1), 0).astype(F32)
    start = jnp.sum(jnp.where(rf == cls, start_ref[...], 0.0), axis=0, keepdims=True)
    pos_ref[0] = (start + rank).astype(jnp.int32)


def _slots(route, starts):
    n, _, tm = route.shape
    return pl.pallas_call(
        _slot_kernel,
        grid=(n,),
        in_specs=[pl.BlockSpec((1, SUBLANES, tm), lambda i: (i, 0, 0)), _full_spec(starts.shape)],
        out_specs=pl.BlockSpec((1, 1, tm), lambda i: (i, 0, 0)),
        out_shape=jax.ShapeDtypeStruct((n, 1, tm), jnp.int32),
        compiler_params=_params("parallel"),
        name="moe_slots",
    )(route, starts)


def _moe_ln(x2, route, counts, wr, br, wg, wu, wd, g, b):
    T = x2.shape[0]
    te = TE_MOE
    n_tiles = (T + N_CLASSES * (te - 1) + te - 1) // te
    cnt = counts[:N_CLASSES, 0].astype(jnp.int32)
    padded = (cnt + te - 1) // te * te
    ends = jnp.cumsum(padded)
    starts = jnp.pad((ends - padded).astype(F32), (0, LANES - N_CLASSES))[:, None]
    pos3 = _slots(route, starts)
    tile_start = jnp.arange(n_tiles, dtype=jnp.int32) * te
    tile_cls = jnp.minimum(jnp.sum(tile_start[:, None] >= ends[None, :], axis=1), N_CLASSES - 1).astype(jnp.int32)
    tile_v = (tile_start < ends[-1]).astype(jnp.int32)
    pair = jnp.asarray(_PAIRS, jnp.int32)[tile_cls % len(_PAIRS)]
    tile_g = tile_cls // len(_PAIRS)
    tile_a = tile_g * EXPERTS_PER_GROUP + pair[:, 0]
    tile_b = tile_g * EXPERTS_PER_GROUP + pair[:, 1]
    xs = _dispatch(x2, pos3, n_tiles * te)
    ys = _experts(xs, tile_g, tile_a, tile_b, tile_v, wr, br, wg, wu, wd, g, b)
    return _collect(ys, pos3)


def _router_weights(w_grp, b_grp, w_sub, b_sub):
    pad = LANES - N_GROUPS - N_EXPERTS
    wr = jnp.concatenate([w_grp, w_sub, jnp.zeros((D_MODEL, pad), w_grp.dtype)], axis=1).astype(BF16)
    br = jnp.concatenate([b_grp, b_sub, jnp.zeros((pad,), b_grp.dtype)])[None].astype(F32)
    return wr, br


def kernel(x, positions, ev_w_in, ev_g_q, ev_g_kv, ev_w_uq, ev_w_ukv, ev_w_o, od_w_in, od_b_f, od_w_o,
           moe_w_grp, moe_b_grp, moe_w_sub, moe_b_sub, moe_w_gate, moe_w_up, moe_w_down,
           ln1_g, ln1_b, ln2_g, ln2_b):
    B, S, D = x.shape
    T = B * S
    ksel = min(TOPK_MAX, S // 4)
    x2 = x.reshape(T, D)
    tabs = _rope_tables(positions.reshape(T, 1).astype(F32))
    row = lambda v: v[None].astype(F32)
    for layer in range(DEPTH):
        j = layer // 2
        wr, br = _router_weights(moe_w_grp[layer], moe_b_grp[layer], moe_w_sub[layer], moe_b_sub[layer])
        ln1 = (row(ln1_g[layer]), row(ln1_b[layer]), wr, br)
        if layer % 2 == 0:
            w1, wuq, wukv = _even_weights(ev_w_in[j], ev_w_uq[j], ev_w_ukv[j])
            qa, kv, kr, qb, kb2, va, qi, ki2, wi = _even_proj(
                x2, w1, row(ev_g_q[j]), row(ev_g_kv[j]), wuq, wukv, tabs)
            o_a = _flash((qa, kv, kr), fox=False, B=B, S=S, scale=(D_NOPE + D_ROPE) ** -0.5)
            wit = jnp.pad(wi[:, :H_IDX].T, ((0, SUBLANES - H_IDX), (0, 0)))
            o_b = _dsa(qi, qb, wit, ki2, kb2, va, B=B, S=S, ksel=ksel)
            wo = ev_w_o[j].astype(BF16)
            n_a = H_MLA * D_V_MLA
            x2, route, counts = _out_ln((o_a, o_b), (wo[:n_a], wo[n_a:]), x2, *ln1)
        else:
            n = H_FOX * D_FOX
            w = jnp.pad(od_w_in[j], ((0, 0), (0, LANES - H_FOX))).astype(BF16)
            bf = jnp.pad(od_b_f[j], (0, LANES - H_FOX))[None].astype(F32)
            q, k, v, lf = _odd_proj(x2, w, bf)
            c = _cumsum_seq(lf, S)
            c_rows = c.reshape(B, S, LANES)[:, :, :H_FOX].swapaxes(1, 2)
            o = _flash((q, k, v, c, c_rows), fox=True, B=B, S=S, scale=D_FOX ** -0.5)
            x2, route, counts = _out_ln((o,), (od_w_o[j].astype(BF16),), x2, *ln1)
        x2 = _moe_ln(x2, route, counts, wr, br, moe_w_gate[layer].astype(BF16), moe_w_up[layer].astype(BF16),
                     moe_w_down[layer].astype(BF16), row(ln2_g[layer]), row(ln2_b[layer]))
    return x2.reshape(B, S, D)
```

```python
import functools

import jax
import jax.numpy as jnp
from jax import lax
from jax.experimental import pallas as pl
from jax.experimental.pallas import tpu as pltpu

D_MODEL = 1024
DEPTH = 4
CHUNK = 64
CHUNK_SHIFT = 6
ROPE_THETA = 10000.0
NEG_INF = -1e30
LN_EPS = 1e-5
RMS_EPS = 1e-6
H_MLA, D_NOPE, D_ROPE, D_V_MLA, Q_LORA, KV_LORA = 8, 64, 32, 64, 384, 256
H_DSA, D_DSA, H_IDX, D_IDX, TOPK_MAX = 8, 64, 4, 64, 256
H_FOX, D_FOX = 16, 64
N_GROUPS, EXPERTS_PER_GROUP, D_EXPERT = 4, 4, 512
N_EXPERTS = N_GROUPS * EXPERTS_PER_GROUP
DN_ALPHA = (2 * DEPTH) ** 0.25
EV_SPLITS = (Q_LORA, KV_LORA, D_ROPE, H_DSA * D_DSA, D_DSA, D_DSA, H_IDX * D_IDX, D_IDX, H_IDX)

LANES = 128
SUBLANES = 8
VMEM_LIMIT_BYTES = 56 * 1024 * 1024

TM_PROJ = 1024
TM_MOVE = 2048
TE_MOE = 256
T_ATT = 512
VT_ROWS = 80
Q_DSA = 128
KC_DSA = 256

F32 = jnp.float32
BF16 = jnp.bfloat16
_NT = (((1,), (1,)), ((), ()))
_TN = (((0,), (0,)), ((), ()))


def _params(*sem):
    return pltpu.CompilerParams(dimension_semantics=sem, vmem_limit_bytes=VMEM_LIMIT_BYTES)


def _lane_iota(shape=(1, LANES)):
    return lax.broadcasted_iota(jnp.int32, shape, len(shape) - 1)


def _full_spec(shape):
    return pl.BlockSpec(shape, lambda *_: (0,) * len(shape))


def _rope_table_kernel(pos_ref, inv32_ref, sg32_ref, inv64_ref, sg64_ref,
                       c32_ref, s32_ref, c64_ref, s64_ref):
    pos = pos_ref[...]
    a32 = pos * inv32_ref[...]
    a64 = pos * inv64_ref[...]
    c32_ref[...] = jnp.cos(a32)
    s32_ref[...] = jnp.sin(a32) * sg32_ref[...]
    c64_ref[...] = jnp.cos(a64)
    s64_ref[...] = jnp.sin(a64) * sg64_ref[...]


def _rope_tables(posf):
    T = posf.shape[0]
    tm = min(T, 2048)
    lane = jnp.arange(LANES)
    inv16 = jnp.power(ROPE_THETA, -jnp.arange(0, D_ROPE, 2, dtype=F32) / D_ROPE)
    inv32 = jnp.power(ROPE_THETA, -jnp.arange(0, D_DSA, 2, dtype=F32) / D_DSA)
    in32 = (lane >= D_NOPE) & (lane < D_NOPE + D_ROPE)
    t32 = jnp.where(in32, inv16[(lane - D_NOPE) % (D_ROPE // 2)], 0.0).astype(F32)[None]
    g32 = jnp.where(in32, jnp.where(lane < D_NOPE + D_ROPE // 2, -1.0, 1.0), 0.0).astype(F32)[None]
    t64 = inv32[lane % (D_DSA // 2)].astype(F32)[None]
    g64 = jnp.where((lane % D_DSA) < D_DSA // 2, -1.0, 1.0).astype(F32)[None]
    row = pl.BlockSpec((tm, LANES), lambda i: (i, 0))
    tab = jax.ShapeDtypeStruct((T, LANES), F32)
    return pl.pallas_call(
        _rope_table_kernel,
        grid=(T // tm,),
        in_specs=[pl.BlockSpec((tm, 1), lambda i: (i, 0))] + [_full_spec((1, LANES))] * 4,
        out_specs=[row] * 4,
        out_shape=[tab] * 4,
        compiler_params=_params("parallel"),
        name="rope_tables",
    )(posf, t32, g32, t64, g64)


def _rope_block(y, c, s, half, second_half):
    partner = jnp.where(second_half, pltpu.roll(y, half, 1), pltpu.roll(y, LANES - half, 1))
    return y * c + partner * s


def _rms(x, g):
    return x * lax.rsqrt(jnp.mean(x * x, axis=-1, keepdims=True) + RMS_EPS) * g


def _layer_norm(z, g, b):
    mu = jnp.mean(z, axis=-1, keepdims=True)
    d = z - mu
    var = jnp.mean(d * d, axis=-1, keepdims=True)
    return d * lax.rsqrt(var + LN_EPS) * g + b


_E_CQ, _E_CKV, _E_KR, _E_QB, _E_KB, _E_VB, _E_QI, _E_KI, _E_WI, _E_END = (
    0, 384, 640, 768, 1280, 1408, 1536, 1792, 1920, 2048)


def _even_proj_kernel(x_ref, w1_ref, gq_ref, gkv_ref, wuq_ref, wukv_ref,
                      c32_ref, s32_ref, c64_ref, s64_ref,
                      qa_ref, kv_ref, kr_ref, qb_ref, kb_ref, va_ref, qi_ref, ki_ref, wi_ref):
    xb = x_ref[...].astype(BF16)
    lane = _lane_iota()
    sec32 = (lane >= D_NOPE + D_ROPE // 2) & (lane < D_NOPE + D_ROPE)
    sec64 = (lane % D_DSA) >= D_DSA // 2
    c32, s32, c64, s64 = c32_ref[...], s32_ref[...], c64_ref[...], s64_ref[...]

    def seg(a, b):
        return jnp.dot(xb, w1_ref[:, a:b], preferred_element_type=F32)

    def rope64_store(h, out_ref):
        for j in range(h.shape[1] // LANES):
            blk = h[:, j * LANES:(j + 1) * LANES]
            out_ref[:, j * LANES:(j + 1) * LANES] = _rope_block(
                blk, c64, s64, D_DSA // 2, sec64).astype(out_ref.dtype)

    cq = _rms(seg(_E_CQ, _E_CKV), gq_ref[...]).astype(BF16)
    q = jnp.dot(cq, wuq_ref[...], preferred_element_type=F32)
    for h in range(H_MLA):
        blk = q[:, h * LANES:(h + 1) * LANES]
        qa_ref[:, h * LANES:(h + 1) * LANES] = _rope_block(
            blk, c32, s32, D_ROPE // 2, sec32).astype(BF16)
    ckv = _rms(seg(_E_CKV, _E_KR), gkv_ref[...]).astype(BF16)
    kv_ref[...] = jnp.dot(ckv, wukv_ref[...], preferred_element_type=F32).astype(BF16)
    kr_ref[...] = _rope_block(seg(_E_KR, _E_QB), c32, s32, D_ROPE // 2, sec32).astype(BF16)
    rope64_store(seg(_E_QB, _E_KB), qb_ref)
    rope64_store(seg(_E_KB, _E_VB), kb_ref)
    va_ref[...] = jnp.where(lane < D_DSA, seg(_E_VB, _E_QI), 1.0).astype(BF16)
    rope64_store(seg(_E_QI, _E_KI), qi_ref)
    rope64_store(seg(_E_KI, _E_WI), ki_ref)
    wi_ref[...] = seg(_E_WI, _E_END)


def _even_weights(w_in, w_uq, w_ukv):
    o = [0]
    for s_ in EV_SPLITS:
        o.append(o[-1] + s_)
    cq, ckv, kr, qb, kb, vb, qi, ki, wi = (w_in[:, o[i]:o[i + 1]] for i in range(9))
    z = lambda n: jnp.zeros((D_MODEL, n), w_in.dtype)
    w1 = jnp.concatenate([
        cq, ckv,
        z(D_NOPE), kr, z(LANES - D_NOPE - D_ROPE),
        qb,
        kb, kb,
        vb, z(LANES - D_DSA),
        qi,
        ki, ki,
        wi, z(LANES - H_IDX)], axis=1).astype(BF16)
    wq = w_uq.reshape(Q_LORA, H_MLA, D_NOPE + D_ROPE)
    wq = jnp.pad(wq, ((0, 0), (0, 0), (0, LANES - D_NOPE - D_ROPE))).reshape(Q_LORA, H_MLA * LANES)
    return w1, wq.astype(BF16), w_ukv.astype(BF16)


def _even_proj(x2, w1, gq, gkv, wuq, wukv, tabs):
    T = x2.shape[0]
    tm = min(TM_PROJ, T)
    row = lambda n: pl.BlockSpec((tm, n), lambda i: (i, 0))
    outs = [(H_MLA * LANES, BF16), (H_MLA * LANES, BF16), (LANES, BF16), (H_DSA * D_DSA, BF16),
            (LANES, BF16), (LANES, BF16), (H_IDX * D_IDX, BF16), (LANES, BF16), (LANES, F32)]
    return pl.pallas_call(
        _even_proj_kernel,
        grid=(T // tm,),
        in_specs=[row(D_MODEL), _full_spec(w1.shape), _full_spec(gq.shape), _full_spec(gkv.shape),
                  _full_spec(wuq.shape), _full_spec(wukv.shape)] + [row(LANES)] * 4,
        out_specs=[row(n) for n, _ in outs],
        out_shape=[jax.ShapeDtypeStruct((T, n), d) for n, d in outs],
        compiler_params=_params("parallel"),
        name="even_proj",
    )(x2, w1, gq, gkv, wuq, wukv, *tabs)


def _odd_proj_kernel(x_ref, w_ref, bf_ref, q_ref, k_ref, v_ref, lf_ref):
    xb = x_ref[...].astype(BF16)
    n = H_FOX * D_FOX
    for j, out in enumerate((q_ref, k_ref, v_ref)):
        out[...] = jnp.dot(xb, w_ref[:, j * n:(j + 1) * n], preferred_element_type=F32).astype(BF16)
    z = jnp.dot(xb, w_ref[:, 3 * n:], preferred_element_type=F32) + bf_ref[...]
    lf_ref[...] = jnp.minimum(z, 0.0) - jnp.log1p(jnp.exp(-jnp.abs(z)))


def _odd_proj(x2, w, bf):
    T = x2.shape[0]
    tm = min(TM_PROJ, T)
    n = H_FOX * D_FOX
    row = lambda c: pl.BlockSpec((tm, c), lambda i: (i, 0))
    return pl.pallas_call(
        _odd_proj_kernel,
        grid=(T // tm,),
        in_specs=[row(D_MODEL), _full_spec(w.shape), _full_spec(bf.shape)],
        out_specs=[row(n), row(n), row(n), row(LANES)],
        out_shape=[jax.ShapeDtypeStruct((T, n), BF16)] * 3 + [jax.ShapeDtypeStruct((T, LANES), F32)],
        compiler_params=_params("parallel"),
        name="odd_proj",
    )(x2, w, bf)


def _cumsum_kernel(lf_ref, c_ref):
    a = lf_ref[...]
    rows = lax.broadcasted_iota(jnp.int32, a.shape, 0)
    k = 1
    while k < a.shape[0]:
        a = a + jnp.where(rows >= k, pltpu.roll(a, k, 0), 0.0)
        k *= 2
    c_ref[...] = a


def _cumsum_seq(lf, S):
    T = lf.shape[0]
    blk = pl.BlockSpec((S, LANES), lambda b: (b, 0))
    return pl.pallas_call(
        _cumsum_kernel, grid=(T // S,), in_specs=[blk], out_specs=blk,
        out_shape=jax.ShapeDtypeStruct((T, LANES), F32),
        compiler_params=_params("parallel"), name="forget_cumsum",
    )(lf)


def _flash_kernel(*refs, fox, scale, t, n_tiles, S):
    if fox:
        q_ref, k_ref, v_ref, cc_ref, cr_ref, o_ref, kx_ref, vt_ref, sb0_ref, sb1_ref = refs
    else:
        q_ref, kv_ref, kr_ref, o_ref, kx_ref, vt_ref, sb0_ref, sb1_ref = refs
    sb_refs = (sb0_ref, sb1_ref)
    hp = pl.program_id(1)
    lane = _lane_iota()
    lo = lane < D_FOX
    hi = lane >= D_FOX
    exp2_scale = scale * 1.4426950408889634

    ones = jnp.ones((VT_ROWS - D_FOX, S), BF16)
    if fox:
        vt = jnp.transpose(v_ref[0].astype(F32)).astype(BF16)
    for hh in range(2):
        if fox:
            col = jnp.sum(jnp.where(lane == 2 * hp + hh, cc_ref[0], 0.0), axis=-1, keepdims=True)
            kx_ref[hh] = jnp.broadcast_to(col, (S, LANES))
            vt_ref[hh, :D_FOX, :] = vt[hh * D_FOX:(hh + 1) * D_FOX]
        else:
            kvh = kv_ref[0, :, hh * LANES:(hh + 1) * LANES]
            kx_ref[hh] = jnp.where(lo, kvh, kr_ref[0])
            vt_ref[hh, :D_V_MLA, :] = jnp.transpose(kvh.astype(F32))[D_NOPE:].astype(BF16)
        vt_ref[hh, D_FOX:, :] = ones

    def body(n):
        qrows = slice(n * t, (n + 1) * t)
        half = t // 2
        blocks = [(c * t, t, 0, False) for c in range(n)] + [(n * t, half, 0, True), (n * t + half, half, half, True)]

        def on_lanes(full, q0, part, op):
            if q0 == 0:
                return op(full, part)
            return jnp.concatenate([full[:, :q0], op(full[:, q0:], part)], axis=1)

        outs = []
        for hh in range(2):
            if fox:
                q2 = q_ref[0, qrows, :] * jnp.asarray(scale, BF16)
                qh = jnp.where(lo if hh == 0 else hi, q2, jnp.zeros_like(q2))
                cq = cr_ref[0, pl.ds(2 * hp + hh, 1), qrows]
            else:
                qh = q_ref[0, qrows, hh * LANES:(hh + 1) * LANES]
            m8 = jnp.full((SUBLANES, t), -jnp.inf, F32)
            for k0, nk, q0, masked in blocks:
                nq = t - q0
                kh = k_ref[0, k0:k0 + nk, :] if fox else kx_ref[hh, k0:k0 + nk, :]
                s = lax.dot_general(kh, qh[q0:], _NT, preferred_element_type=F32)
                if fox:
                    s = s + cq[:, q0:] - jnp.concatenate([kx_ref[hh, k0:k0 + nk, :]] * (nq // LANES), axis=1)
                if masked:
                    k_idx = k0 + lax.broadcasted_iota(jnp.int32, (nk, 1), 0)
                    q_idx = n * t + q0 + lax.broadcasted_iota(jnp.int32, (1, nq), 1)
                    vis = (k_idx <= q_idx) if fox else ((k_idx >> CHUNK_SHIFT) <= (q_idx >> CHUNK_SHIFT))
                    s = jnp.where(vis, s, NEG_INF)
                sb_refs[hh][k0:k0 + nk, q0:] = s
                m8 = on_lanes(m8, q0, jnp.max(s.reshape(nk // SUBLANES, SUBLANES, nq), axis=0), jnp.maximum)
            m = jnp.max(m8, axis=0, keepdims=True)
            acc = jnp.zeros((VT_ROWS, t), F32)
            for k0, nk, q0, masked in blocks:
                x = sb_refs[hh][k0:k0 + nk, q0:] - m[:, q0:]
                p = jnp.exp(x) if fox else jnp.exp2(x * exp2_scale)
                pv = jnp.dot(vt_ref[hh, :, k0:k0 + nk], p.astype(BF16), preferred_element_type=F32)
                acc = on_lanes(acc, q0, pv, jnp.add)
            outs.append(acc[:D_FOX] / acc[D_FOX:D_FOX + 1])
        o_ref[0, qrows, :] = jnp.transpose(jnp.concatenate(outs, axis=0)).astype(BF16)

    for n in range(n_tiles):
        body(n)


def _flash(args, *, fox, B, S, scale):
    t = min(T_ATT, S)
    n = S // t
    npairs = (H_FOX if fox else H_MLA) // 2
    args = tuple(a.reshape(B, S, a.shape[-1]) if a.ndim == 2 else a for a in args)
    seq = lambda b, hp: (b, 0, hp)
    whole = lambda b, hp: (b, 0, 0)
    scratch = [pltpu.VMEM((2, S, LANES), F32 if fox else BF16), pltpu.VMEM((2, VT_ROWS, S), BF16),
               pltpu.VMEM((S, t), F32), pltpu.VMEM((S, t), F32)]
    if fox:
        in_specs = [pl.BlockSpec((1, S, LANES), seq), pl.BlockSpec((1, S, LANES), seq),
                    pl.BlockSpec((1, S, LANES), seq), pl.BlockSpec((1, S, LANES), whole),
                    pl.BlockSpec((1, H_FOX, S), whole)]
    else:
        in_specs = [pl.BlockSpec((1, S, 2 * LANES), seq), pl.BlockSpec((1, S, 2 * LANES), seq),
                    pl.BlockSpec((1, S, LANES), whole)]
    out = pl.pallas_call(
        functools.partial(_flash_kernel, fox=fox, scale=scale, t=t, n_tiles=n, S=S),
        grid=(B, npairs),
        in_specs=in_specs,
        out_specs=pl.BlockSpec((1, S, LANES), seq),
        out_shape=jax.ShapeDtypeStruct((B, S, npairs * LANES), BF16),
        scratch_shapes=scratch,
        compiler_params=_params("parallel", "parallel"),
        name="fox_attention" if fox else "mla_attention",
    )(*args)
    return out.reshape(B * S, npairs * LANES)


def _dsa_kernel(qi_ref, qb_ref, wit_ref, ki_ref, kb_ref, va_ref, o_ref,
                qim_ref, qbm_ref, sc_ref, sb_ref, *, nk, i0, ksel):
    i = i0 + pl.program_id(1)
    chunks = [slice(k0, min(k0 + KC_DSA, nk)) for k0 in range(0, nk, KC_DSA)]
    lane = _lane_iota()
    lo = lane < D_DSA
    hi = lane >= D_DSA
    q_chunk = (i * Q_DSA + lane) >> CHUNK_SHIFT
    select = nk > ksel

    def key_idx(rows):
        return rows.start + lax.broadcasted_iota(jnp.int32, (rows.stop - rows.start, 1), 0)

    def admissible(rows):
        return (key_idx(rows) >> CHUNK_SHIFT) <= q_chunk

    def fold_max(s):
        return jnp.max(s.reshape(s.shape[0] // SUBLANES, SUBLANES, s.shape[1]), axis=0)

    for h in range(H_DSA):
        blk = qb_ref[:, (h // 2) * LANES:(h // 2 + 1) * LANES] * jnp.asarray(D_DSA ** -0.5, BF16)
        qbm_ref[h * Q_DSA:(h + 1) * Q_DSA, :] = jnp.where(lo if h % 2 == 0 else hi, blk, jnp.zeros_like(blk))

    if select:
        for h in range(H_IDX):
            blk = qi_ref[:, (h // 2) * LANES:(h // 2 + 1) * LANES]
            qim_ref[h * Q_DSA:(h + 1) * Q_DSA, :] = jnp.where(lo if h % 2 == 0 else hi, blk, jnp.zeros_like(blk))
        w_flat = jnp.concatenate([wit_ref[h:h + 1, :] for h in range(H_IDX)], axis=1)
        for rows in chunks:
            r = lax.dot_general(ki_ref[0, rows, :], qim_ref[...], _NT, preferred_element_type=F32)
            r = jnp.maximum(r, 0.0) * w_flat
            acc = r[:, :LANES]
            for h in range(1, H_IDX):
                acc = acc + r[:, h * LANES:(h + 1) * LANES]
            sc_ref[rows, :] = jnp.where(admissible(rows), acc, NEG_INF)

        def count(mask):
            part = jnp.sum(mask.astype(F32).reshape(nk // 32, 32, LANES), axis=0)
            return jnp.sum(part, axis=0, keepdims=True)

        def as_float(key):
            return lax.bitcast_convert_type(jnp.where(key >= 0, key, key ^ 0x7FFFFFFF), F32)

        def bit_body(r, ans):
            cand = ans + lax.shift_left(jnp.int32(1), 31 - r)
            cnt = count(sc_ref[...] >= as_float(cand))
            return jnp.where(cnt >= ksel, cand, ans)
        ans = lax.fori_loop(0, 32, bit_body, jnp.full((1, LANES), jnp.iinfo(jnp.int32).min, jnp.int32))
        t = as_float(ans)
        s_all = sc_ref[...]
        need = ksel - count(s_all > t)
        n_ge = count(s_all >= t)
        k_all = lax.broadcasted_iota(jnp.int32, (nk, 1), 0)

        def tie_bound():
            def bit_body2(r, aj):
                cand = aj + lax.shift_left(jnp.int32(1), 11 - r)
                g = count((sc_ref[...] == t) & (k_all < cand))
                return jnp.where(g <= need, cand, aj)
            return lax.fori_loop(0, 12, bit_body2, jnp.zeros((1, LANES), jnp.int32))

        jmax = lax.cond(jnp.max(n_ge) > ksel, tie_bound,
                        lambda: jnp.full((1, LANES), 4095, jnp.int32))
        for rows in chunks:
            s = sc_ref[rows, :]
            sel = (s > t) | ((s == t) & (key_idx(rows) < jmax))
            sc_ref[rows, :] = jnp.where(sel & admissible(rows), 0.0, NEG_INF)
    else:
        for rows in chunks:
            sc_ref[rows, :] = jnp.where(admissible(rows), 0.0, NEG_INF)

    m8 = jnp.full((SUBLANES, H_DSA * LANES), -jnp.inf, F32)
    for rows in chunks:
        s = lax.dot_general(kb_ref[0, rows, :], qbm_ref[...], _NT, preferred_element_type=F32)
        s = s + jnp.concatenate([sc_ref[rows, :]] * H_DSA, axis=1)
        sb_ref[rows, :] = s
        m8 = jnp.maximum(m8, fold_max(s))
    m = jnp.max(m8, axis=0, keepdims=True)
    acc = jnp.zeros((H_DSA * Q_DSA, LANES), F32)
    for rows in chunks:
        p = jnp.exp(sb_ref[rows, :] - m).astype(BF16)
        acc = acc + lax.dot_general(p, va_ref[0, rows, :], _TN, preferred_element_type=F32)

    for hp in range(H_DSA // 2):
        a = acc[(2 * hp) * Q_DSA:(2 * hp + 1) * Q_DSA]
        b = acc[(2 * hp + 1) * Q_DSA:(2 * hp + 2) * Q_DSA]
        ra = a / pltpu.roll(a, D_DSA, 1)
        rb = b / pltpu.roll(b, D_DSA, 1)
        o_ref[:, hp * LANES:(hp + 1) * LANES] = jnp.where(lo, ra, pltpu.roll(rb, D_DSA, 1)).astype(BF16)


def _dsa(qi, qb, wit, ki2, kb2, va, *, B, S, ksel):
    nb = S // Q_DSA
    per = KC_DSA // Q_DSA
    seq3 = lambda a: a.reshape(B, S, LANES)
    ki3, kb3, va3 = seq3(ki2), seq3(kb2), seq3(va)
    outs = []
    for g in range(nb // per):
        nk, i0 = (g + 1) * per * Q_DSA, g * per
        qrow = lambda n, i0=i0: pl.BlockSpec((Q_DSA, n), lambda b, j: (b * nb + i0 + j, 0))
        seq = pl.BlockSpec((1, nk, LANES), lambda b, j: (b, 0, 0))
        o = pl.pallas_call(
            functools.partial(_dsa_kernel, nk=nk, i0=i0, ksel=ksel),
            grid=(B, per),
            in_specs=[qrow(H_IDX * D_IDX), qrow(H_DSA * D_DSA),
                      pl.BlockSpec((SUBLANES, Q_DSA), lambda b, j, i0=i0: (0, b * nb + i0 + j)),
                      seq, seq, seq],
            out_specs=pl.BlockSpec((Q_DSA, H_DSA * D_DSA), lambda b, j: (b * per + j, 0)),
            out_shape=jax.ShapeDtypeStruct((B * per * Q_DSA, H_DSA * D_DSA), BF16),
            scratch_shapes=[pltpu.VMEM((H_IDX * Q_DSA, LANES), BF16), pltpu.VMEM((H_DSA * Q_DSA, LANES), BF16),
                            pltpu.VMEM((nk, LANES), F32), pltpu.VMEM((nk, H_DSA * LANES), F32)],
            compiler_params=_params("parallel", "arbitrary"),
            name="dsa_attention_%d" % nk,
        )(qi, qb, wit, ki3, kb3, va3)
        outs.append(o.reshape(B, per * Q_DSA, H_DSA * D_DSA))
    return jnp.concatenate(outs, axis=1).reshape(B * S, H_DSA * D_DSA)


_R_SUB = N_GROUPS
_PAIRS = [(a, b) for a in range(EXPERTS_PER_GROUP) for b in range(a + 1, EXPERTS_PER_GROUP)]
N_CLASSES = N_GROUPS * len(_PAIRS)


def _out_ln_kernel(*refs, n_in):
    o_refs, w_refs = refs[:n_in], refs[n_in:2 * n_in]
    x_ref, g_ref, b_ref, wrt_ref, brt_ref, tri_ref, out_ref, route_ref, cnt_ref, run_ref = refs[2 * n_in:]
    y = jnp.dot(o_refs[0][...], w_refs[0][...], preferred_element_type=F32)
    for o_r, w_r in zip(o_refs[1:], w_refs[1:]):
        y = y + jnp.dot(o_r[...], w_r[...], preferred_element_type=F32)
    x_new = _layer_norm(DN_ALPHA * x_ref[...] + y, g_ref[...], b_ref[...])
    out_ref[...] = x_new
    _route_rows(x_new, wrt_ref, brt_ref, tri_ref, route_ref, cnt_ref, run_ref)


def _out_ln(os_, ws, x2, g, b, wr, br):
    T = x2.shape[0]
    tm = min(TM_PROJ, T)
    row = lambda c: pl.BlockSpec((tm, c), lambda i: (i, 0))
    tri = jnp.triu(jnp.ones((tm, tm), BF16), 1)
    wrt, brt = wr.T, br.T
    return pl.pallas_call(
        functools.partial(_out_ln_kernel, n_in=len(os_)),
        grid=(T // tm,),
        in_specs=[row(o.shape[1]) for o in os_] + [_full_spec(w.shape) for w in ws]
                 + [row(D_MODEL), _full_spec(g.shape), _full_spec(b.shape),
                    _full_spec(wrt.shape), _full_spec(brt.shape), _full_spec(tri.shape)],
        out_specs=[row(D_MODEL), pl.BlockSpec((1, SUBLANES, tm), lambda i: (i, 0, 0)), _full_spec((LANES, LANES))],
        out_shape=[jax.ShapeDtypeStruct((T, D_MODEL), F32), jax.ShapeDtypeStruct((T // tm, SUBLANES, tm), F32),
                   jax.ShapeDtypeStruct((LANES, LANES), F32)],
        scratch_shapes=[pltpu.VMEM((LANES, 1), F32)],
        compiler_params=_params("arbitrary"),
        name="out_proj_ln",
    )(*os_, *ws, x2, g, b, wrt, brt, tri)


def _route_rows(x, wrt_ref, brt_ref, tri_ref, route_ref, cnt_ref, run_ref):
    @pl.when(pl.program_id(0) == 0)
    def _():
        run_ref[...] = jnp.zeros(run_ref.shape, F32)

    tm = x.shape[0]
    lg = lax.dot_general(wrt_ref[...], x.astype(BF16), _NT, preferred_element_type=F32) + brt_ref[...]
    r = lax.broadcasted_iota(jnp.int32, (LANES, 1), 0)
    rf = r.astype(F32)
    far = float(LANES)
    grp = jnp.where(r < N_GROUPS, lg, -jnp.inf)
    mg = jnp.max(grp, axis=0, keepdims=True)
    g_star = jnp.min(jnp.where(grp == mg, rf, far), axis=0, keepdims=True)
    row_grp = ((r - _R_SUB) >> 2).astype(F32)
    in_grp = (row_grp == g_star) & (r >= _R_SUB) & (r < _R_SUB + N_EXPERTS)
    l1 = jnp.where(in_grp, lg, -jnp.inf)
    v1 = jnp.max(l1, axis=0, keepdims=True)
    j1 = jnp.min(jnp.where(l1 == v1, rf, far), axis=0, keepdims=True)
    l2 = jnp.where(in_grp & (rf != j1), lg, -jnp.inf)
    v2 = jnp.max(l2, axis=0, keepdims=True)
    j2 = jnp.min(jnp.where(l2 == v2, rf, far), axis=0, keepdims=True)
    base = _R_SUB + EXPERTS_PER_GROUP * g_star
    ja = jnp.minimum(j1, j2) - base
    jb = jnp.maximum(j1, j2) - base
    cls = g_star * len(_PAIRS) + ja * (7.0 - ja) * 0.5 + (jb - ja - 1.0)
    onehot = rf == cls
    ones = jnp.where(onehot, 1.0, 0.0)
    before = jnp.dot(ones.astype(BF16), tri_ref[...], preferred_element_type=F32)
    rank = jnp.sum(jnp.where(onehot, before + run_ref[...], 0.0), axis=0, keepdims=True)
    row8 = lax.broadcasted_iota(jnp.int32, (SUBLANES, 1), 0)
    route_ref[0] = jnp.where(row8 == 0, cls, jnp.where(row8 == 1, rank, 0.0))
    run_ref[...] += jnp.sum(ones, axis=1, keepdims=True)
    cnt_ref[...] = jnp.broadcast_to(run_ref[...], cnt_ref.shape)


def _row_copy(src_ref, dst_ref, src_row, dst_row, sem):
    return pltpu.make_async_copy(src_ref.at[pl.ds(src_row, 1), :], dst_ref.at[pl.ds(dst_row, 1), :], sem)


def _dispatch_kernel(pos_ref, x_ref, init_ref, xs_ref, sem):
    del init_ref
    tm = x_ref.shape[0]

    def start(g, carry):
        base = pl.multiple_of(g * SUBLANES, SUBLANES)
        for j in range(SUBLANES):
            _row_copy(x_ref, xs_ref, base + j, pos_ref[0, 0, base + j], sem).start()
        return carry
    lax.fori_loop(0, tm // SUBLANES, start, 0)

    def wait(t, carry):
        _row_copy(x_ref, xs_ref, 0, 0, sem).wait()
        return carry
    lax.fori_loop(0, tm, wait, 0, unroll=8)


def _collect_kernel(pos_ref, ys_ref, out_ref, sem):
    tm = out_ref.shape[0]

    def start(g, carry):
        base = pl.multiple_of(g * SUBLANES, SUBLANES)
        for j in range(SUBLANES):
            _row_copy(ys_ref, out_ref, pos_ref[0, 0, base + j], base + j, sem).start()
        return carry
    lax.fori_loop(0, tm // SUBLANES, start, 0)

    def wait(t, carry):
        _row_copy(ys_ref, out_ref, 0, 0, sem).wait()
        return carry
    lax.fori_loop(0, tm, wait, 0, unroll=8)


def _dispatch(x2, pos3, n_rows):
    T = x2.shape[0]
    tm = pos3.shape[-1]
    return pl.pallas_call(
        _dispatch_kernel,
        grid=(T // tm,),
        in_specs=[pl.BlockSpec((1, 1, tm), lambda i: (i, 0, 0), memory_space=pltpu.SMEM),
                  pl.BlockSpec((tm, D_MODEL), lambda i: (i, 0)),
                  pl.BlockSpec(memory_space=pl.ANY)],
        out_specs=pl.BlockSpec(memory_space=pl.ANY),
        out_shape=jax.ShapeDtypeStruct((n_rows, D_MODEL), F32),
        scratch_shapes=[pltpu.SemaphoreType.DMA(())],
        input_output_aliases={2: 0},
        compiler_params=_params("arbitrary"),
        name="moe_dispatch",
    )(pos3, x2, jnp.zeros((n_rows, D_MODEL), F32))


def _collect(ys, pos3):
    T = pos3.shape[0] * pos3.shape[-1]
    tm = pos3.shape[-1]
    return pl.pallas_call(
        _collect_kernel,
        grid=(T // tm,),
        in_specs=[pl.BlockSpec((1, 1, tm), lambda i: (i, 0, 0), memory_space=pltpu.SMEM),
                  pl.BlockSpec(memory_space=pl.ANY)],
        out_specs=pl.BlockSpec((tm, D_MODEL), lambda i: (i, 0)),
        out_shape=jax.ShapeDtypeStruct((T, D_MODEL), F32),
        scratch_shapes=[pltpu.SemaphoreType.DMA(())],
        compiler_params=_params("arbitrary"),
        name="moe_collect",
    )(pos3, ys)


def _expert_kernel(tg_ref, ta_ref, tb_ref, tv_ref, x_ref, wr_ref, br_ref,
                   wga_ref, wua_ref, wda_ref, wgb_ref, wub_ref, wdb_ref, g_ref, b_ref, out_ref):
    r = pl.program_id(0)

    @pl.when(tv_ref[r] == 0)
    def _():
        out_ref[...] = jnp.zeros(out_ref.shape, F32)

    @pl.when(tv_ref[r] != 0)
    def _():
        x = x_ref[...]
        xb = x.astype(BF16)
        lg = jnp.dot(xb, wr_ref[...], preferred_element_type=F32) + br_ref[...]
        lane = _lane_iota()
        is_grp = lane < N_GROUPS
        mg = jnp.max(jnp.where(is_grp, lg, -jnp.inf), axis=-1, keepdims=True)
        den = jnp.sum(jnp.where(is_grp, jnp.exp(lg - mg), 0.0), axis=-1, keepdims=True)
        pick = lambda l: jnp.sum(jnp.where(lane == l, lg, 0.0), axis=-1, keepdims=True)
        p_top = jnp.exp(pick(tg_ref[r]) - mg) / den
        va, vb = pick(_R_SUB + ta_ref[r]), pick(_R_SUB + tb_ref[r])
        e = jnp.exp(jnp.minimum(va, vb) - jnp.maximum(va, vb))
        w_top, w_oth = 1.0 / (1.0 + e), e / (1.0 + e)
        a_top = va >= vb
        gates = (jnp.where(a_top, w_top, w_oth) * p_top, jnp.where(a_top, w_oth, w_top) * p_top)
        y = jnp.zeros(x.shape, F32)
        for gate, wg, wu, wd in ((gates[0], wga_ref, wua_ref, wda_ref), (gates[1], wgb_ref, wub_ref, wdb_ref)):
            a = jnp.dot(xb, wg[0], preferred_element_type=F32)
            u = jnp.dot(xb, wu[0], preferred_element_type=F32)
            h = (a * (1.0 / (1.0 + jnp.exp(-a))) * u).astype(BF16)
            y = y + gate * jnp.dot(h, wd[0], preferred_element_type=F32)
        out_ref[...] = _layer_norm(DN_ALPHA * x + y, g_ref[...], b_ref[...])


def _experts(xs, tile_g, tile_a, tile_b, tile_v, wr, br, wg, wu, wd, g, b):
    n_tiles = tile_g.shape[0]
    te = xs.shape[0] // n_tiles
    row = pl.BlockSpec((te, D_MODEL), lambda r, *_: (r, 0))
    full = lambda shape: pl.BlockSpec(shape, lambda r, *_: (0,) * len(shape))
    up_a = pl.BlockSpec((1, D_MODEL, D_EXPERT), lambda r, tg, ta, tb, tv: (ta[r], 0, 0))
    up_b = pl.BlockSpec((1, D_MODEL, D_EXPERT), lambda r, tg, ta, tb, tv: (tb[r], 0, 0))
    dn_a = pl.BlockSpec((1, D_EXPERT, D_MODEL), lambda r, tg, ta, tb, tv: (ta[r], 0, 0))
    dn_b = pl.BlockSpec((1, D_EXPERT, D_MODEL), lambda r, tg, ta, tb, tv: (tb[r], 0, 0))
    return pl.pallas_call(
        _expert_kernel,
        grid_spec=pltpu.PrefetchScalarGridSpec(
            num_scalar_prefetch=4, grid=(n_tiles,),
            in_specs=[row, full(wr.shape), full(br.shape), up_a, up_a, dn_a, up_b, up_b, dn_b,
                      full(g.shape), full(b.shape)],
            out_specs=row),
        out_shape=jax.ShapeDtypeStruct(xs.shape, F32),
        compiler_params=_params("arbitrary"),
        name="moe_experts",
    )(tile_g, tile_a, tile_b, tile_v, xs, wr, br, wg, wu, wd, wg, wu, wd, g, b)


def _slot_kernel(route_ref, start_ref, pos_ref):
    rf = lax.broadcasted_iota(jnp.int32, (LANES, 1), 0).astype(F32)
    for j in range(route_ref.shape[0]):
        cls, rank = route_ref[j, 0:1, :], route_ref[j, 1:2, :]
        start = jnp.sum(jnp.where(rf == cls, start_ref[...], 0.0), axis=0, keepdims=True)
        pos_ref[j] = (start + rank).astype(jnp.int32)


def _slots(route, starts):
    n, _, tm = route.shape
    nb = min(n, 16)
    slots = pl.pallas_call(
        _slot_kernel,
        grid=(n // nb,),
        in_specs=[pl.BlockSpec((nb, SUBLANES, tm), lambda i: (i, 0, 0)), _full_spec(starts.shape)],
        out_specs=pl.BlockSpec((nb, 1, tm), lambda i: (i, 0, 0)),
        out_shape=jax.ShapeDtypeStruct((n, 1, tm), jnp.int32),
        compiler_params=_params("parallel"),
        name="moe_slots",
    )(route, starts)
    move = min(TM_MOVE, n * tm)
    return slots.reshape(n * tm // move, 1, move)


def _moe_ln(x2, route, counts, wr, br, wg, wu, wd, g, b):
    T = x2.shape[0]
    te = TE_MOE
    n_tiles = (T + N_CLASSES * (te - 1) + te - 1) // te
    cnt = counts[:N_CLASSES, 0].astype(jnp.int32)
    padded = (cnt + te - 1) // te * te
    ends = jnp.cumsum(padded)
    starts = jnp.pad((ends - padded).astype(F32), (0, LANES - N_CLASSES))[:, None]
    pos3 = _slots(route, starts)
    tile_start = jnp.arange(n_tiles, dtype=jnp.int32) * te
    tile_cls = jnp.minimum(jnp.sum(tile_start[:, None] >= ends[None, :], axis=1), N_CLASSES - 1).astype(jnp.int32)
    tile_v = (tile_start < ends[-1]).astype(jnp.int32)
    pair = jnp.asarray(_PAIRS, jnp.int32)[tile_cls % len(_PAIRS)]
    tile_g = tile_cls // len(_PAIRS)
    tile_a = tile_g * EXPERTS_PER_GROUP + pair[:, 0]
    tile_b = tile_g * EXPERTS_PER_GROUP + pair[:, 1]
    xs = _dispatch(x2, pos3, n_tiles * te)
    ys = _experts(xs, tile_g, tile_a, tile_b, tile_v, wr, br, wg, wu, wd, g, b)
    return _collect(ys, pos3)


def _router_weights(w_grp, b_grp, w_sub, b_sub):
    pad = LANES - N_GROUPS - N_EXPERTS
    wr = jnp.concatenate([w_grp, w_sub, jnp.zeros((D_MODEL, pad), w_grp.dtype)], axis=1).astype(BF16)
    br = jnp.concatenate([b_grp, b_sub, jnp.zeros((pad,), b_grp.dtype)])[None].astype(F32)
    return wr, br


def kernel(x, positions, ev_w_in, ev_g_q, ev_g_kv, ev_w_uq, ev_w_ukv, ev_w_o, od_w_in, od_b_f, od_w_o,
           moe_w_grp, moe_b_grp, moe_w_sub, moe_b_sub, moe_w_gate, moe_w_up, moe_w_down,
           ln1_g, ln1_b, ln2_g, ln2_b):
    B, S, D = x.shape
    T = B * S
    ksel = min(TOPK_MAX, S // 4)
    x2 = x.reshape(T, D)
    tabs = _rope_tables(positions.reshape(T, 1).astype(F32))
    row = lambda v: v[None].astype(F32)
    for layer in range(DEPTH):
        j = layer // 2
        wr, br = _router_weights(moe_w_grp[layer], moe_b_grp[layer], moe_w_sub[layer], moe_b_sub[layer])
        ln1 = (row(ln1_g[layer]), row(ln1_b[layer]), wr, br)
        if layer % 2 == 0:
            w1, wuq, wukv = _even_weights(ev_w_in[j], ev_w_uq[j], ev_w_ukv[j])
            qa, kv, kr, qb, kb2, va, qi, ki2, wi = _even_proj(
                x2, w1, row(ev_g_q[j]), row(ev_g_kv[j]), wuq, wukv, tabs)
            o_a = _flash((qa, kv, kr), fox=False, B=B, S=S, scale=(D_NOPE + D_ROPE) ** -0.5)
            wit = jnp.pad(wi[:, :H_IDX].T, ((0, SUBLANES - H_IDX), (0, 0)))
            o_b = _dsa(qi, qb, wit, ki2, kb2, va, B=B, S=S, ksel=ksel)
            wo = ev_w_o[j].astype(BF16)
            n_a = H_MLA * D_V_MLA
            x2, route, counts = _out_ln((o_a, o_b), (wo[:n_a], wo[n_a:]), x2, *ln1)
        else:
            n = H_FOX * D_FOX
            w = jnp.pad(od_w_in[j], ((0, 0), (0, LANES - H_FOX))).astype(BF16)
            bf = jnp.pad(od_b_f[j], (0, LANES - H_FOX))[None].astype(F32)
            q, k, v, lf = _odd_proj(x2, w, bf)
            c = _cumsum_seq(lf, S)
            c_rows = c.reshape(B, S, LANES)[:, :, :H_FOX].swapaxes(1, 2)
            o = _flash((q, k, v, c, c_rows), fox=True, B=B, S=S, scale=D_FOX ** -0.5)
            x2, route, counts = _out_ln((o,), (od_w_o[j].astype(BF16),), x2, *ln1)
        x2 = _moe_ln(x2, route, counts, wr, br, moe_w_gate[layer].astype(BF16), moe_w_up[layer].astype(BF16),
                     moe_w_down[layer].astype(BF16), row(ln2_g[layer]), row(ln2_b[layer]))
    return x2.reshape(B, S, D)
```

```python
import functools

import jax
import jax.numpy as jnp
from jax import lax
from jax.experimental import pallas as pl
from jax.experimental.pallas import tpu as pltpu

D_MODEL = 1024
DEPTH = 4
CHUNK = 64
CHUNK_SHIFT = 6
ROPE_THETA = 10000.0
NEG_INF = -1e30
LN_EPS = 1e-5
RMS_EPS = 1e-6
H_MLA, D_NOPE, D_ROPE, D_V_MLA, Q_LORA, KV_LORA = 8, 64, 32, 64, 384, 256
H_DSA, D_DSA, H_IDX, D_IDX, TOPK_MAX = 8, 64, 4, 64, 256
H_FOX, D_FOX = 16, 64
N_GROUPS, EXPERTS_PER_GROUP, D_EXPERT = 4, 4, 512
N_EXPERTS = N_GROUPS * EXPERTS_PER_GROUP
DN_ALPHA = (2 * DEPTH) ** 0.25
EV_SPLITS = (Q_LORA, KV_LORA, D_ROPE, H_DSA * D_DSA, D_DSA, D_DSA, H_IDX * D_IDX, D_IDX, H_IDX)

LANES = 128
SUBLANES = 8
VMEM_LIMIT_BYTES = 56 * 1024 * 1024

TM_PROJ = 1024
TM_MOVE = 2048
TE_MOE = 256
T_ATT = 512
VT_ROWS = 80
Q_DSA = 128
KC_DSA = 256

F32 = jnp.float32
BF16 = jnp.bfloat16
_NT = (((1,), (1,)), ((), ()))
_TN = (((0,), (0,)), ((), ()))


def _params(*sem):
    return pltpu.CompilerParams(dimension_semantics=sem, vmem_limit_bytes=VMEM_LIMIT_BYTES)


def _lane_iota(shape=(1, LANES)):
    return lax.broadcasted_iota(jnp.int32, shape, len(shape) - 1)


def _full_spec(shape):
    return pl.BlockSpec(shape, lambda *_: (0,) * len(shape))


def _rope_table_kernel(pos_ref, inv32_ref, sg32_ref, inv64_ref, sg64_ref,
                       c32_ref, s32_ref, c64_ref, s64_ref):
    pos = pos_ref[...]
    a32 = pos * inv32_ref[...]
    a64 = pos * inv64_ref[...]
    c32_ref[...] = jnp.cos(a32)
    s32_ref[...] = jnp.sin(a32) * sg32_ref[...]
    c64_ref[...] = jnp.cos(a64)
    s64_ref[...] = jnp.sin(a64) * sg64_ref[...]


def _rope_tables(posf):
    T = posf.shape[0]
    tm = min(T, 2048)
    lane = jnp.arange(LANES)
    inv16 = jnp.power(ROPE_THETA, -jnp.arange(0, D_ROPE, 2, dtype=F32) / D_ROPE)
    inv32 = jnp.power(ROPE_THETA, -jnp.arange(0, D_DSA, 2, dtype=F32) / D_DSA)
    in32 = (lane >= D_NOPE) & (lane < D_NOPE + D_ROPE)
    t32 = jnp.where(in32, inv16[(lane - D_NOPE) % (D_ROPE // 2)], 0.0).astype(F32)[None]
    g32 = jnp.where(in32, jnp.where(lane < D_NOPE + D_ROPE // 2, -1.0, 1.0), 0.0).astype(F32)[None]
    t64 = inv32[lane % (D_DSA // 2)].astype(F32)[None]
    g64 = jnp.where((lane % D_DSA) < D_DSA // 2, -1.0, 1.0).astype(F32)[None]
    row = pl.BlockSpec((tm, LANES), lambda i: (i, 0))
    tab = jax.ShapeDtypeStruct((T, LANES), F32)
    return pl.pallas_call(
        _rope_table_kernel,
        grid=(T // tm,),
        in_specs=[pl.BlockSpec((tm, 1), lambda i: (i, 0))] + [_full_spec((1, LANES))] * 4,
        out_specs=[row] * 4,
        out_shape=[tab] * 4,
        compiler_params=_params("parallel"),
        name="rope_tables",
    )(posf, t32, g32, t64, g64)


def _rope_block(y, c, s, half, second_half):
    partner = jnp.where(second_half, pltpu.roll(y, half, 1), pltpu.roll(y, LANES - half, 1))
    return y * c + partner * s


def _rms(x, g):
    return x * lax.rsqrt(jnp.mean(x * x, axis=-1, keepdims=True) + RMS_EPS) * g


def _layer_norm(z, g, b):
    mu = jnp.mean(z, axis=-1, keepdims=True)
    d = z - mu
    var = jnp.mean(d * d, axis=-1, keepdims=True)
    return d * lax.rsqrt(var + LN_EPS) * g + b


_E_CQ, _E_CKV, _E_KR, _E_QB, _E_KB, _E_VB, _E_QI, _E_KI, _E_WI, _E_END = (
    0, 384, 640, 768, 1280, 1408, 1536, 1792, 1920, 2048)


def _even_proj_kernel(x_ref, w1_ref, gq_ref, gkv_ref, wuq_ref, wukv_ref,
                      c32_ref, s32_ref, c64_ref, s64_ref,
                      qa_ref, kv_ref, kr_ref, qb_ref, kb_ref, va_ref, qi_ref, ki_ref, wi_ref):
    xb = x_ref[...].astype(BF16)
    lane = _lane_iota()
    sec32 = (lane >= D_NOPE + D_ROPE // 2) & (lane < D_NOPE + D_ROPE)
    sec64 = (lane % D_DSA) >= D_DSA // 2
    c32, s32, c64, s64 = c32_ref[...], s32_ref[...], c64_ref[...], s64_ref[...]

    def seg(a, b):
        return jnp.dot(xb, w1_ref[:, a:b], preferred_element_type=F32)

    def rope64_store(h, out_ref):
        for j in range(h.shape[1] // LANES):
            blk = h[:, j * LANES:(j + 1) * LANES]
            out_ref[:, j * LANES:(j + 1) * LANES] = _rope_block(
                blk, c64, s64, D_DSA // 2, sec64).astype(out_ref.dtype)

    cq = _rms(seg(_E_CQ, _E_CKV), gq_ref[...]).astype(BF16)
    q = jnp.dot(cq, wuq_ref[...], preferred_element_type=F32)
    for h in range(H_MLA):
        blk = q[:, h * LANES:(h + 1) * LANES]
        qa_ref[:, h * LANES:(h + 1) * LANES] = _rope_block(
            blk, c32, s32, D_ROPE // 2, sec32).astype(BF16)
    ckv = _rms(seg(_E_CKV, _E_KR), gkv_ref[...]).astype(BF16)
    kv_ref[...] = jnp.dot(ckv, wukv_ref[...], preferred_element_type=F32).astype(BF16)
    kr_ref[...] = _rope_block(seg(_E_KR, _E_QB), c32, s32, D_ROPE // 2, sec32).astype(BF16)
    rope64_store(seg(_E_QB, _E_KB), qb_ref)
    rope64_store(seg(_E_KB, _E_VB), kb_ref)
    va_ref[...] = jnp.where(lane < D_DSA, seg(_E_VB, _E_QI), 1.0).astype(BF16)
    rope64_store(seg(_E_QI, _E_KI), qi_ref)
    rope64_store(seg(_E_KI, _E_WI), ki_ref)
    wi_ref[...] = seg(_E_WI, _E_END)


def _even_weights(w_in, w_uq, w_ukv):
    o = [0]
    for s_ in EV_SPLITS:
        o.append(o[-1] + s_)
    cq, ckv, kr, qb, kb, vb, qi, ki, wi = (w_in[:, o[i]:o[i + 1]] for i in range(9))
    z = lambda n: jnp.zeros((D_MODEL, n), w_in.dtype)
    w1 = jnp.concatenate([
        cq, ckv,
        z(D_NOPE), kr, z(LANES - D_NOPE - D_ROPE),
        qb,
        kb, kb,
        vb, z(LANES - D_DSA),
        qi,
        ki, ki,
        wi, z(LANES - H_IDX)], axis=1).astype(BF16)
    wq = w_uq.reshape(Q_LORA, H_MLA, D_NOPE + D_ROPE)
    wq = jnp.pad(wq, ((0, 0), (0, 0), (0, LANES - D_NOPE - D_ROPE))).reshape(Q_LORA, H_MLA * LANES)
    return w1, wq.astype(BF16), w_ukv.astype(BF16)


def _even_proj(x2, w1, gq, gkv, wuq, wukv, tabs):
    T = x2.shape[0]
    tm = min(TM_PROJ, T)
    row = lambda n: pl.BlockSpec((tm, n), lambda i: (i, 0))
    outs = [(H_MLA * LANES, BF16), (H_MLA * LANES, BF16), (LANES, BF16), (H_DSA * D_DSA, BF16),
            (LANES, BF16), (LANES, BF16), (H_IDX * D_IDX, BF16), (LANES, BF16), (LANES, F32)]
    return pl.pallas_call(
        _even_proj_kernel,
        grid=(T // tm,),
        in_specs=[row(D_MODEL), _full_spec(w1.shape), _full_spec(gq.shape), _full_spec(gkv.shape),
                  _full_spec(wuq.shape), _full_spec(wukv.shape)] + [row(LANES)] * 4,
        out_specs=[row(n) for n, _ in outs],
        out_shape=[jax.ShapeDtypeStruct((T, n), d) for n, d in outs],
        compiler_params=_params("parallel"),
        name="even_proj",
    )(x2, w1, gq, gkv, wuq, wukv, *tabs)


def _odd_proj_kernel(x_ref, w_ref, bf_ref, q_ref, k_ref, v_ref, lf_ref):
    xb = x_ref[...].astype(BF16)
    n = H_FOX * D_FOX
    for j, out in enumerate((q_ref, k_ref, v_ref)):
        out[...] = jnp.dot(xb, w_ref[:, j * n:(j + 1) * n], preferred_element_type=F32).astype(BF16)
    z = jnp.dot(xb, w_ref[:, 3 * n:], preferred_element_type=F32) + bf_ref[...]
    lf_ref[...] = jnp.minimum(z, 0.0) - jnp.log1p(jnp.exp(-jnp.abs(z)))


def _odd_proj(x2, w, bf):
    T = x2.shape[0]
    tm = min(TM_PROJ, T)
    n = H_FOX * D_FOX
    row = lambda c: pl.BlockSpec((tm, c), lambda i: (i, 0))
    return pl.pallas_call(
        _odd_proj_kernel,
        grid=(T // tm,),
        in_specs=[row(D_MODEL), _full_spec(w.shape), _full_spec(bf.shape)],
        out_specs=[row(n), row(n), row(n), row(LANES)],
        out_shape=[jax.ShapeDtypeStruct((T, n), BF16)] * 3 + [jax.ShapeDtypeStruct((T, LANES), F32)],
        compiler_params=_params("parallel"),
        name="odd_proj",
    )(x2, w, bf)


def _cumsum_kernel(lf_ref, c_ref):
    a = lf_ref[...]
    rows = lax.broadcasted_iota(jnp.int32, a.shape, 0)
    k = 1
    while k < a.shape[0]:
        a = a + jnp.where(rows >= k, pltpu.roll(a, k, 0), 0.0)
        k *= 2
    c_ref[...] = a


def _cumsum_seq(lf, S):
    T = lf.shape[0]
    blk = pl.BlockSpec((S, LANES), lambda b: (b, 0))
    return pl.pallas_call(
        _cumsum_kernel, grid=(T // S,), in_specs=[blk], out_specs=blk,
        out_shape=jax.ShapeDtypeStruct((T, LANES), F32),
        compiler_params=_params("parallel"), name="forget_cumsum",
    )(lf)


def _flash_kernel(*refs, fox, scale, t, n_tiles, S):
    if fox:
        q_ref, k_ref, v_ref, cc_ref, cr_ref, o_ref, kx_ref, vt_ref, sb0_ref, sb1_ref = refs
    else:
        q_ref, kv_ref, kr_ref, o_ref, kx_ref, vt_ref, sb0_ref, sb1_ref = refs
    sb_refs = (sb0_ref, sb1_ref)
    hp = pl.program_id(1)
    lane = _lane_iota()
    lo = lane < D_FOX
    hi = lane >= D_FOX
    exp2_scale = scale * 1.4426950408889634

    ones = jnp.ones((VT_ROWS - D_FOX, S), BF16)
    if fox:
        vt = jnp.transpose(v_ref[0].astype(F32)).astype(BF16)
    for hh in range(2):
        if fox:
            col = jnp.sum(jnp.where(lane == 2 * hp + hh, cc_ref[0], 0.0), axis=-1, keepdims=True)
            kx_ref[hh] = jnp.broadcast_to(col, (S, LANES))
            vt_ref[hh, :D_FOX, :] = vt[hh * D_FOX:(hh + 1) * D_FOX]
        else:
            kvh = kv_ref[0, :, hh * LANES:(hh + 1) * LANES]
            kx_ref[hh] = jnp.where(lo, kvh, kr_ref[0])
            vt_ref[hh, :D_V_MLA, :] = jnp.transpose(kvh.astype(F32))[D_NOPE:].astype(BF16)
        vt_ref[hh, D_FOX:, :] = ones

    def body(n):
        qrows = slice(n * t, (n + 1) * t)
        half = t // 2
        blocks = [(c * t, t, 0, False) for c in range(n)] + [(n * t, half, 0, True), (n * t + half, half, half, True)]

        def on_lanes(full, q0, part, op):
            if q0 == 0:
                return op(full, part)
            return jnp.concatenate([full[:, :q0], op(full[:, q0:], part)], axis=1)

        outs = []
        for hh in range(2):
            if fox:
                q2 = q_ref[0, qrows, :] * jnp.asarray(scale, BF16)
                qh = jnp.where(lo if hh == 0 else hi, q2, jnp.zeros_like(q2))
                cq = cr_ref[0, pl.ds(2 * hp + hh, 1), qrows]
            else:
                qh = q_ref[0, qrows, hh * LANES:(hh + 1) * LANES]
            m8 = jnp.full((SUBLANES, t), -jnp.inf, F32)
            for k0, nk, q0, masked in blocks:
                nq = t - q0
                kh = k_ref[0, k0:k0 + nk, :] if fox else kx_ref[hh, k0:k0 + nk, :]
                s = lax.dot_general(kh, qh[q0:], _NT, preferred_element_type=F32)
                if fox:
                    s = s + cq[:, q0:] - jnp.concatenate([kx_ref[hh, k0:k0 + nk, :]] * (nq // LANES), axis=1)
                if masked:
                    k_idx = k0 + lax.broadcasted_iota(jnp.int32, (nk, 1), 0)
                    q_idx = n * t + q0 + lax.broadcasted_iota(jnp.int32, (1, nq), 1)
                    vis = (k_idx <= q_idx) if fox else ((k_idx >> CHUNK_SHIFT) <= (q_idx >> CHUNK_SHIFT))
                    s = jnp.where(vis, s, NEG_INF)
                sb_refs[hh][k0:k0 + nk, q0:] = s
                m8 = on_lanes(m8, q0, jnp.max(s.reshape(nk // SUBLANES, SUBLANES, nq), axis=0), jnp.maximum)
            m = jnp.max(m8, axis=0, keepdims=True)
            acc = jnp.zeros((VT_ROWS, t), F32)
            for k0, nk, q0, masked in blocks:
                x = sb_refs[hh][k0:k0 + nk, q0:] - m[:, q0:]
                p = jnp.exp(x) if fox else jnp.exp2(x * exp2_scale)
                pv = jnp.dot(vt_ref[hh, :, k0:k0 + nk], p.astype(BF16), preferred_element_type=F32)
                acc = on_lanes(acc, q0, pv, jnp.add)
            outs.append(acc[:D_FOX] / acc[D_FOX:D_FOX + 1])
        o_ref[0, qrows, :] = jnp.transpose(jnp.concatenate(outs, axis=0)).astype(BF16)

    for n in range(n_tiles):
        body(n)


def _flash(args, *, fox, B, S, scale):
    t = min(T_ATT, S)
    n = S // t
    npairs = (H_FOX if fox else H_MLA) // 2
    args = tuple(a.reshape(B, S, a.shape[-1]) if a.ndim == 2 else a for a in args)
    seq = lambda b, hp: (b, 0, hp)
    whole = lambda b, hp: (b, 0, 0)
    scratch = [pltpu.VMEM((2, S, LANES), F32 if fox else BF16), pltpu.VMEM((2, VT_ROWS, S), BF16),
               pltpu.VMEM((S, t), F32), pltpu.VMEM((S, t), F32)]
    if fox:
        in_specs = [pl.BlockSpec((1, S, LANES), seq), pl.BlockSpec((1, S, LANES), seq),
                    pl.BlockSpec((1, S, LANES), seq), pl.BlockSpec((1, S, LANES), whole),
                    pl.BlockSpec((1, H_FOX, S), whole)]
    else:
        in_specs = [pl.BlockSpec((1, S, 2 * LANES), seq), pl.BlockSpec((1, S, 2 * LANES), seq),
                    pl.BlockSpec((1, S, LANES), whole)]
    out = pl.pallas_call(
        functools.partial(_flash_kernel, fox=fox, scale=scale, t=t, n_tiles=n, S=S),
        grid=(B, npairs),
        in_specs=in_specs,
        out_specs=pl.BlockSpec((1, S, LANES), seq),
        out_shape=jax.ShapeDtypeStruct((B, S, npairs * LANES), BF16),
        scratch_shapes=scratch,
        compiler_params=_params("parallel", "parallel"),
        name="fox_attention" if fox else "mla_attention",
    )(*args)
    return out.reshape(B * S, npairs * LANES)


def _dsa_kernel(qi_ref, qb_ref, wit_ref, ki_ref, kb_ref, va_ref, o_ref,
                qim_ref, qbm_ref, sc_ref, sb_ref, *, nk, i0, ksel):
    i = i0 + pl.program_id(1)
    chunks = [slice(k0, min(k0 + KC_DSA, nk)) for k0 in range(0, nk, KC_DSA)]
    lane = _lane_iota()
    lo = lane < D_DSA
    hi = lane >= D_DSA
    q_chunk = (i * Q_DSA + lane) >> CHUNK_SHIFT
    select = nk > ksel

    def key_idx(rows):
        return rows.start + lax.broadcasted_iota(jnp.int32, (rows.stop - rows.start, 1), 0)

    def admissible(rows):
        return (key_idx(rows) >> CHUNK_SHIFT) <= q_chunk

    def fold_max(s):
        return jnp.max(s.reshape(s.shape[0] // SUBLANES, SUBLANES, s.shape[1]), axis=0)

    for h in range(H_DSA):
        blk = qb_ref[:, (h // 2) * LANES:(h // 2 + 1) * LANES] * jnp.asarray(D_DSA ** -0.5, BF16)
        qbm_ref[h * Q_DSA:(h + 1) * Q_DSA, :] = jnp.where(lo if h % 2 == 0 else hi, blk, jnp.zeros_like(blk))

    if select:
        for h in range(H_IDX):
            blk = qi_ref[:, (h // 2) * LANES:(h // 2 + 1) * LANES]
            qim_ref[h * Q_DSA:(h + 1) * Q_DSA, :] = jnp.where(lo if h % 2 == 0 else hi, blk, jnp.zeros_like(blk))
        w_flat = jnp.concatenate([wit_ref[h:h + 1, :] for h in range(H_IDX)], axis=1)
        for rows in chunks:
            r = lax.dot_general(ki_ref[0, rows, :], qim_ref[...], _NT, preferred_element_type=F32)
            r = jnp.maximum(r, 0.0) * w_flat
            acc = r[:, :LANES]
            for h in range(1, H_IDX):
                acc = acc + r[:, h * LANES:(h + 1) * LANES]
            sc_ref[rows, :] = jnp.where(admissible(rows), acc, NEG_INF)

        def count(mask):
            part = jnp.sum(mask.astype(F32).reshape(nk // 32, 32, LANES), axis=0)
            return jnp.sum(part, axis=0, keepdims=True)

        def as_float(key):
            return lax.bitcast_convert_type(jnp.where(key >= 0, key, key ^ 0x7FFFFFFF), F32)

        def bit_body(r, ans):
            cand = ans + lax.shift_left(jnp.int32(1), 31 - r)
            cnt = count(sc_ref[...] >= as_float(cand))
            return jnp.where(cnt >= ksel, cand, ans)
        ans = lax.fori_loop(0, 32, bit_body, jnp.full((1, LANES), jnp.iinfo(jnp.int32).min, jnp.int32))
        t = as_float(ans)
        s_all = sc_ref[...]
        need = ksel - count(s_all > t)
        n_ge = count(s_all >= t)
        k_all = lax.broadcasted_iota(jnp.int32, (nk, 1), 0)

        def tie_bound():
            def bit_body2(r, aj):
                cand = aj + lax.shift_left(jnp.int32(1), 11 - r)
                g = count((sc_ref[...] == t) & (k_all < cand))
                return jnp.where(g <= need, cand, aj)
            return lax.fori_loop(0, 12, bit_body2, jnp.zeros((1, LANES), jnp.int32))

        jmax = lax.cond(jnp.max(n_ge) > ksel, tie_bound,
                        lambda: jnp.full((1, LANES), 4095, jnp.int32))
        for rows in chunks:
            s = sc_ref[rows, :]
            sel = (s > t) | ((s == t) & (key_idx(rows) < jmax))
            sc_ref[rows, :] = jnp.where(sel & admissible(rows), 0.0, NEG_INF)
    else:
        for rows in chunks:
            sc_ref[rows, :] = jnp.where(admissible(rows), 0.0, NEG_INF)

    m8 = jnp.full((SUBLANES, H_DSA * LANES), -jnp.inf, F32)
    for rows in chunks:
        s = lax.dot_general(kb_ref[0, rows, :], qbm_ref[...], _NT, preferred_element_type=F32)
        s = s + jnp.concatenate([sc_ref[rows, :]] * H_DSA, axis=1)
        sb_ref[rows, :] = s
        m8 = jnp.maximum(m8, fold_max(s))
    m = jnp.max(m8, axis=0, keepdims=True)
    acc = jnp.zeros((H_DSA * Q_DSA, LANES), F32)
    for rows in chunks:
        p = jnp.exp(sb_ref[rows, :] - m).astype(BF16)
        acc = acc + lax.dot_general(p, va_ref[0, rows, :], _TN, preferred_element_type=F32)

    for hp in range(H_DSA // 2):
        a = acc[(2 * hp) * Q_DSA:(2 * hp + 1) * Q_DSA]
        b = acc[(2 * hp + 1) * Q_DSA:(2 * hp + 2) * Q_DSA]
        ra = a / pltpu.roll(a, D_DSA, 1)
        rb = b / pltpu.roll(b, D_DSA, 1)
        o_ref[:, hp * LANES:(hp + 1) * LANES] = jnp.where(lo, ra, pltpu.roll(rb, D_DSA, 1)).astype(BF16)


def _dsa(qi, qb, wit, ki2, kb2, va, *, B, S, ksel):
    nb = S // Q_DSA
    per = KC_DSA // Q_DSA
    seq3 = lambda a: a.reshape(B, S, LANES)
    ki3, kb3, va3 = seq3(ki2), seq3(kb2), seq3(va)
    outs = []
    for g in range(nb // per):
        nk, i0 = (g + 1) * per * Q_DSA, g * per
        qrow = lambda n, i0=i0: pl.BlockSpec((Q_DSA, n), lambda b, j: (b * nb + i0 + j, 0))
        seq = pl.BlockSpec((1, nk, LANES), lambda b, j: (b, 0, 0))
        o = pl.pallas_call(
            functools.partial(_dsa_kernel, nk=nk, i0=i0, ksel=ksel),
            grid=(B, per),
            in_specs=[qrow(H_IDX * D_IDX), qrow(H_DSA * D_DSA),
                      pl.BlockSpec((SUBLANES, Q_DSA), lambda b, j, i0=i0: (0, b * nb + i0 + j)),
                      seq, seq, seq],
            out_specs=pl.BlockSpec((Q_DSA, H_DSA * D_DSA), lambda b, j: (b * per + j, 0)),
            out_shape=jax.ShapeDtypeStruct((B * per * Q_DSA, H_DSA * D_DSA), BF16),
            scratch_shapes=[pltpu.VMEM((H_IDX * Q_DSA, LANES), BF16), pltpu.VMEM((H_DSA * Q_DSA, LANES), BF16),
                            pltpu.VMEM((nk, LANES), F32), pltpu.VMEM((nk, H_DSA * LANES), F32)],
            compiler_params=_params("parallel", "arbitrary"),
            name="dsa_attention_%d" % nk,
        )(qi, qb, wit, ki3, kb3, va3)
        outs.append(o.reshape(B, per * Q_DSA, H_DSA * D_DSA))
    return jnp.concatenate(outs, axis=1).reshape(B * S, H_DSA * D_DSA)


_R_SUB = N_GROUPS
_PAIRS = [(a, b) for a in range(EXPERTS_PER_GROUP) for b in range(a + 1, EXPERTS_PER_GROUP)]
N_CLASSES = N_GROUPS * len(_PAIRS)


def _out_ln_kernel(*refs, n_in):
    o_refs, w_refs = refs[:n_in], refs[n_in:2 * n_in]
    x_ref, g_ref, b_ref, wrt_ref, brt_ref, tri_ref, out_ref, route_ref, cnt_ref, run_ref = refs[2 * n_in:]
    y = jnp.dot(o_refs[0][...], w_refs[0][...], preferred_element_type=F32)
    for o_r, w_r in zip(o_refs[1:], w_refs[1:]):
        y = y + jnp.dot(o_r[...], w_r[...], preferred_element_type=F32)
    x_new = _layer_norm(DN_ALPHA * x_ref[...] + y, g_ref[...], b_ref[...])
    out_ref[...] = x_new
    _route_rows(x_new, wrt_ref, brt_ref, tri_ref, route_ref, cnt_ref, run_ref)


def _out_ln(os_, ws, x2, g, b, wr, br):
    T = x2.shape[0]
    tm = min(TM_PROJ, T)
    row = lambda c: pl.BlockSpec((tm, c), lambda i: (i, 0))
    tri = jnp.triu(jnp.ones((tm, tm), BF16), 1)
    wrt, brt = wr.T, br.T
    return pl.pallas_call(
        functools.partial(_out_ln_kernel, n_in=len(os_)),
        grid=(T // tm,),
        in_specs=[row(o.shape[1]) for o in os_] + [_full_spec(w.shape) for w in ws]
                 + [row(D_MODEL), _full_spec(g.shape), _full_spec(b.shape),
                    _full_spec(wrt.shape), _full_spec(brt.shape), _full_spec(tri.shape)],
        out_specs=[row(D_MODEL), pl.BlockSpec((1, SUBLANES, tm), lambda i: (i, 0, 0)), _full_spec((LANES, LANES))],
        out_shape=[jax.ShapeDtypeStruct((T, D_MODEL), F32), jax.ShapeDtypeStruct((T // tm, SUBLANES, tm), F32),
                   jax.ShapeDtypeStruct((LANES, LANES), F32)],
        scratch_shapes=[pltpu.VMEM((LANES, 1), F32)],
        compiler_params=_params("arbitrary"),
        name="out_proj_ln",
    )(*os_, *ws, x2, g, b, wrt, brt, tri)


def _route_rows(x, wrt_ref, brt_ref, tri_ref, route_ref, cnt_ref, run_ref):
    @pl.when(pl.program_id(0) == 0)
    def _():
        run_ref[...] = jnp.zeros(run_ref.shape, F32)

    tm = x.shape[0]
    lg = lax.dot_general(wrt_ref[...], x.astype(BF16), _NT, preferred_element_type=F32) + brt_ref[...]
    r = lax.broadcasted_iota(jnp.int32, (LANES, 1), 0)
    rf = r.astype(F32)
    far = float(LANES)
    grp = jnp.where(r < N_GROUPS, lg, -jnp.inf)
    mg = jnp.max(grp, axis=0, keepdims=True)
    g_star = jnp.min(jnp.where(grp == mg, rf, far), axis=0, keepdims=True)
    row_grp = ((r - _R_SUB) >> 2).astype(F32)
    in_grp = (row_grp == g_star) & (r >= _R_SUB) & (r < _R_SUB + N_EXPERTS)
    l1 = jnp.where(in_grp, lg, -jnp.inf)
    v1 = jnp.max(l1, axis=0, keepdims=True)
    j1 = jnp.min(jnp.where(l1 == v1, rf, far), axis=0, keepdims=True)
    l2 = jnp.where(in_grp & (rf != j1), lg, -jnp.inf)
    v2 = jnp.max(l2, axis=0, keepdims=True)
    j2 = jnp.min(jnp.where(l2 == v2, rf, far), axis=0, keepdims=True)
    base = _R_SUB + EXPERTS_PER_GROUP * g_star
    ja = jnp.minimum(j1, j2) - base
    jb = jnp.maximum(j1, j2) - base
    cls = g_star * len(_PAIRS) + ja * (7.0 - ja) * 0.5 + (jb - ja - 1.0)
    onehot = rf == cls
    ones = jnp.where(onehot, 1.0, 0.0)
    before = jnp.dot(ones.astype(BF16), tri_ref[...], preferred_element_type=F32)
    rank = jnp.sum(jnp.where(onehot, before + run_ref[...], 0.0), axis=0, keepdims=True)
    row8 = lax.broadcasted_iota(jnp.int32, (SUBLANES, 1), 0)
    route_ref[0] = jnp.where(row8 == 0, cls, jnp.where(row8 == 1, rank, 0.0))
    run_ref[...] += jnp.sum(ones, axis=1, keepdims=True)
    cnt_ref[...] = jnp.broadcast_to(run_ref[...], cnt_ref.shape)


def _row_copy(src_ref, dst_ref, src_row, dst_row, sem):
    return pltpu.make_async_copy(src_ref.at[pl.ds(src_row, 1), :], dst_ref.at[pl.ds(dst_row, 1), :], sem)


def _dispatch_kernel(pos_ref, x_ref, init_ref, xs_ref, sem):
    del init_ref
    tm = x_ref.shape[0]

    def start(g, carry):
        base = pl.multiple_of(g * SUBLANES, SUBLANES)
        for j in range(SUBLANES):
            _row_copy(x_ref, xs_ref, base + j, pos_ref[0, 0, base + j], sem).start()
        return carry
    lax.fori_loop(0, tm // SUBLANES, start, 0)

    def wait(t, carry):
        _row_copy(x_ref, xs_ref, 0, 0, sem).wait()
        return carry
    lax.fori_loop(0, tm, wait, 0, unroll=8)


def _collect_kernel(pos_ref, ys_ref, out_ref, sem):
    tm = out_ref.shape[0]

    def start(g, carry):
        base = pl.multiple_of(g * SUBLANES, SUBLANES)
        for j in range(SUBLANES):
            _row_copy(ys_ref, out_ref, pos_ref[0, 0, base + j], base + j, sem).start()
        return carry
    lax.fori_loop(0, tm // SUBLANES, start, 0)

    def wait(t, carry):
        _row_copy(ys_ref, out_ref, 0, 0, sem).wait()
        return carry
    lax.fori_loop(0, tm, wait, 0, unroll=8)


def _dispatch(x2, pos3, n_rows):
    T = x2.shape[0]
    tm = pos3.shape[-1]
    return pl.pallas_call(
        _dispatch_kernel,
        grid=(T // tm,),
        in_specs=[pl.BlockSpec((1, 1, tm), lambda i: (i, 0, 0), memory_space=pltpu.SMEM),
                  pl.BlockSpec((tm, D_MODEL), lambda i: (i, 0)),
                  pl.BlockSpec(memory_space=pl.ANY)],
        out_specs=pl.BlockSpec(memory_space=pl.ANY),
        out_shape=jax.ShapeDtypeStruct((n_rows, D_MODEL), F32),
        scratch_shapes=[pltpu.SemaphoreType.DMA(())],
        input_output_aliases={2: 0},
        compiler_params=_params("arbitrary"),
        name="moe_dispatch",
    )(pos3, x2, jnp.zeros((n_rows, D_MODEL), F32))


def _collect(ys, pos3):
    T = pos3.shape[0] * pos3.shape[-1]
    tm = pos3.shape[-1]
    return pl.pallas_call(
        _collect_kernel,
        grid=(T // tm,),
        in_specs=[pl.BlockSpec((1, 1, tm), lambda i: (i, 0, 0), memory_space=pltpu.SMEM),
                  pl.BlockSpec(memory_space=pl.ANY)],
        out_specs=pl.BlockSpec((tm, D_MODEL), lambda i: (i, 0)),
        out_shape=jax.ShapeDtypeStruct((T, D_MODEL), F32),
        scratch_shapes=[pltpu.SemaphoreType.DMA(())],
        compiler_params=_params("arbitrary"),
        name="moe_collect",
    )(pos3, ys)


def _expert_kernel(tg_ref, ta_ref, tb_ref, x_ref, wr_ref, br_ref,
                   wga_ref, wua_ref, wda_ref, wgb_ref, wub_ref, wdb_ref, g_ref, b_ref, out_ref, z_ref):
    step = pl.program_id(0)
    r = jnp.minimum(step, pl.num_programs(0) - 2)

    @pl.when(step == 0)
    def _():
        z_ref[...] = jnp.zeros(z_ref.shape, F32)

    out_ref[...] = _layer_norm(z_ref[...], g_ref[...], b_ref[...])
    x = x_ref[...]
    xb = x.astype(BF16)
    lg = jnp.dot(xb, wr_ref[...], preferred_element_type=F32) + br_ref[...]
    lane = _lane_iota()
    is_grp = lane < N_GROUPS
    mg = jnp.max(jnp.where(is_grp, lg, -jnp.inf), axis=-1, keepdims=True)
    den = jnp.sum(jnp.where(is_grp, jnp.exp(lg - mg), 0.0), axis=-1, keepdims=True)
    pick = lambda l: jnp.sum(jnp.where(lane == l, lg, 0.0), axis=-1, keepdims=True)
    p_top = jnp.exp(pick(tg_ref[r]) - mg) / den
    va, vb = pick(_R_SUB + ta_ref[r]), pick(_R_SUB + tb_ref[r])
    e = jnp.exp(jnp.minimum(va, vb) - jnp.maximum(va, vb))
    w_top, w_oth = 1.0 / (1.0 + e), e / (1.0 + e)
    a_top = va >= vb
    gates = (jnp.where(a_top, w_top, w_oth) * p_top, jnp.where(a_top, w_oth, w_top) * p_top)
    y = jnp.zeros(x.shape, F32)
    for gate, wg, wu, wd in ((gates[0], wga_ref, wua_ref, wda_ref), (gates[1], wgb_ref, wub_ref, wdb_ref)):
        a = jnp.dot(xb, wg[0], preferred_element_type=F32)
        u = jnp.dot(xb, wu[0], preferred_element_type=F32)
        h = (a * (1.0 / (1.0 + jnp.exp(-a))) * u).astype(BF16)
        y = y + gate * jnp.dot(h, wd[0], preferred_element_type=F32)
    z_ref[...] = DN_ALPHA * x + y


def _experts(xs, tile_g, tile_a, tile_b, wr, br, wg, wu, wd, g, b):
    n_tiles = tile_g.shape[0]
    te = xs.shape[0] // n_tiles
    tile = lambda r: jnp.minimum(r, n_tiles - 1)
    x_row = pl.BlockSpec((te, D_MODEL), lambda r, *_: (tile(r), 0))
    out_row = pl.BlockSpec((te, D_MODEL), lambda r, *_: (jnp.maximum(r - 1, 0), 0))
    full = lambda shape: pl.BlockSpec(shape, lambda r, *_: (0,) * len(shape))
    up_a = pl.BlockSpec((1, D_MODEL, D_EXPERT), lambda r, tg, ta, tb: (ta[tile(r)], 0, 0))
    up_b = pl.BlockSpec((1, D_MODEL, D_EXPERT), lambda r, tg, ta, tb: (tb[tile(r)], 0, 0))
    dn_a = pl.BlockSpec((1, D_EXPERT, D_MODEL), lambda r, tg, ta, tb: (ta[tile(r)], 0, 0))
    dn_b = pl.BlockSpec((1, D_EXPERT, D_MODEL), lambda r, tg, ta, tb: (tb[tile(r)], 0, 0))
    return pl.pallas_call(
        _expert_kernel,
        grid_spec=pltpu.PrefetchScalarGridSpec(
            num_scalar_prefetch=3, grid=(n_tiles + 1,),
            in_specs=[x_row, full(wr.shape), full(br.shape), up_a, up_a, dn_a, up_b, up_b, dn_b,
                      full(g.shape), full(b.shape)],
            out_specs=out_row,
            scratch_shapes=[pltpu.VMEM((te, D_MODEL), F32)]),
        out_shape=jax.ShapeDtypeStruct(xs.shape, F32),
        compiler_params=_params("arbitrary"),
        name="moe_experts",
    )(tile_g, tile_a, tile_b, xs, wr, br, wg, wu, wd, wg, wu, wd, g, b)


def _slot_kernel(route_ref, start_ref, pos_ref):
    rf = lax.broadcasted_iota(jnp.int32, (LANES, 1), 0).astype(F32)
    for j in range(route_ref.shape[0]):
        cls, rank = route_ref[j, 0:1, :], route_ref[j, 1:2, :]
        start = jnp.sum(jnp.where(rf == cls, start_ref[...], 0.0), axis=0, keepdims=True)
        pos_ref[j] = (start + rank).astype(jnp.int32)


def _slots(route, starts):
    n, _, tm = route.shape
    nb = min(n, 16)
    slots = pl.pallas_call(
        _slot_kernel,
        grid=(n // nb,),
        in_specs=[pl.BlockSpec((nb, SUBLANES, tm), lambda i: (i, 0, 0)), _full_spec(starts.shape)],
        out_specs=pl.BlockSpec((nb, 1, tm), lambda i: (i, 0, 0)),
        out_shape=jax.ShapeDtypeStruct((n, 1, tm), jnp.int32),
        compiler_params=_params("parallel"),
        name="moe_slots",
    )(route, starts)
    move = min(TM_MOVE, n * tm)
    return slots.reshape(n * tm // move, 1, move)


def _moe_ln(x2, route, counts, wr, br, wg, wu, wd, g, b):
    T = x2.shape[0]
    te = TE_MOE
    n_tiles = (T + N_CLASSES * (te - 1) + te - 1) // te
    cnt = counts[:N_CLASSES, 0].astype(jnp.int32)
    padded = (cnt + te - 1) // te * te
    ends = jnp.cumsum(padded)
    starts = jnp.pad((ends - padded).astype(F32), (0, LANES - N_CLASSES))[:, None]
    pos3 = _slots(route, starts)
    tile_start = jnp.arange(n_tiles, dtype=jnp.int32) * te
    tile_cls = jnp.minimum(jnp.sum(tile_start[:, None] >= ends[None, :], axis=1), N_CLASSES - 1).astype(jnp.int32)
    pair = jnp.asarray(_PAIRS, jnp.int32)[tile_cls % len(_PAIRS)]
    tile_g = tile_cls // len(_PAIRS)
    tile_a = tile_g * EXPERTS_PER_GROUP + pair[:, 0]
    tile_b = tile_g * EXPERTS_PER_GROUP + pair[:, 1]
    xs = _dispatch(x2, pos3, n_tiles * te)
    ys = _experts(xs, tile_g, tile_a, tile_b, wr, br, wg, wu, wd, g, b)
    return _collect(ys, pos3)


def _router_weights(w_grp, b_grp, w_sub, b_sub):
    pad = LANES - N_GROUPS - N_EXPERTS
    wr = jnp.concatenate([w_grp, w_sub, jnp.zeros((D_MODEL, pad), w_grp.dtype)], axis=1).astype(BF16)
    br = jnp.concatenate([b_grp, b_sub, jnp.zeros((pad,), b_grp.dtype)])[None].astype(F32)
    return wr, br


def kernel(x, positions, ev_w_in, ev_g_q, ev_g_kv, ev_w_uq, ev_w_ukv, ev_w_o, od_w_in, od_b_f, od_w_o,
           moe_w_grp, moe_b_grp, moe_w_sub, moe_b_sub, moe_w_gate, moe_w_up, moe_w_down,
           ln1_g, ln1_b, ln2_g, ln2_b):
    B, S, D = x.shape
    T = B * S
    ksel = min(TOPK_MAX, S // 4)
    x2 = x.reshape(T, D)
    tabs = _rope_tables(positions.reshape(T, 1).astype(F32))
    row = lambda v: v[None].astype(F32)
    for layer in range(DEPTH):
        j = layer // 2
        wr, br = _router_weights(moe_w_grp[layer], moe_b_grp[layer], moe_w_sub[layer], moe_b_sub[layer])
        ln1 = (row(ln1_g[layer]), row(ln1_b[layer]), wr, br)
        if layer % 2 == 0:
            w1, wuq, wukv = _even_weights(ev_w_in[j], ev_w_uq[j], ev_w_ukv[j])
            qa, kv, kr, qb, kb2, va, qi, ki2, wi = _even_proj(
                x2, w1, row(ev_g_q[j]), row(ev_g_kv[j]), wuq, wukv, tabs)
            o_a = _flash((qa, kv, kr), fox=False, B=B, S=S, scale=(D_NOPE + D_ROPE) ** -0.5)
            wit = jnp.pad(wi[:, :H_IDX].T, ((0, SUBLANES - H_IDX), (0, 0)))
            o_b = _dsa(qi, qb, wit, ki2, kb2, va, B=B, S=S, ksel=ksel)
            wo = ev_w_o[j].astype(BF16)
            n_a = H_MLA * D_V_MLA
            x2, route, counts = _out_ln((o_a, o_b), (wo[:n_a], wo[n_a:]), x2, *ln1)
        else:
            n = H_FOX * D_FOX
            w = jnp.pad(od_w_in[j], ((0, 0), (0, LANES - H_FOX))).astype(BF16)
            bf = jnp.pad(od_b_f[j], (0, LANES - H_FOX))[None].astype(F32)
            q, k, v, lf = _odd_proj(x2, w, bf)
            c = _cumsum_seq(lf, S)
            c_rows = c.reshape(B, S, LANES)[:, :, :H_FOX].swapaxes(1, 2)
            o = _flash((q, k, v, c, c_rows), fox=True, B=B, S=S, scale=D_FOX ** -0.5)
            x2, route, counts = _out_ln((o,), (od_w_o[j].astype(BF16),), x2, *ln1)
        x2 = _moe_ln(x2, route, counts, wr, br, moe_w_gate[layer].astype(BF16), moe_w_up[layer].astype(BF16),
                     moe_w_down[layer].astype(BF16), row(ln2_g[layer]), row(ln2_b[layer]))
    return x2.reshape(B, S, D)
```

```python
import functools

import jax
import jax.numpy as jnp
from jax import lax
from jax.experimental import pallas as pl
from jax.experimental.pallas import tpu as pltpu

D_MODEL = 1024
DEPTH = 4
CHUNK = 64
CHUNK_SHIFT = 6
ROPE_THETA = 10000.0
NEG_INF = -1e30
LN_EPS = 1e-5
RMS_EPS = 1e-6
H_MLA, D_NOPE, D_ROPE, D_V_MLA, Q_LORA, KV_LORA = 8, 64, 32, 64, 384, 256
H_DSA, D_DSA, H_IDX, D_IDX, TOPK_MAX = 8, 64, 4, 64, 256
H_FOX, D_FOX = 16, 64
N_GROUPS, EXPERTS_PER_GROUP, D_EXPERT = 4, 4, 512
N_EXPERTS = N_GROUPS * EXPERTS_PER_GROUP
DN_ALPHA = (2 * DEPTH) ** 0.25
EV_SPLITS = (Q_LORA, KV_LORA, D_ROPE, H_DSA * D_DSA, D_DSA, D_DSA, H_IDX * D_IDX, D_IDX, H_IDX)

LANES = 128
SUBLANES = 8
VMEM_LIMIT_BYTES = 56 * 1024 * 1024

TM_PROJ = 1024
TM_MOVE = 2048
TE_MOE = 256
T_ATT = 512
VT_ROWS = 80
Q_DSA = 128
KC_DSA = 256
NARROW_EXP = 4
NARROW_ROUNDS = (NARROW_EXP * 2 ** 23 + 1).bit_length()
NARROW_MIN_TOP = 1e-30

F32 = jnp.float32
BF16 = jnp.bfloat16
_NT = (((1,), (1,)), ((), ()))
_TN = (((0,), (0,)), ((), ()))


def _params(*sem):
    return pltpu.CompilerParams(dimension_semantics=sem, vmem_limit_bytes=VMEM_LIMIT_BYTES)


def _lane_iota(shape=(1, LANES)):
    return lax.broadcasted_iota(jnp.int32, shape, len(shape) - 1)


def _full_spec(shape):
    return pl.BlockSpec(shape, lambda *_: (0,) * len(shape))


def _rope_table_kernel(pos_ref, inv32_ref, sg32_ref, inv64_ref, sg64_ref,
                       c32_ref, s32_ref, c64_ref, s64_ref):
    pos = pos_ref[...]
    a32 = pos * inv32_ref[...]
    a64 = pos * inv64_ref[...]
    c32_ref[...] = jnp.cos(a32)
    s32_ref[...] = jnp.sin(a32) * sg32_ref[...]
    c64_ref[...] = jnp.cos(a64)
    s64_ref[...] = jnp.sin(a64) * sg64_ref[...]


def _rope_tables(posf):
    T = posf.shape[0]
    tm = min(T, 2048)
    lane = jnp.arange(LANES)
    inv16 = jnp.power(ROPE_THETA, -jnp.arange(0, D_ROPE, 2, dtype=F32) / D_ROPE)
    inv32 = jnp.power(ROPE_THETA, -jnp.arange(0, D_DSA, 2, dtype=F32) / D_DSA)
    in32 = (lane >= D_NOPE) & (lane < D_NOPE + D_ROPE)
    t32 = jnp.where(in32, inv16[(lane - D_NOPE) % (D_ROPE // 2)], 0.0).astype(F32)[None]
    g32 = jnp.where(in32, jnp.where(lane < D_NOPE + D_ROPE // 2, -1.0, 1.0), 0.0).astype(F32)[None]
    t64 = inv32[lane % (D_DSA // 2)].astype(F32)[None]
    g64 = jnp.where((lane % D_DSA) < D_DSA // 2, -1.0, 1.0).astype(F32)[None]
    row = pl.BlockSpec((tm, LANES), lambda i: (i, 0))
    tab = jax.ShapeDtypeStruct((T, LANES), F32)
    return pl.pallas_call(
        _rope_table_kernel,
        grid=(T // tm,),
        in_specs=[pl.BlockSpec((tm, 1), lambda i: (i, 0))] + [_full_spec((1, LANES))] * 4,
        out_specs=[row] * 4,
        out_shape=[tab] * 4,
        compiler_params=_params("parallel"),
        name="rope_tables",
    )(posf, t32, g32, t64, g64)


def _rope_block(y, c, s, half, second_half):
    partner = jnp.where(second_half, pltpu.roll(y, half, 1), pltpu.roll(y, LANES - half, 1))
    return y * c + partner * s


def _rms(x, g):
    return x * lax.rsqrt(jnp.mean(x * x, axis=-1, keepdims=True) + RMS_EPS) * g


def _layer_norm(z, g, b):
    mu = jnp.mean(z, axis=-1, keepdims=True)
    d = z - mu
    var = jnp.mean(d * d, axis=-1, keepdims=True)
    return d * lax.rsqrt(var + LN_EPS) * g + b


_E_CQ, _E_CKV, _E_KR, _E_QB, _E_KB, _E_VB, _E_QI, _E_KI, _E_WI, _E_END = (
    0, 384, 640, 768, 1280, 1408, 1536, 1792, 1920, 2048)


def _even_proj_kernel(x_ref, w1_ref, gq_ref, gkv_ref, wuq_ref, wukv_ref,
                      c32_ref, s32_ref, c64_ref, s64_ref,
                      qa_ref, kv_ref, kr_ref, qb_ref, kb_ref, va_ref, qi_ref, ki_ref, wi_ref):
    xb = x_ref[...].astype(BF16)
    lane = _lane_iota()
    sec32 = (lane >= D_NOPE + D_ROPE // 2) & (lane < D_NOPE + D_ROPE)
    sec64 = (lane % D_DSA) >= D_DSA // 2
    c32, s32, c64, s64 = c32_ref[...], s32_ref[...], c64_ref[...], s64_ref[...]

    def seg(a, b):
        return jnp.dot(xb, w1_ref[:, a:b], preferred_element_type=F32)

    def rope64_store(h, out_ref):
        for j in range(h.shape[1] // LANES):
            blk = h[:, j * LANES:(j + 1) * LANES]
            out_ref[:, j * LANES:(j + 1) * LANES] = _rope_block(
                blk, c64, s64, D_DSA // 2, sec64).astype(out_ref.dtype)

    cq = _rms(seg(_E_CQ, _E_CKV), gq_ref[...]).astype(BF16)
    q = jnp.dot(cq, wuq_ref[...], preferred_element_type=F32)
    for h in range(H_MLA):
        blk = q[:, h * LANES:(h + 1) * LANES]
        qa_ref[:, h * LANES:(h + 1) * LANES] = _rope_block(
            blk, c32, s32, D_ROPE // 2, sec32).astype(BF16)
    ckv = _rms(seg(_E_CKV, _E_KR), gkv_ref[...]).astype(BF16)
    kv_ref[...] = jnp.dot(ckv, wukv_ref[...], preferred_element_type=F32).astype(BF16)
    kr_ref[...] = _rope_block(seg(_E_KR, _E_QB), c32, s32, D_ROPE // 2, sec32).astype(BF16)
    rope64_store(seg(_E_QB, _E_KB), qb_ref)
    rope64_store(seg(_E_KB, _E_VB), kb_ref)
    va_ref[...] = jnp.where(lane < D_DSA, seg(_E_VB, _E_QI), 1.0).astype(BF16)
    rope64_store(seg(_E_QI, _E_KI), qi_ref)
    rope64_store(seg(_E_KI, _E_WI), ki_ref)
    wi_ref[...] = seg(_E_WI, _E_END)


def _even_weights(w_in, w_uq, w_ukv):
    o = [0]
    for s_ in EV_SPLITS:
        o.append(o[-1] + s_)
    cq, ckv, kr, qb, kb, vb, qi, ki, wi = (w_in[:, o[i]:o[i + 1]] for i in range(9))
    z = lambda n: jnp.zeros((D_MODEL, n), w_in.dtype)
    w1 = jnp.concatenate([
        cq, ckv,
        z(D_NOPE), kr, z(LANES - D_NOPE - D_ROPE),
        qb,
        kb, kb,
        vb, z(LANES - D_DSA),
        qi,
        ki, ki,
        wi, z(LANES - H_IDX)], axis=1).astype(BF16)
    wq = w_uq.reshape(Q_LORA, H_MLA, D_NOPE + D_ROPE)
    wq = jnp.pad(wq, ((0, 0), (0, 0), (0, LANES - D_NOPE - D_ROPE))).reshape(Q_LORA, H_MLA * LANES)
    return w1, wq.astype(BF16), w_ukv.astype(BF16)


def _even_proj(x2, w1, gq, gkv, wuq, wukv, tabs):
    T = x2.shape[0]
    tm = min(TM_PROJ, T)
    row = lambda n: pl.BlockSpec((tm, n), lambda i: (i, 0))
    outs = [(H_MLA * LANES, BF16), (H_MLA * LANES, BF16), (LANES, BF16), (H_DSA * D_DSA, BF16),
            (LANES, BF16), (LANES, BF16), (H_IDX * D_IDX, BF16), (LANES, BF16), (LANES, F32)]
    return pl.pallas_call(
        _even_proj_kernel,
        grid=(T // tm,),
        in_specs=[row(D_MODEL), _full_spec(w1.shape), _full_spec(gq.shape), _full_spec(gkv.shape),
                  _full_spec(wuq.shape), _full_spec(wukv.shape)] + [row(LANES)] * 4,
        out_specs=[row(n) for n, _ in outs],
        out_shape=[jax.ShapeDtypeStruct((T, n), d) for n, d in outs],
        compiler_params=_params("parallel"),
        name="even_proj",
    )(x2, w1, gq, gkv, wuq, wukv, *tabs)


def _odd_proj_kernel(x_ref, w_ref, bf_ref, q_ref, k_ref, v_ref, lf_ref):
    xb = x_ref[...].astype(BF16)
    n = H_FOX * D_FOX
    for j, out in enumerate((q_ref, k_ref, v_ref)):
        out[...] = jnp.dot(xb, w_ref[:, j * n:(j + 1) * n], preferred_element_type=F32).astype(BF16)
    z = jnp.dot(xb, w_ref[:, 3 * n:], preferred_element_type=F32) + bf_ref[...]
    lf_ref[...] = jnp.minimum(z, 0.0) - jnp.log1p(jnp.exp(-jnp.abs(z)))


def _odd_proj(x2, w, bf):
    T = x2.shape[0]
    tm = min(TM_PROJ, T)
    n = H_FOX * D_FOX
    row = lambda c: pl.BlockSpec((tm, c), lambda i: (i, 0))
    return pl.pallas_call(
        _odd_proj_kernel,
        grid=(T // tm,),
        in_specs=[row(D_MODEL), _full_spec(w.shape), _full_spec(bf.shape)],
        out_specs=[row(n), row(n), row(n), row(LANES)],
        out_shape=[jax.ShapeDtypeStruct((T, n), BF16)] * 3 + [jax.ShapeDtypeStruct((T, LANES), F32)],
        compiler_params=_params("parallel"),
        name="odd_proj",
    )(x2, w, bf)


def _cumsum_kernel(lf_ref, c_ref):
    a = lf_ref[...]
    rows = lax.broadcasted_iota(jnp.int32, a.shape, 0)
    k = 1
    while k < a.shape[0]:
        a = a + jnp.where(rows >= k, pltpu.roll(a, k, 0), 0.0)
        k *= 2
    c_ref[...] = a


def _cumsum_seq(lf, S):
    T = lf.shape[0]
    blk = pl.BlockSpec((S, LANES), lambda b: (b, 0))
    return pl.pallas_call(
        _cumsum_kernel, grid=(T // S,), in_specs=[blk], out_specs=blk,
        out_shape=jax.ShapeDtypeStruct((T, LANES), F32),
        compiler_params=_params("parallel"), name="forget_cumsum",
    )(lf)


def _flash_kernel(*refs, fox, scale, t, n_tiles, S):
    if fox:
        q_ref, k_ref, v_ref, cc_ref, cr_ref, o_ref, kx_ref, vt_ref, sb0_ref, sb1_ref = refs
    else:
        q_ref, kv_ref, kr_ref, o_ref, kx_ref, vt_ref, sb0_ref, sb1_ref = refs
    sb_refs = (sb0_ref, sb1_ref)
    hp = pl.program_id(1)
    lane = _lane_iota()
    lo = lane < D_FOX
    hi = lane >= D_FOX
    exp2_scale = scale * 1.4426950408889634

    ones = jnp.ones((VT_ROWS - D_FOX, S), BF16)
    if fox:
        vt = jnp.transpose(v_ref[0].astype(F32)).astype(BF16)
    for hh in range(2):
        if fox:
            col = jnp.sum(jnp.where(lane == 2 * hp + hh, cc_ref[0], 0.0), axis=-1, keepdims=True)
            kx_ref[hh] = jnp.broadcast_to(col, (S, LANES))
            vt_ref[hh, :D_FOX, :] = vt[hh * D_FOX:(hh + 1) * D_FOX]
        else:
            kvh = kv_ref[0, :, hh * LANES:(hh + 1) * LANES]
            kx_ref[hh] = jnp.where(lo, kvh, kr_ref[0])
            vt_ref[hh, :D_V_MLA, :] = jnp.transpose(kvh.astype(F32))[D_NOPE:].astype(BF16)
        vt_ref[hh, D_FOX:, :] = ones

    def body(n):
        qrows = slice(n * t, (n + 1) * t)
        half = t // 2
        blocks = [(c * t, t, 0, False) for c in range(n)] + [(n * t, half, 0, True), (n * t + half, half, half, True)]

        def on_lanes(full, q0, part, op):
            if q0 == 0:
                return op(full, part)
            return jnp.concatenate([full[:, :q0], op(full[:, q0:], part)], axis=1)

        outs = []
        for hh in range(2):
            if fox:
                q2 = q_ref[0, qrows, :] * jnp.asarray(scale, BF16)
                qh = jnp.where(lo if hh == 0 else hi, q2, jnp.zeros_like(q2))
                cq = cr_ref[0, pl.ds(2 * hp + hh, 1), qrows]
            else:
                qh = q_ref[0, qrows, hh * LANES:(hh + 1) * LANES]
            m8 = jnp.full((SUBLANES, t), -jnp.inf, F32)
            for k0, nk, q0, masked in blocks:
                nq = t - q0
                kh = k_ref[0, k0:k0 + nk, :] if fox else kx_ref[hh, k0:k0 + nk, :]
                s = lax.dot_general(kh, qh[q0:], _NT, preferred_element_type=F32)
                if fox:
                    s = s + cq[:, q0:] - jnp.concatenate([kx_ref[hh, k0:k0 + nk, :]] * (nq // LANES), axis=1)
                if masked:
                    k_idx = k0 + lax.broadcasted_iota(jnp.int32, (nk, 1), 0)
                    q_idx = n * t + q0 + lax.broadcasted_iota(jnp.int32, (1, nq), 1)
                    vis = (k_idx <= q_idx) if fox else ((k_idx >> CHUNK_SHIFT) <= (q_idx >> CHUNK_SHIFT))
                    s = jnp.where(vis, s, NEG_INF)
                sb_refs[hh][k0:k0 + nk, q0:] = s
                m8 = on_lanes(m8, q0, jnp.max(s.reshape(nk // SUBLANES, SUBLANES, nq), axis=0), jnp.maximum)
            m = jnp.max(m8, axis=0, keepdims=True)
            acc = jnp.zeros((VT_ROWS, t), F32)
            for k0, nk, q0, masked in blocks:
                x = sb_refs[hh][k0:k0 + nk, q0:] - m[:, q0:]
                p = jnp.exp(x) if fox else jnp.exp2(x * exp2_scale)
                pv = jnp.dot(vt_ref[hh, :, k0:k0 + nk], p.astype(BF16), preferred_element_type=F32)
                acc = on_lanes(acc, q0, pv, jnp.add)
            outs.append(acc[:D_FOX] / acc[D_FOX:D_FOX + 1])
        o_ref[0, qrows, :] = jnp.transpose(jnp.concatenate(outs, axis=0)).astype(BF16)

    for n in range(n_tiles):
        body(n)


def _flash(args, *, fox, B, S, scale):
    t = min(T_ATT, S)
    n = S // t
    npairs = (H_FOX if fox else H_MLA) // 2
    args = tuple(a.reshape(B, S, a.shape[-1]) if a.ndim == 2 else a for a in args)
    seq = lambda b, hp: (b, 0, hp)
    whole = lambda b, hp: (b, 0, 0)
    scratch = [pltpu.VMEM((2, S, LANES), F32 if fox else BF16), pltpu.VMEM((2, VT_ROWS, S), BF16),
               pltpu.VMEM((S, t), F32), pltpu.VMEM((S, t), F32)]
    if fox:
        in_specs = [pl.BlockSpec((1, S, LANES), seq), pl.BlockSpec((1, S, LANES), seq),
                    pl.BlockSpec((1, S, LANES), seq), pl.BlockSpec((1, S, LANES), whole),
                    pl.BlockSpec((1, H_FOX, S), whole)]
    else:
        in_specs = [pl.BlockSpec((1, S, 2 * LANES), seq), pl.BlockSpec((1, S, 2 * LANES), seq),
                    pl.BlockSpec((1, S, LANES), whole)]
    out = pl.pallas_call(
        functools.partial(_flash_kernel, fox=fox, scale=scale, t=t, n_tiles=n, S=S),
        grid=(B, npairs),
        in_specs=in_specs,
        out_specs=pl.BlockSpec((1, S, LANES), seq),
        out_shape=jax.ShapeDtypeStruct((B, S, npairs * LANES), BF16),
        scratch_shapes=scratch,
        compiler_params=_params("parallel", "parallel"),
        name="fox_attention" if fox else "mla_attention",
    )(*args)
    return out.reshape(B * S, npairs * LANES)


def _dsa_kernel(qi_ref, qb_ref, wit_ref, ki_ref, kb_ref, va_ref, o_ref,
                qim_ref, qbm_ref, sc_ref, sb_ref, *, nk, i0, ksel):
    i = i0 + pl.program_id(1)
    chunks = [slice(k0, min(k0 + KC_DSA, nk)) for k0 in range(0, nk, KC_DSA)]
    lane = _lane_iota()
    lo = lane < D_DSA
    hi = lane >= D_DSA
    q_chunk = (i * Q_DSA + lane) >> CHUNK_SHIFT
    select = nk > ksel

    def key_idx(rows):
        return rows.start + lax.broadcasted_iota(jnp.int32, (rows.stop - rows.start, 1), 0)

    def admissible(rows):
        return (key_idx(rows) >> CHUNK_SHIFT) <= q_chunk

    def fold_max(s):
        return jnp.max(s.reshape(s.shape[0] // SUBLANES, SUBLANES, s.shape[1]), axis=0)

    for h in range(H_DSA):
        blk = qb_ref[:, (h // 2) * LANES:(h // 2 + 1) * LANES] * jnp.asarray(D_DSA ** -0.5, BF16)
        qbm_ref[h * Q_DSA:(h + 1) * Q_DSA, :] = jnp.where(lo if h % 2 == 0 else hi, blk, jnp.zeros_like(blk))

    if select:
        for h in range(H_IDX):
            blk = qi_ref[:, (h // 2) * LANES:(h // 2 + 1) * LANES]
            qim_ref[h * Q_DSA:(h + 1) * Q_DSA, :] = jnp.where(lo if h % 2 == 0 else hi, blk, jnp.zeros_like(blk))
        w_flat = jnp.concatenate([wit_ref[h:h + 1, :] for h in range(H_IDX)], axis=1)
        for rows in chunks:
            r = lax.dot_general(ki_ref[0, rows, :], qim_ref[...], _NT, preferred_element_type=F32)
            r = jnp.maximum(r, 0.0) * w_flat
            acc = r[:, :LANES]
            for h in range(1, H_IDX):
                acc = acc + r[:, h * LANES:(h + 1) * LANES]
            sc_ref[rows, :] = jnp.where(admissible(rows), acc, NEG_INF)

        def count(mask):
            part = jnp.sum(mask.astype(F32).reshape(nk // 32, 32, LANES), axis=0)
            return jnp.sum(part, axis=0, keepdims=True)

        def as_float(key):
            return lax.bitcast_convert_type(jnp.where(key >= 0, key, key ^ 0x7FFFFFFF), F32)

        def all_bits():
            def bit_body(r, ans):
                cand = ans + lax.shift_left(jnp.int32(1), 31 - r)
                cnt = count(sc_ref[...] >= as_float(cand))
                return jnp.where(cnt >= ksel, cand, ans)
            return lax.fori_loop(0, 32, bit_body, jnp.full((1, LANES), jnp.iinfo(jnp.int32).min, jnp.int32))

        s_all = sc_ref[...]
        top = jnp.max(jnp.max(s_all.reshape(nk // SUBLANES, SUBLANES, LANES), axis=0), axis=0, keepdims=True)
        low = top * (2.0 ** -NARROW_EXP)
        narrow = (jnp.min(count(s_all >= low)) >= ksel) & (jnp.min(top) > NARROW_MIN_TOP)

        def in_bracket():
            def body(r, lo_hi):
                lo, hi = lo_hi
                mid = lo + lax.shift_right_logical(hi - lo, 1)
                up = count(sc_ref[...] >= as_float(mid)) >= ksel
                return jnp.where(up, mid, lo), jnp.where(up, hi, mid)
            lo0 = lax.bitcast_convert_type(low, jnp.int32)
            hi0 = lax.bitcast_convert_type(top, jnp.int32) + 1
            return lax.fori_loop(0, NARROW_ROUNDS, body, (lo0, hi0))[0]

        ans = lax.cond(narrow, in_bracket, all_bits)
        t = as_float(ans)
        need = ksel - count(s_all > t)
        n_ge = count(s_all >= t)
        k_all = lax.broadcasted_iota(jnp.int32, (nk, 1), 0)
        idx_bits = nk.bit_length()

        def tie_bound():
            def bit_body2(r, aj):
                cand = aj + lax.shift_left(jnp.int32(1), idx_bits - 1 - r)
                g = count((sc_ref[...] == t) & (k_all < cand))
                return jnp.where(g <= need, cand, aj)
            return lax.fori_loop(0, idx_bits, bit_body2, jnp.zeros((1, LANES), jnp.int32))

        jmax = lax.cond(jnp.max(n_ge) > ksel, tie_bound,
                        lambda: jnp.full((1, LANES), nk, jnp.int32))
        for rows in chunks:
            s = sc_ref[rows, :]
            sel = (s > t) | ((s == t) & (key_idx(rows) < jmax))
            sc_ref[rows, :] = jnp.where(sel & admissible(rows), 0.0, NEG_INF)
    else:
        for rows in chunks:
            sc_ref[rows, :] = jnp.where(admissible(rows), 0.0, NEG_INF)

    m8 = jnp.full((SUBLANES, H_DSA * LANES), -jnp.inf, F32)
    for rows in chunks:
        s = lax.dot_general(kb_ref[0, rows, :], qbm_ref[...], _NT, preferred_element_type=F32)
        s = s + jnp.concatenate([sc_ref[rows, :]] * H_DSA, axis=1)
        sb_ref[rows, :] = s
        m8 = jnp.maximum(m8, fold_max(s))
    m = jnp.max(m8, axis=0, keepdims=True)
    acc = jnp.zeros((H_DSA * Q_DSA, LANES), F32)
    for rows in chunks:
        p = jnp.exp(sb_ref[rows, :] - m).astype(BF16)
        acc = acc + lax.dot_general(p, va_ref[0, rows, :], _TN, preferred_element_type=F32)

    for hp in range(H_DSA // 2):
        a = acc[(2 * hp) * Q_DSA:(2 * hp + 1) * Q_DSA]
        b = acc[(2 * hp + 1) * Q_DSA:(2 * hp + 2) * Q_DSA]
        ra = a / pltpu.roll(a, D_DSA, 1)
        rb = b / pltpu.roll(b, D_DSA, 1)
        o_ref[:, hp * LANES:(hp + 1) * LANES] = jnp.where(lo, ra, pltpu.roll(rb, D_DSA, 1)).astype(BF16)


def _dsa(qi, qb, wit, ki2, kb2, va, *, B, S, ksel):
    nb = S // Q_DSA
    per = KC_DSA // Q_DSA
    seq3 = lambda a: a.reshape(B, S, LANES)
    ki3, kb3, va3 = seq3(ki2), seq3(kb2), seq3(va)
    outs = []
    for g in range(nb // per):
        nk, i0 = (g + 1) * per * Q_DSA, g * per
        qrow = lambda n, i0=i0: pl.BlockSpec((Q_DSA, n), lambda b, j: (b * nb + i0 + j, 0))
        seq = pl.BlockSpec((1, nk, LANES), lambda b, j: (b, 0, 0))
        o = pl.pallas_call(
            functools.partial(_dsa_kernel, nk=nk, i0=i0, ksel=ksel),
            grid=(B, per),
            in_specs=[qrow(H_IDX * D_IDX), qrow(H_DSA * D_DSA),
                      pl.BlockSpec((SUBLANES, Q_DSA), lambda b, j, i0=i0: (0, b * nb + i0 + j)),
                      seq, seq, seq],
            out_specs=pl.BlockSpec((Q_DSA, H_DSA * D_DSA), lambda b, j: (b * per + j, 0)),
            out_shape=jax.ShapeDtypeStruct((B * per * Q_DSA, H_DSA * D_DSA), BF16),
            scratch_shapes=[pltpu.VMEM((H_IDX * Q_DSA, LANES), BF16), pltpu.VMEM((H_DSA * Q_DSA, LANES), BF16),
                            pltpu.VMEM((nk, LANES), F32), pltpu.VMEM((nk, H_DSA * LANES), F32)],
            compiler_params=_params("parallel", "arbitrary"),
            name="dsa_attention_%d" % nk,
        )(qi, qb, wit, ki3, kb3, va3)
        outs.append(o.reshape(B, per * Q_DSA, H_DSA * D_DSA))
    return jnp.concatenate(outs, axis=1).reshape(B * S, H_DSA * D_DSA)


_R_SUB = N_GROUPS
_PAIRS = [(a, b) for a in range(EXPERTS_PER_GROUP) for b in range(a + 1, EXPERTS_PER_GROUP)]
N_CLASSES = N_GROUPS * len(_PAIRS)


def _out_ln_kernel(*refs, n_in):
    o_refs, w_refs = refs[:n_in], refs[n_in:2 * n_in]
    x_ref, g_ref, b_ref, wrt_ref, brt_ref, tri_ref, out_ref, route_ref, cnt_ref, run_ref = refs[2 * n_in:]
    y = jnp.dot(o_refs[0][...], w_refs[0][...], preferred_element_type=F32)
    for o_r, w_r in zip(o_refs[1:], w_refs[1:]):
        y = y + jnp.dot(o_r[...], w_r[...], preferred_element_type=F32)
    x_new = _layer_norm(DN_ALPHA * x_ref[...] + y, g_ref[...], b_ref[...])
    out_ref[...] = x_new
    _route_rows(x_new, wrt_ref, brt_ref, tri_ref, route_ref, cnt_ref, run_ref)


def _out_ln(os_, ws, x2, g, b, wr, br):
    T = x2.shape[0]
    tm = min(TM_PROJ, T)
    row = lambda c: pl.BlockSpec((tm, c), lambda i: (i, 0))
    tri = jnp.triu(jnp.ones((tm, tm), BF16), 1)
    wrt, brt = wr.T, br.T
    return pl.pallas_call(
        functools.partial(_out_ln_kernel, n_in=len(os_)),
        grid=(T // tm,),
        in_specs=[row(o.shape[1]) for o in os_] + [_full_spec(w.shape) for w in ws]
                 + [row(D_MODEL), _full_spec(g.shape), _full_spec(b.shape),
                    _full_spec(wrt.shape), _full_spec(brt.shape), _full_spec(tri.shape)],
        out_specs=[row(D_MODEL), pl.BlockSpec((1, SUBLANES, tm), lambda i: (i, 0, 0)), _full_spec((LANES, LANES))],
        out_shape=[jax.ShapeDtypeStruct((T, D_MODEL), F32), jax.ShapeDtypeStruct((T // tm, SUBLANES, tm), F32),
                   jax.ShapeDtypeStruct((LANES, LANES), F32)],
        scratch_shapes=[pltpu.VMEM((LANES, 1), F32)],
        compiler_params=_params("arbitrary"),
        name="out_proj_ln",
    )(*os_, *ws, x2, g, b, wrt, brt, tri)


def _route_rows(x, wrt_ref, brt_ref, tri_ref, route_ref, cnt_ref, run_ref):
    @pl.when(pl.program_id(0) == 0)
    def _():
        run_ref[...] = jnp.zeros(run_ref.shape, F32)

    tm = x.shape[0]
    lg = lax.dot_general(wrt_ref[...], x.astype(BF16), _NT, preferred_element_type=F32) + brt_ref[...]
    r = lax.broadcasted_iota(jnp.int32, (LANES, 1), 0)
    rf = r.astype(F32)
    far = float(LANES)
    grp = jnp.where(r < N_GROUPS, lg, -jnp.inf)
    mg = jnp.max(grp, axis=0, keepdims=True)
    g_star = jnp.min(jnp.where(grp == mg, rf, far), axis=0, keepdims=True)
    row_grp = ((r - _R_SUB) >> 2).astype(F32)
    in_grp = (row_grp == g_star) & (r >= _R_SUB) & (r < _R_SUB + N_EXPERTS)
    l1 = jnp.where(in_grp, lg, -jnp.inf)
    v1 = jnp.max(l1, axis=0, keepdims=True)
    j1 = jnp.min(jnp.where(l1 == v1, rf, far), axis=0, keepdims=True)
    l2 = jnp.where(in_grp & (rf != j1), lg, -jnp.inf)
    v2 = jnp.max(l2, axis=0, keepdims=True)
    j2 = jnp.min(jnp.where(l2 == v2, rf, far), axis=0, keepdims=True)
    base = _R_SUB + EXPERTS_PER_GROUP * g_star
    ja = jnp.minimum(j1, j2) - base
    jb = jnp.maximum(j1, j2) - base
    cls = g_star * len(_PAIRS) + ja * (7.0 - ja) * 0.5 + (jb - ja - 1.0)
    onehot = rf == cls
    ones = jnp.where(onehot, 1.0, 0.0)
    before = jnp.dot(ones.astype(BF16), tri_ref[...], preferred_element_type=F32)
    rank = jnp.sum(jnp.where(onehot, before + run_ref[...], 0.0), axis=0, keepdims=True)
    row8 = lax.broadcasted_iota(jnp.int32, (SUBLANES, 1), 0)
    route_ref[0] = jnp.where(row8 == 0, cls, jnp.where(row8 == 1, rank, 0.0))
    run_ref[...] += jnp.sum(ones, axis=1, keepdims=True)
    cnt_ref[...] = jnp.broadcast_to(run_ref[...], cnt_ref.shape)


def _row_copy(src_ref, dst_ref, src_row, dst_row, sem):
    return pltpu.make_async_copy(src_ref.at[pl.ds(src_row, 1), :], dst_ref.at[pl.ds(dst_row, 1), :], sem)


def _dispatch_kernel(pos_ref, x_ref, init_ref, xs_ref, sem):
    del init_ref
    tm = x_ref.shape[0]

    def start(g, carry):
        base = pl.multiple_of(g * SUBLANES, SUBLANES)
        for j in range(SUBLANES):
            _row_copy(x_ref, xs_ref, base + j, pos_ref[0, 0, base + j], sem).start()
        return carry
    lax.fori_loop(0, tm // SUBLANES, start, 0)

    def wait(t, carry):
        _row_copy(x_ref, xs_ref, 0, 0, sem).wait()
        return carry
    lax.fori_loop(0, tm, wait, 0, unroll=8)


def _collect_kernel(pos_ref, ys_ref, out_ref, sem):
    tm = out_ref.shape[0]

    def start(g, carry):
        base = pl.multiple_of(g * SUBLANES, SUBLANES)
        for j in range(SUBLANES):
            _row_copy(ys_ref, out_ref, pos_ref[0, 0, base + j], base + j, sem).start()
        return carry
    lax.fori_loop(0, tm // SUBLANES, start, 0)

    def wait(t, carry):
        _row_copy(ys_ref, out_ref, 0, 0, sem).wait()
        return carry
    lax.fori_loop(0, tm, wait, 0, unroll=8)


def _dispatch(x2, pos3, n_rows):
    T = x2.shape[0]
    tm = pos3.shape[-1]
    return pl.pallas_call(
        _dispatch_kernel,
        grid=(T // tm,),
        in_specs=[pl.BlockSpec((1, 1, tm), lambda i: (i, 0, 0), memory_space=pltpu.SMEM),
                  pl.BlockSpec((tm, D_MODEL), lambda i: (i, 0)),
                  pl.BlockSpec(memory_space=pl.ANY)],
        out_specs=pl.BlockSpec(memory_space=pl.ANY),
        out_shape=jax.ShapeDtypeStruct((n_rows, D_MODEL), F32),
        scratch_shapes=[pltpu.SemaphoreType.DMA(())],
        input_output_aliases={2: 0},
        compiler_params=_params("arbitrary"),
        name="moe_dispatch",
    )(pos3, x2, jnp.zeros((n_rows, D_MODEL), F32))


def _collect(ys, pos3):
    T = pos3.shape[0] * pos3.shape[-1]
    tm = pos3.shape[-1]
    return pl.pallas_call(
        _collect_kernel,
        grid=(T // tm,),
        in_specs=[pl.BlockSpec((1, 1, tm), lambda i: (i, 0, 0), memory_space=pltpu.SMEM),
                  pl.BlockSpec(memory_space=pl.ANY)],
        out_specs=pl.BlockSpec((tm, D_MODEL), lambda i: (i, 0)),
        out_shape=jax.ShapeDtypeStruct((T, D_MODEL), F32),
        scratch_shapes=[pltpu.SemaphoreType.DMA(())],
        compiler_params=_params("arbitrary"),
        name="moe_collect",
    )(pos3, ys)


def _expert_kernel(tg_ref, ta_ref, tb_ref, x_ref, wr_ref, br_ref,
                   wga_ref, wua_ref, wda_ref, wgb_ref, wub_ref, wdb_ref, g_ref, b_ref, out_ref, z_ref):
    step = pl.program_id(0)
    r = jnp.minimum(step, pl.num_programs(0) - 2)

    @pl.when(step == 0)
    def _():
        z_ref[...] = jnp.zeros(z_ref.shape, F32)

    out_ref[...] = _layer_norm(z_ref[...], g_ref[...], b_ref[...])
    x = x_ref[...]
    xb = x.astype(BF16)
    lg = jnp.dot(xb, wr_ref[...], preferred_element_type=F32) + br_ref[...]
    lane = _lane_iota()
    is_grp = lane < N_GROUPS
    mg = jnp.max(jnp.where(is_grp, lg, -jnp.inf), axis=-1, keepdims=True)
    den = jnp.sum(jnp.where(is_grp, jnp.exp(lg - mg), 0.0), axis=-1, keepdims=True)
    pick = lambda l: jnp.sum(jnp.where(lane == l, lg, 0.0), axis=-1, keepdims=True)
    p_top = jnp.exp(pick(tg_ref[r]) - mg) / den
    va, vb = pick(_R_SUB + ta_ref[r]), pick(_R_SUB + tb_ref[r])
    e = jnp.exp(jnp.minimum(va, vb) - jnp.maximum(va, vb))
    w_top, w_oth = 1.0 / (1.0 + e), e / (1.0 + e)
    a_top = va >= vb
    gates = (jnp.where(a_top, w_top, w_oth) * p_top, jnp.where(a_top, w_oth, w_top) * p_top)
    y = jnp.zeros(x.shape, F32)
    for gate, wg, wu, wd in ((gates[0], wga_ref, wua_ref, wda_ref), (gates[1], wgb_ref, wub_ref, wdb_ref)):
        a = jnp.dot(xb, wg[0], preferred_element_type=F32)
        u = jnp.dot(xb, wu[0], preferred_element_type=F32)
        h = (a * (1.0 / (1.0 + jnp.exp(-a))) * u).astype(BF16)
        y = y + gate * jnp.dot(h, wd[0], preferred_element_type=F32)
    z_ref[...] = DN_ALPHA * x + y


def _experts(xs, tile_g, tile_a, tile_b, wr, br, wg, wu, wd, g, b):
    n_tiles = tile_g.shape[0]
    te = xs.shape[0] // n_tiles
    tile = lambda r: jnp.minimum(r, n_tiles - 1)
    x_row = pl.BlockSpec((te, D_MODEL), lambda r, *_: (tile(r), 0))
    out_row = pl.BlockSpec((te, D_MODEL), lambda r, *_: (jnp.maximum(r - 1, 0), 0))
    full = lambda shape: pl.BlockSpec(shape, lambda r, *_: (0,) * len(shape))
    up_a = pl.BlockSpec((1, D_MODEL, D_EXPERT), lambda r, tg, ta, tb: (ta[tile(r)], 0, 0))
    up_b = pl.BlockSpec((1, D_MODEL, D_EXPERT), lambda r, tg, ta, tb: (tb[tile(r)], 0, 0))
    dn_a = pl.BlockSpec((1, D_EXPERT, D_MODEL), lambda r, tg, ta, tb: (ta[tile(r)], 0, 0))
    dn_b = pl.BlockSpec((1, D_EXPERT, D_MODEL), lambda r, tg, ta, tb: (tb[tile(r)], 0, 0))
    return pl.pallas_call(
        _expert_kernel,
        grid_spec=pltpu.PrefetchScalarGridSpec(
            num_scalar_prefetch=3, grid=(n_tiles + 1,),
            in_specs=[x_row, full(wr.shape), full(br.shape), up_a, up_a, dn_a, up_b, up_b, dn_b,
                      full(g.shape), full(b.shape)],
            out_specs=out_row,
            scratch_shapes=[pltpu.VMEM((te, D_MODEL), F32)]),
        out_shape=jax.ShapeDtypeStruct(xs.shape, F32),
        compiler_params=_params("arbitrary"),
        name="moe_experts",
    )(tile_g, tile_a, tile_b, xs, wr, br, wg, wu, wd, wg, wu, wd, g, b)


def _slot_kernel(route_ref, start_ref, pos_ref):
    rf = lax.broadcasted_iota(jnp.int32, (LANES, 1), 0).astype(F32)
    for j in range(route_ref.shape[0]):
        cls, rank = route_ref[j, 0:1, :], route_ref[j, 1:2, :]
        start = jnp.sum(jnp.where(rf == cls, start_ref[...], 0.0), axis=0, keepdims=True)
        pos_ref[j] = (start + rank).astype(jnp.int32)


def _slots(route, starts):
    n, _, tm = route.shape
    nb = min(n, 16)
    slots = pl.pallas_call(
        _slot_kernel,
        grid=(n // nb,),
        in_specs=[pl.BlockSpec((nb, SUBLANES, tm), lambda i: (i, 0, 0)), _full_spec(starts.shape)],
        out_specs=pl.BlockSpec((nb, 1, tm), lambda i: (i, 0, 0)),
        out_shape=jax.ShapeDtypeStruct((n, 1, tm), jnp.int32),
        compiler_params=_params("parallel"),
        name="moe_slots",
    )(route, starts)
    move = min(TM_MOVE, n * tm)
    return slots.reshape(n * tm // move, 1, move)


def _moe_ln(x2, route, counts, wr, br, wg, wu, wd, g, b):
    T = x2.shape[0]
    te = TE_MOE
    n_tiles = (T + N_CLASSES * (te - 1) + te - 1) // te
    cnt = counts[:N_CLASSES, 0].astype(jnp.int32)
    padded = (cnt + te - 1) // te * te
    ends = jnp.cumsum(padded)
    starts = jnp.pad((ends - padded).astype(F32), (0, LANES - N_CLASSES))[:, None]
    pos3 = _slots(route, starts)
    tile_start = jnp.arange(n_tiles, dtype=jnp.int32) * te
    tile_cls = jnp.minimum(jnp.sum(tile_start[:, None] >= ends[None, :], axis=1), N_CLASSES - 1).astype(jnp.int32)
    pair = jnp.asarray(_PAIRS, jnp.int32)[tile_cls % len(_PAIRS)]
    tile_g = tile_cls // len(_PAIRS)
    tile_a = tile_g * EXPERTS_PER_GROUP + pair[:, 0]
    tile_b = tile_g * EXPERTS_PER_GROUP + pair[:, 1]
    xs = _dispatch(x2, pos3, n_tiles * te)
    ys = _experts(xs, tile_g, tile_a, tile_b, wr, br, wg, wu, wd, g, b)
    return _collect(ys, pos3)


def _router_weights(w_grp, b_grp, w_sub, b_sub):
    pad = LANES - N_GROUPS - N_EXPERTS
    wr = jnp.concatenate([w_grp, w_sub, jnp.zeros((D_MODEL, pad), w_grp.dtype)], axis=1).astype(BF16)
    br = jnp.concatenate([b_grp, b_sub, jnp.zeros((pad,), b_grp.dtype)])[None].astype(F32)
    return wr, br


def kernel(x, positions, ev_w_in, ev_g_q, ev_g_kv, ev_w_uq, ev_w_ukv, ev_w_o, od_w_in, od_b_f, od_w_o,
           moe_w_grp, moe_b_grp, moe_w_sub, moe_b_sub, moe_w_gate, moe_w_up, moe_w_down,
           ln1_g, ln1_b, ln2_g, ln2_b):
    B, S, D = x.shape
    T = B * S
    ksel = min(TOPK_MAX, S // 4)
    x2 = x.reshape(T, D)
    tabs = _rope_tables(positions.reshape(T, 1).astype(F32))
    row = lambda v: v[None].astype(F32)
    for layer in range(DEPTH):
        j = layer // 2
        wr, br = _router_weights(moe_w_grp[layer], moe_b_grp[layer], moe_w_sub[layer], moe_b_sub[layer])
        ln1 = (row(ln1_g[layer]), row(ln1_b[layer]), wr, br)
        if layer % 2 == 0:
            w1, wuq, wukv = _even_weights(ev_w_in[j], ev_w_uq[j], ev_w_ukv[j])
            qa, kv, kr, qb, kb2, va, qi, ki2, wi = _even_proj(
                x2, w1, row(ev_g_q[j]), row(ev_g_kv[j]), wuq, wukv, tabs)
            o_a = _flash((qa, kv, kr), fox=False, B=B, S=S, scale=(D_NOPE + D_ROPE) ** -0.5)
            wit = jnp.pad(wi[:, :H_IDX].T, ((0, SUBLANES - H_IDX), (0, 0)))
            o_b = _dsa(qi, qb, wit, ki2, kb2, va, B=B, S=S, ksel=ksel)
            wo = ev_w_o[j].astype(BF16)
            n_a = H_MLA * D_V_MLA
            x2, route, counts = _out_ln((o_a, o_b), (wo[:n_a], wo[n_a:]), x2, *ln1)
        else:
            n = H_FOX * D_FOX
            w = jnp.pad(od_w_in[j], ((0, 0), (0, LANES - H_FOX))).astype(BF16)
            bf = jnp.pad(od_b_f[j], (0, LANES - H_FOX))[None].astype(F32)
            q, k, v, lf = _odd_proj(x2, w, bf)
            c = _cumsum_seq(lf, S)
            c_rows = c.reshape(B, S, LANES)[:, :, :H_FOX].swapaxes(1, 2)
            o = _flash((q, k, v, c, c_rows), fox=True, B=B, S=S, scale=D_FOX ** -0.5)
            x2, route, counts = _out_ln((o,), (od_w_o[j].astype(BF16),), x2, *ln1)
        x2 = _moe_ln(x2, route, counts, wr, br, moe_w_gate[layer].astype(BF16), moe_w_up[layer].astype(BF16),
                     moe_w_down[layer].astype(BF16), row(ln2_g[layer]), row(ln2_b[layer]))
    return x2.reshape(B, S, D)
```

```python
import functools

import jax
import jax.numpy as jnp
from jax import lax
from jax.experimental import pallas as pl
from jax.experimental.pallas import tpu as pltpu

D_MODEL = 1024
DEPTH = 4
CHUNK = 64
CHUNK_SHIFT = 6
ROPE_THETA = 10000.0
NEG_INF = -1e30
LN_EPS = 1e-5
RMS_EPS = 1e-6
H_MLA, D_NOPE, D_ROPE, D_V_MLA, Q_LORA, KV_LORA = 8, 64, 32, 64, 384, 256
H_DSA, D_DSA, H_IDX, D_IDX, TOPK_MAX = 8, 64, 4, 64, 256
H_FOX, D_FOX = 16, 64
N_GROUPS, EXPERTS_PER_GROUP, D_EXPERT = 4, 4, 512
N_EXPERTS = N_GROUPS * EXPERTS_PER_GROUP
DN_ALPHA = (2 * DEPTH) ** 0.25
EV_SPLITS = (Q_LORA, KV_LORA, D_ROPE, H_DSA * D_DSA, D_DSA, D_DSA, H_IDX * D_IDX, D_IDX, H_IDX)

LANES = 128
SUBLANES = 8
VMEM_LIMIT_BYTES = 56 * 1024 * 1024

TM_PROJ = 1024
TM_MOVE = 2048
TE_MOE = 256
T_ATT = 512
VT_ROWS = 80
Q_DSA = 128
KC_DSA = 256

F32 = jnp.float32
BF16 = jnp.bfloat16
_NT = (((1,), (1,)), ((), ()))
_TN = (((0,), (0,)), ((), ()))


def _params(*sem):
    return pltpu.CompilerParams(dimension_semantics=sem, vmem_limit_bytes=VMEM_LIMIT_BYTES)


def _lane_iota(shape=(1, LANES)):
    return lax.broadcasted_iota(jnp.int32, shape, len(shape) - 1)


def _full_spec(shape):
    return pl.BlockSpec(shape, lambda *_: (0,) * len(shape))


def _rope_table_kernel(pos_ref, inv_ref, sg32_ref, sg64_ref, c32_ref, s32_ref, c64_ref, s64_ref):
    ang = pos_ref[...] * inv_ref[...]
    c, s = jnp.cos(ang), jnp.sin(ang)
    lane = _lane_iota()
    h64, h32 = D_DSA // 2, D_ROPE // 2
    in64 = lane < h64
    in32 = (lane >= h64) & (lane < h64 + h32)
    rope32 = (lane >= D_NOPE) & (lane < D_NOPE + D_ROPE)

    def tile64(x):
        a = jnp.where(in64, x, 0.0)
        return a + pltpu.roll(a, h64, 1) + pltpu.roll(a, 2 * h64, 1) + pltpu.roll(a, 3 * h64, 1)

    def place32(x):
        b = jnp.where(in32, x, 0.0)
        return pltpu.roll(b, D_NOPE - h64, 1) + pltpu.roll(b, D_NOPE - h64 + h32, 1)

    c64_ref[...] = tile64(c)
    s64_ref[...] = tile64(s) * sg64_ref[...]
    c32_ref[...] = jnp.where(rope32, place32(c), 1.0)
    s32_ref[...] = place32(s) * sg32_ref[...]


def _rope_tables(posf):
    T = posf.shape[0]
    tm = min(T, 2048)
    lane = jnp.arange(LANES)
    inv16 = jnp.power(ROPE_THETA, -jnp.arange(0, D_ROPE, 2, dtype=F32) / D_ROPE)
    inv32 = jnp.power(ROPE_THETA, -jnp.arange(0, D_DSA, 2, dtype=F32) / D_DSA)
    in32 = (lane >= D_NOPE) & (lane < D_NOPE + D_ROPE)
    g32 = jnp.where(in32, jnp.where(lane < D_NOPE + D_ROPE // 2, -1.0, 1.0), 0.0).astype(F32)[None]
    g64 = jnp.where((lane % D_DSA) < D_DSA // 2, -1.0, 1.0).astype(F32)[None]
    inv = jnp.concatenate([inv32, inv16, jnp.zeros((LANES - D_DSA // 2 - D_ROPE // 2,), F32)])[None]
    row = pl.BlockSpec((tm, LANES), lambda i: (i, 0))
    tab = jax.ShapeDtypeStruct((T, LANES), F32)
    return pl.pallas_call(
        _rope_table_kernel,
        grid=(T // tm,),
        in_specs=[pl.BlockSpec((tm, 1), lambda i: (i, 0))] + [_full_spec((1, LANES))] * 3,
        out_specs=[row] * 4,
        out_shape=[tab] * 4,
        compiler_params=_params("parallel"),
        name="rope_tables",
    )(posf, inv, g32, g64)


def _rope_block(y, c, s, half, second_half):
    partner = jnp.where(second_half, pltpu.roll(y, half, 1), pltpu.roll(y, LANES - half, 1))
    return y * c + partner * s


def _rms(x, g):
    return x * lax.rsqrt(jnp.mean(x * x, axis=-1, keepdims=True) + RMS_EPS) * g


def _layer_norm(z, g, b):
    mu = jnp.mean(z, axis=-1, keepdims=True)
    d = z - mu
    var = jnp.mean(d * d, axis=-1, keepdims=True)
    return d * lax.rsqrt(var + LN_EPS) * g + b


_E_CQ, _E_CKV, _E_KR, _E_QB, _E_KB, _E_VB, _E_QI, _E_KI, _E_WI, _E_END = (
    0, 384, 640, 768, 1280, 1408, 1536, 1792, 1920, 2048)


def _even_proj_kernel(x_ref, w1_ref, gq_ref, gkv_ref, wuq_ref, wukv_ref,
                      c32_ref, s32_ref, c64_ref, s64_ref,
                      qa_ref, kv_ref, kr_ref, qb_ref, kb_ref, va_ref, qi_ref, ki_ref, wi_ref):
    xb = x_ref[...].astype(BF16)
    lane = _lane_iota()
    sec32 = (lane >= D_NOPE + D_ROPE // 2) & (lane < D_NOPE + D_ROPE)
    sec64 = (lane % D_DSA) >= D_DSA // 2
    c32, s32, c64, s64 = c32_ref[...], s32_ref[...], c64_ref[...], s64_ref[...]

    def seg(a, b):
        return jnp.dot(xb, w1_ref[:, a:b], preferred_element_type=F32)

    def rope64_store(h, out_ref):
        for j in range(h.shape[1] // LANES):
            blk = h[:, j * LANES:(j + 1) * LANES]
            out_ref[:, j * LANES:(j + 1) * LANES] = _rope_block(
                blk, c64, s64, D_DSA // 2, sec64).astype(out_ref.dtype)

    cq = _rms(seg(_E_CQ, _E_CKV), gq_ref[...]).astype(BF16)
    q = jnp.dot(cq, wuq_ref[...], preferred_element_type=F32)
    for h in range(H_MLA):
        blk = q[:, h * LANES:(h + 1) * LANES]
        qa_ref[:, h * LANES:(h + 1) * LANES] = _rope_block(
            blk, c32, s32, D_ROPE // 2, sec32).astype(BF16)
    ckv = _rms(seg(_E_CKV, _E_KR), gkv_ref[...]).astype(BF16)
    kv_ref[...] = jnp.dot(ckv, wukv_ref[...], preferred_element_type=F32).astype(BF16)
    kr_ref[...] = _rope_block(seg(_E_KR, _E_QB), c32, s32, D_ROPE // 2, sec32).astype(BF16)
    rope64_store(seg(_E_QB, _E_KB), qb_ref)
    rope64_store(seg(_E_KB, _E_VB), kb_ref)
    va_ref[...] = jnp.where(lane < D_DSA, seg(_E_VB, _E_QI), 1.0).astype(BF16)
    rope64_store(seg(_E_QI, _E_KI), qi_ref)
    rope64_store(seg(_E_KI, _E_WI), ki_ref)
    wi_ref[...] = seg(_E_WI, _E_END)


def _even_weights(w_in, w_uq, w_ukv):
    o = [0]
    for s_ in EV_SPLITS:
        o.append(o[-1] + s_)
    cq, ckv, kr, qb, kb, vb, qi, ki, wi = (w_in[:, o[i]:o[i + 1]] for i in range(9))
    z = lambda n: jnp.zeros((D_MODEL, n), w_in.dtype)
    w1 = jnp.concatenate([
        cq, ckv,
        z(D_NOPE), kr, z(LANES - D_NOPE - D_ROPE),
        qb,
        kb, kb,
        vb, z(LANES - D_DSA),
        qi,
        ki, ki,
        wi, z(LANES - H_IDX)], axis=1).astype(BF16)
    wq = w_uq.reshape(Q_LORA, H_MLA, D_NOPE + D_ROPE)
    wq = jnp.pad(wq, ((0, 0), (0, 0), (0, LANES - D_NOPE - D_ROPE))).reshape(Q_LORA, H_MLA * LANES)
    return w1, wq.astype(BF16), w_ukv.astype(BF16)


def _even_proj(x2, w1, gq, gkv, wuq, wukv, tabs):
    T = x2.shape[0]
    tm = min(TM_PROJ, T)
    row = lambda n: pl.BlockSpec((tm, n), lambda i: (i, 0))
    outs = [(H_MLA * LANES, BF16), (H_MLA * LANES, BF16), (LANES, BF16), (H_DSA * D_DSA, BF16),
            (LANES, BF16), (LANES, BF16), (H_IDX * D_IDX, BF16), (LANES, BF16), (LANES, F32)]
    return pl.pallas_call(
        _even_proj_kernel,
        grid=(T // tm,),
        in_specs=[row(D_MODEL), _full_spec(w1.shape), _full_spec(gq.shape), _full_spec(gkv.shape),
                  _full_spec(wuq.shape), _full_spec(wukv.shape)] + [row(LANES)] * 4,
        out_specs=[row(n) for n, _ in outs],
        out_shape=[jax.ShapeDtypeStruct((T, n), d) for n, d in outs],
        compiler_params=_params("parallel"),
        name="even_proj",
    )(x2, w1, gq, gkv, wuq, wukv, *tabs)


def _odd_proj_kernel(x_ref, w_ref, bf_ref, q_ref, k_ref, v_ref, lf_ref):
    xb = x_ref[...].astype(BF16)
    n = H_FOX * D_FOX
    for j, out in enumerate((q_ref, k_ref, v_ref)):
        out[...] = jnp.dot(xb, w_ref[:, j * n:(j + 1) * n], preferred_element_type=F32).astype(BF16)
    z = jnp.dot(xb, w_ref[:, 3 * n:], preferred_element_type=F32) + bf_ref[...]
    lf_ref[...] = jnp.minimum(z, 0.0) - jnp.log1p(jnp.exp(-jnp.abs(z)))


def _odd_proj(x2, w, bf):
    T = x2.shape[0]
    tm = min(TM_PROJ, T)
    n = H_FOX * D_FOX
    row = lambda c: pl.BlockSpec((tm, c), lambda i: (i, 0))
    return pl.pallas_call(
        _odd_proj_kernel,
        grid=(T // tm,),
        in_specs=[row(D_MODEL), _full_spec(w.shape), _full_spec(bf.shape)],
        out_specs=[row(n), row(n), row(n), row(LANES)],
        out_shape=[jax.ShapeDtypeStruct((T, n), BF16)] * 3 + [jax.ShapeDtypeStruct((T, LANES), F32)],
        compiler_params=_params("parallel"),
        name="odd_proj",
    )(x2, w, bf)


def _cumsum_kernel(lf_ref, c_ref):
    a = lf_ref[...]
    rows = lax.broadcasted_iota(jnp.int32, a.shape, 0)
    k = 1
    while k < a.shape[0]:
        a = a + jnp.where(rows >= k, pltpu.roll(a, k, 0), 0.0)
        k *= 2
    c_ref[...] = a


def _cumsum_seq(lf, S):
    T = lf.shape[0]
    blk = pl.BlockSpec((S, LANES), lambda b: (b, 0))
    return pl.pallas_call(
        _cumsum_kernel, grid=(T // S,), in_specs=[blk], out_specs=blk,
        out_shape=jax.ShapeDtypeStruct((T, LANES), F32),
        compiler_params=_params("parallel"), name="forget_cumsum",
    )(lf)


def _flash_kernel(*refs, fox, scale, t, n_tiles, S):
    if fox:
        q_ref, k_ref, v_ref, cc_ref, cr_ref, o_ref, kx_ref, vt_ref, sb0_ref, sb1_ref = refs
    else:
        q_ref, kv_ref, kr_ref, o_ref, kx_ref, vt_ref, sb0_ref, sb1_ref = refs
    sb_refs = (sb0_ref, sb1_ref)
    hp = pl.program_id(1)
    lane = _lane_iota()
    lo = lane < D_FOX
    hi = lane >= D_FOX
    exp2_scale = scale * 1.4426950408889634

    ones = jnp.ones((VT_ROWS - D_FOX, S), BF16)
    if fox:
        vt = jnp.transpose(v_ref[0].astype(F32)).astype(BF16)
    for hh in range(2):
        if fox:
            col = jnp.sum(jnp.where(lane == 2 * hp + hh, cc_ref[0], 0.0), axis=-1, keepdims=True)
            kx_ref[hh] = jnp.broadcast_to(col, (S, LANES))
            vt_ref[hh, :D_FOX, :] = vt[hh * D_FOX:(hh + 1) * D_FOX]
        else:
            kvh = kv_ref[0, :, hh * LANES:(hh + 1) * LANES]
            kx_ref[hh] = jnp.where(lo, kvh, kr_ref[0])
            vt_ref[hh, :D_V_MLA, :] = jnp.transpose(kvh.astype(F32))[D_NOPE:].astype(BF16)
        vt_ref[hh, D_FOX:, :] = ones

    def body(n):
        qrows = slice(n * t, (n + 1) * t)
        half = t // 2
        blocks = [(c * t, t, 0, False) for c in range(n)] + [(n * t, half, 0, True), (n * t + half, half, half, True)]

        def on_lanes(full, q0, part, op):
            if q0 == 0:
                return op(full, part)
            return jnp.concatenate([full[:, :q0], op(full[:, q0:], part)], axis=1)

        outs = []
        for hh in range(2):
            if fox:
                q2 = q_ref[0, qrows, :] * jnp.asarray(scale, BF16)
                qh = jnp.where(lo if hh == 0 else hi, q2, jnp.zeros_like(q2))
                cq = cr_ref[0, pl.ds(2 * hp + hh, 1), qrows]
            else:
                qh = q_ref[0, qrows, hh * LANES:(hh + 1) * LANES]
            m8 = jnp.full((SUBLANES, t), -jnp.inf, F32)
            for k0, nk, q0, masked in blocks:
                nq = t - q0
                kh = k_ref[0, k0:k0 + nk, :] if fox else kx_ref[hh, k0:k0 + nk, :]
                s = lax.dot_general(kh, qh[q0:], _NT, preferred_element_type=F32)
                if fox:
                    s = s + cq[:, q0:] - jnp.concatenate([kx_ref[hh, k0:k0 + nk, :]] * (nq // LANES), axis=1)
                if masked:
                    k_idx = k0 + lax.broadcasted_iota(jnp.int32, (nk, 1), 0)
                    q_idx = n * t + q0 + lax.broadcasted_iota(jnp.int32, (1, nq), 1)
                    vis = (k_idx <= q_idx) if fox else ((k_idx >> CHUNK_SHIFT) <= (q_idx >> CHUNK_SHIFT))
                    s = jnp.where(vis, s, NEG_INF)
                sb_refs[hh][k0:k0 + nk, q0:] = s
                m8 = on_lanes(m8, q0, jnp.max(s.reshape(nk // SUBLANES, SUBLANES, nq), axis=0), jnp.maximum)
            m = jnp.max(m8, axis=0, keepdims=True)
            acc = jnp.zeros((VT_ROWS, t), F32)
            for k0, nk, q0, masked in blocks:
                x = sb_refs[hh][k0:k0 + nk, q0:] - m[:, q0:]
                p = jnp.exp(x) if fox else jnp.exp2(x * exp2_scale)
                pv = jnp.dot(vt_ref[hh, :, k0:k0 + nk], p.astype(BF16), preferred_element_type=F32)
                acc = on_lanes(acc, q0, pv, jnp.add)
            outs.append(acc[:D_FOX] / acc[D_FOX:D_FOX + 1])
        o_ref[0, qrows, :] = jnp.transpose(jnp.concatenate(outs, axis=0)).astype(BF16)

    for n in range(n_tiles):
        body(n)


def _flash(args, *, fox, B, S, scale):
    t = min(T_ATT, S)
    n = S // t
    npairs = (H_FOX if fox else H_MLA) // 2
    args = tuple(a.reshape(B, S, a.shape[-1]) if a.ndim == 2 else a for a in args)
    seq = lambda b, hp: (b, 0, hp)
    whole = lambda b, hp: (b, 0, 0)
    scratch = [pltpu.VMEM((2, S, LANES), F32 if fox else BF16), pltpu.VMEM((2, VT_ROWS, S), BF16),
               pltpu.VMEM((S, t), F32), pltpu.VMEM((S, t), F32)]
    if fox:
        in_specs = [pl.BlockSpec((1, S, LANES), seq), pl.BlockSpec((1, S, LANES), seq),
                    pl.BlockSpec((1, S, LANES), seq), pl.BlockSpec((1, S, LANES), whole),
                    pl.BlockSpec((1, H_FOX, S), whole)]
    else:
        in_specs = [pl.BlockSpec((1, S, 2 * LANES), seq), pl.BlockSpec((1, S, 2 * LANES), seq),
                    pl.BlockSpec((1, S, LANES), whole)]
    out = pl.pallas_call(
        functools.partial(_flash_kernel, fox=fox, scale=scale, t=t, n_tiles=n, S=S),
        grid=(B, npairs),
        in_specs=in_specs,
        out_specs=pl.BlockSpec((1, S, LANES), seq),
        out_shape=jax.ShapeDtypeStruct((B, S, npairs * LANES), BF16),
        scratch_shapes=scratch,
        compiler_params=_params("parallel", "parallel"),
        name="fox_attention" if fox else "mla_attention",
    )(*args)
    return out.reshape(B * S, npairs * LANES)


def _dsa_kernel(qi_ref, qb_ref, wit_ref, ki_ref, kb_ref, va_ref, o_ref,
                qim_ref, qbm_ref, sc_ref, sb_ref, *, nk, i0, ksel):
    i = i0 + pl.program_id(1)
    chunks = [slice(k0, min(k0 + KC_DSA, nk)) for k0 in range(0, nk, KC_DSA)]
    lane = _lane_iota()
    lo = lane < D_DSA
    hi = lane >= D_DSA
    q_chunk = (i * Q_DSA + lane) >> CHUNK_SHIFT
    select = nk > ksel

    def key_idx(rows):
        return rows.start + lax.broadcasted_iota(jnp.int32, (rows.stop - rows.start, 1), 0)

    def admissible(rows):
        return (key_idx(rows) >> CHUNK_SHIFT) <= q_chunk

    def fold_max(s):
        return jnp.max(s.reshape(s.shape[0] // SUBLANES, SUBLANES, s.shape[1]), axis=0)

    for h in range(H_DSA):
        blk = qb_ref[:, (h // 2) * LANES:(h // 2 + 1) * LANES] * jnp.asarray(D_DSA ** -0.5, BF16)
        qbm_ref[h * Q_DSA:(h + 1) * Q_DSA, :] = jnp.where(lo if h % 2 == 0 else hi, blk, jnp.zeros_like(blk))

    if select:
        for h in range(H_IDX):
            blk = qi_ref[:, (h // 2) * LANES:(h // 2 + 1) * LANES]
            qim_ref[h * Q_DSA:(h + 1) * Q_DSA, :] = jnp.where(lo if h % 2 == 0 else hi, blk, jnp.zeros_like(blk))
        w_flat = jnp.concatenate([wit_ref[h:h + 1, :] for h in range(H_IDX)], axis=1)
        for rows in chunks:
            r = lax.dot_general(ki_ref[0, rows, :], qim_ref[...], _NT, preferred_element_type=F32)
            r = jnp.maximum(r, 0.0) * w_flat
            acc = r[:, :LANES]
            for h in range(1, H_IDX):
                acc = acc + r[:, h * LANES:(h + 1) * LANES]
            sc_ref[rows, :] = jnp.where(admissible(rows), acc, NEG_INF)

        def count(mask):
            part = jnp.sum(mask.astype(F32).reshape(nk // 32, 32, LANES), axis=0)
            return jnp.sum(part, axis=0, keepdims=True)

        def as_float(key):
            return lax.bitcast_convert_type(jnp.where(key >= 0, key, key ^ 0x7FFFFFFF), F32)

        def bit_body(r, ans):
            cand = ans + lax.shift_left(jnp.int32(1), 31 - r)
            cnt = count(sc_ref[...] >= as_float(cand))
            return jnp.where(cnt >= ksel, cand, ans)
        ans = lax.fori_loop(0, 32, bit_body, jnp.full((1, LANES), jnp.iinfo(jnp.int32).min, jnp.int32))
        t = as_float(ans)
        s_all = sc_ref[...]
        need = ksel - count(s_all > t)
        n_ge = count(s_all >= t)
        k_all = lax.broadcasted_iota(jnp.int32, (nk, 1), 0)
        idx_bits = nk.bit_length()

        def tie_bound():
            def bit_body2(r, aj):
                cand = aj + lax.shift_left(jnp.int32(1), idx_bits - 1 - r)
                g = count((sc_ref[...] == t) & (k_all < cand))
                return jnp.where(g <= need, cand, aj)
            return lax.fori_loop(0, idx_bits, bit_body2, jnp.zeros((1, LANES), jnp.int32))

        jmax = lax.cond(jnp.max(n_ge) > ksel, tie_bound,
                        lambda: jnp.full((1, LANES), nk, jnp.int32))
        for rows in chunks:
            s = sc_ref[rows, :]
            sel = (s > t) | ((s == t) & (key_idx(rows) < jmax))
            sc_ref[rows, :] = jnp.where(sel & admissible(rows), 0.0, NEG_INF)
    else:
        for rows in chunks:
            sc_ref[rows, :] = jnp.where(admissible(rows), 0.0, NEG_INF)

    m8 = jnp.full((SUBLANES, H_DSA * LANES), -jnp.inf, F32)
    for rows in chunks:
        s = lax.dot_general(kb_ref[0, rows, :], qbm_ref[...], _NT, preferred_element_type=F32)
        s = s + jnp.concatenate([sc_ref[rows, :]] * H_DSA, axis=1)
        sb_ref[rows, :] = s
        m8 = jnp.maximum(m8, fold_max(s))
    m = jnp.max(m8, axis=0, keepdims=True)
    acc = jnp.zeros((H_DSA * Q_DSA, LANES), F32)
    for rows in chunks:
        p = jnp.exp(sb_ref[rows, :] - m).astype(BF16)
        acc = acc + lax.dot_general(p, va_ref[0, rows, :], _TN, preferred_element_type=F32)

    for hp in range(H_DSA // 2):
        a = acc[(2 * hp) * Q_DSA:(2 * hp + 1) * Q_DSA]
        b = acc[(2 * hp + 1) * Q_DSA:(2 * hp + 2) * Q_DSA]
        ra = a / pltpu.roll(a, D_DSA, 1)
        rb = b / pltpu.roll(b, D_DSA, 1)
        o_ref[:, hp * LANES:(hp + 1) * LANES] = jnp.where(lo, ra, pltpu.roll(rb, D_DSA, 1)).astype(BF16)


def _dsa(qi, qb, wit, ki2, kb2, va, *, B, S, ksel):
    nb = S // Q_DSA
    per = KC_DSA // Q_DSA
    seq3 = lambda a: a.reshape(B, S, LANES)
    ki3, kb3, va3 = seq3(ki2), seq3(kb2), seq3(va)
    outs = []
    for g in range(nb // per):
        nk, i0 = (g + 1) * per * Q_DSA, g * per
        qrow = lambda n, i0=i0: pl.BlockSpec((Q_DSA, n), lambda b, j: (b * nb + i0 + j, 0))
        seq = pl.BlockSpec((1, nk, LANES), lambda b, j: (b, 0, 0))
        o = pl.pallas_call(
            functools.partial(_dsa_kernel, nk=nk, i0=i0, ksel=ksel),
            grid=(B, per),
            in_specs=[qrow(H_IDX * D_IDX), qrow(H_DSA * D_DSA),
                      pl.BlockSpec((SUBLANES, Q_DSA), lambda b, j, i0=i0: (0, b * nb + i0 + j)),
                      seq, seq, seq],
            out_specs=pl.BlockSpec((Q_DSA, H_DSA * D_DSA), lambda b, j: (b * per + j, 0)),
            out_shape=jax.ShapeDtypeStruct((B * per * Q_DSA, H_DSA * D_DSA), BF16),
            scratch_shapes=[pltpu.VMEM((H_IDX * Q_DSA, LANES), BF16), pltpu.VMEM((H_DSA * Q_DSA, LANES), BF16),
                            pltpu.VMEM((nk, LANES), F32), pltpu.VMEM((nk, H_DSA * LANES), F32)],
            compiler_params=_params("parallel", "arbitrary"),
            name="dsa_attention_%d" % nk,
        )(qi, qb, wit, ki3, kb3, va3)
        outs.append(o.reshape(B, per * Q_DSA, H_DSA * D_DSA))
    return jnp.concatenate(outs, axis=1).reshape(B * S, H_DSA * D_DSA)


_R_SUB = N_GROUPS
_PAIRS = [(a, b) for a in range(EXPERTS_PER_GROUP) for b in range(a + 1, EXPERTS_PER_GROUP)]
N_CLASSES = N_GROUPS * len(_PAIRS)


def _out_ln_kernel(*refs, n_in):
    o_refs, w_refs = refs[:n_in], refs[n_in:2 * n_in]
    x_ref, g_ref, b_ref, wrt_ref, brt_ref, tri_ref, out_ref, route_ref, cnt_ref, run_ref = refs[2 * n_in:]
    y = jnp.dot(o_refs[0][...], w_refs[0][...], preferred_element_type=F32)
    for o_r, w_r in zip(o_refs[1:], w_refs[1:]):
        y = y + jnp.dot(o_r[...], w_r[...], preferred_element_type=F32)
    x_new = _layer_norm(DN_ALPHA * x_ref[...] + y, g_ref[...], b_ref[...])
    out_ref[...] = x_new
    _route_rows(x_new, wrt_ref, brt_ref, tri_ref, route_ref, cnt_ref, run_ref)


def _out_ln(os_, ws, x2, g, b, wr, br):
    T = x2.shape[0]
    tm = min(TM_PROJ, T)
    row = lambda c: pl.BlockSpec((tm, c), lambda i: (i, 0))
    tri = jnp.triu(jnp.ones((tm, tm), BF16), 1)
    wrt, brt = wr.T, br.T
    return pl.pallas_call(
        functools.partial(_out_ln_kernel, n_in=len(os_)),
        grid=(T // tm,),
        in_specs=[row(o.shape[1]) for o in os_] + [_full_spec(w.shape) for w in ws]
                 + [row(D_MODEL), _full_spec(g.shape), _full_spec(b.shape),
                    _full_spec(wrt.shape), _full_spec(brt.shape), _full_spec(tri.shape)],
        out_specs=[row(D_MODEL), pl.BlockSpec((1, SUBLANES, tm), lambda i: (i, 0, 0)), _full_spec((LANES, LANES))],
        out_shape=[jax.ShapeDtypeStruct((T, D_MODEL), F32), jax.ShapeDtypeStruct((T // tm, SUBLANES, tm), F32),
                   jax.ShapeDtypeStruct((LANES, LANES), F32)],
        scratch_shapes=[pltpu.VMEM((LANES, 1), F32)],
        compiler_params=_params("arbitrary"),
        name="out_proj_ln",
    )(*os_, *ws, x2, g, b, wrt, brt, tri)


def _route_rows(x, wrt_ref, brt_ref, tri_ref, route_ref, cnt_ref, run_ref):
    @pl.when(pl.program_id(0) == 0)
    def _():
        run_ref[...] = jnp.zeros(run_ref.shape, F32)

    tm = x.shape[0]
    lg = lax.dot_general(wrt_ref[...], x.astype(BF16), _NT, preferred_element_type=F32) + brt_ref[...]
    r = lax.broadcasted_iota(jnp.int32, (LANES, 1), 0)
    rf = r.astype(F32)
    far = float(LANES)
    grp = jnp.where(r < N_GROUPS, lg, -jnp.inf)
    mg = jnp.max(grp, axis=0, keepdims=True)
    g_star = jnp.min(jnp.where(grp == mg, rf, far), axis=0, keepdims=True)
    row_grp = ((r - _R_SUB) >> 2).astype(F32)
    in_grp = (row_grp == g_star) & (r >= _R_SUB) & (r < _R_SUB + N_EXPERTS)
    l1 = jnp.where(in_grp, lg, -jnp.inf)
    v1 = jnp.max(l1, axis=0, keepdims=True)
    j1 = jnp.min(jnp.where(l1 == v1, rf, far), axis=0, keepdims=True)
    l2 = jnp.where(in_grp & (rf != j1), lg, -jnp.inf)
    v2 = jnp.max(l2, axis=0, keepdims=True)
    j2 = jnp.min(jnp.where(l2 == v2, rf, far), axis=0, keepdims=True)
    base = _R_SUB + EXPERTS_PER_GROUP * g_star
    ja = jnp.minimum(j1, j2) - base
    jb = jnp.maximum(j1, j2) - base
    cls = g_star * len(_PAIRS) + ja * (7.0 - ja) * 0.5 + (jb - ja - 1.0)
    onehot = rf == cls
    ones = jnp.where(onehot, 1.0, 0.0)
    before = jnp.dot(ones.astype(BF16), tri_ref[...], preferred_element_type=F32)
    rank = jnp.sum(jnp.where(onehot, before + run_ref[...], 0.0), axis=0, keepdims=True)
    row8 = lax.broadcasted_iota(jnp.int32, (SUBLANES, 1), 0)
    route_ref[0] = jnp.where(row8 == 0, cls, jnp.where(row8 == 1, rank, 0.0))
    run_ref[...] += jnp.sum(ones, axis=1, keepdims=True)
    cnt_ref[...] = jnp.broadcast_to(run_ref[...], cnt_ref.shape)


def _row_copy(src_ref, dst_ref, src_row, dst_row, sem):
    return pltpu.make_async_copy(src_ref.at[pl.ds(src_row, 1), :], dst_ref.at[pl.ds(dst_row, 1), :], sem)


def _dispatch_kernel(pos_ref, x_ref, init_ref, xs_ref, sem):
    del init_ref
    tm = x_ref.shape[0]

    def start(g, carry):
        base = pl.multiple_of(g * SUBLANES, SUBLANES)
        for j in range(SUBLANES):
            _row_copy(x_ref, xs_ref, base + j, pos_ref[0, 0, base + j], sem).start()
        return carry
    lax.fori_loop(0, tm // SUBLANES, start, 0)

    def wait(t, carry):
        _row_copy(x_ref, xs_ref, 0, 0, sem).wait()
        return carry
    lax.fori_loop(0, tm, wait, 0, unroll=8)


def _collect_kernel(pos_ref, ys_ref, out_ref, sem):
    tm = out_ref.shape[0]

    def start(g, carry):
        base = pl.multiple_of(g * SUBLANES, SUBLANES)
        for j in range(SUBLANES):
            _row_copy(ys_ref, out_ref, pos_ref[0, 0, base + j], base + j, sem).start()
        return carry
    lax.fori_loop(0, tm // SUBLANES, start, 0)

    def wait(t, carry):
        _row_copy(ys_ref, out_ref, 0, 0, sem).wait()
        return carry
    lax.fori_loop(0, tm, wait, 0, unroll=8)


def _dispatch(x2, pos3, n_rows):
    T = x2.shape[0]
    tm = pos3.shape[-1]
    return pl.pallas_call(
        _dispatch_kernel,
        grid=(T // tm,),
        in_specs=[pl.BlockSpec((1, 1, tm), lambda i: (i, 0, 0), memory_space=pltpu.SMEM),
                  pl.BlockSpec((tm, D_MODEL), lambda i: (i, 0)),
                  pl.BlockSpec(memory_space=pl.ANY)],
        out_specs=pl.BlockSpec(memory_space=pl.ANY),
        out_shape=jax.ShapeDtypeStruct((n_rows, D_MODEL), F32),
        scratch_shapes=[pltpu.SemaphoreType.DMA(())],
        input_output_aliases={2: 0},
        compiler_params=_params("arbitrary"),
        name="moe_dispatch",
    )(pos3, x2, jnp.zeros((n_rows, D_MODEL), F32))


def _collect(ys, pos3):
    T = pos3.shape[0] * pos3.shape[-1]
    tm = pos3.shape[-1]
    return pl.pallas_call(
        _collect_kernel,
        grid=(T // tm,),
        in_specs=[pl.BlockSpec((1, 1, tm), lambda i: (i, 0, 0), memory_space=pltpu.SMEM),
                  pl.BlockSpec(memory_space=pl.ANY)],
        out_specs=pl.BlockSpec((tm, D_MODEL), lambda i: (i, 0)),
        out_shape=jax.ShapeDtypeStruct((T, D_MODEL), F32),
        scratch_shapes=[pltpu.SemaphoreType.DMA(())],
        compiler_params=_params("arbitrary"),
        name="moe_collect",
    )(pos3, ys)


def _expert_kernel(tg_ref, ta_ref, tb_ref, x_ref, wr_ref, br_ref,
                   wga_ref, wua_ref, wda_ref, wgb_ref, wub_ref, wdb_ref, g_ref, b_ref, out_ref, z_ref):
    step = pl.program_id(0)
    r = jnp.minimum(step, pl.num_programs(0) - 2)

    @pl.when(step == 0)
    def _():
        z_ref[...] = jnp.zeros(z_ref.shape, F32)

    out_ref[...] = _layer_norm(z_ref[...], g_ref[...], b_ref[...])
    x = x_ref[...]
    xb = x.astype(BF16)
    lg = jnp.dot(xb, wr_ref[...], preferred_element_type=F32) + br_ref[...]
    lane = _lane_iota()
    is_grp = lane < N_GROUPS
    mg = jnp.max(jnp.where(is_grp, lg, -jnp.inf), axis=-1, keepdims=True)
    den = jnp.sum(jnp.where(is_grp, jnp.exp(lg - mg), 0.0), axis=-1, keepdims=True)
    pick = lambda l: jnp.sum(jnp.where(lane == l, lg, 0.0), axis=-1, keepdims=True)
    p_top = jnp.exp(pick(tg_ref[r]) - mg) / den
    va, vb = pick(_R_SUB + ta_ref[r]), pick(_R_SUB + tb_ref[r])
    e = jnp.exp(jnp.minimum(va, vb) - jnp.maximum(va, vb))
    w_top, w_oth = 1.0 / (1.0 + e), e / (1.0 + e)
    a_top = va >= vb
    gates = (jnp.where(a_top, w_top, w_oth) * p_top, jnp.where(a_top, w_oth, w_top) * p_top)
    y = jnp.zeros(x.shape, F32)
    for gate, wg, wu, wd in ((gates[0], wga_ref, wua_ref, wda_ref), (gates[1], wgb_ref, wub_ref, wdb_ref)):
        a = jnp.dot(xb, wg[0], preferred_element_type=F32)
        u = jnp.dot(xb, wu[0], preferred_element_type=F32)
        h = (a * (1.0 / (1.0 + jnp.exp(-a))) * u).astype(BF16)
        y = y + gate * jnp.dot(h, wd[0], preferred_element_type=F32)
    z_ref[...] = DN_ALPHA * x + y


def _experts(xs, tile_g, tile_a, tile_b, wr, br, wg, wu, wd, g, b):
    n_tiles = tile_g.shape[0]
    te = xs.shape[0] // n_tiles
    tile = lambda r: jnp.minimum(r, n_tiles - 1)
    x_row = pl.BlockSpec((te, D_MODEL), lambda r, *_: (tile(r), 0))
    out_row = pl.BlockSpec((te, D_MODEL), lambda r, *_: (jnp.maximum(r - 1, 0), 0))
    full = lambda shape: pl.BlockSpec(shape, lambda r, *_: (0,) * len(shape))
    up_a = pl.BlockSpec((1, D_MODEL, D_EXPERT), lambda r, tg, ta, tb: (ta[tile(r)], 0, 0))
    up_b = pl.BlockSpec((1, D_MODEL, D_EXPERT), lambda r, tg, ta, tb: (tb[tile(r)], 0, 0))
    dn_a = pl.BlockSpec((1, D_EXPERT, D_MODEL), lambda r, tg, ta, tb: (ta[tile(r)], 0, 0))
    dn_b = pl.BlockSpec((1, D_EXPERT, D_MODEL), lambda r, tg, ta, tb: (tb[tile(r)], 0, 0))
    return pl.pallas_call(
        _expert_kernel,
        grid_spec=pltpu.PrefetchScalarGridSpec(
            num_scalar_prefetch=3, grid=(n_tiles + 1,),
            in_specs=[x_row, full(wr.shape), full(br.shape), up_a, up_a, dn_a, up_b, up_b, dn_b,
                      full(g.shape), full(b.shape)],
            out_specs=out_row,
            scratch_shapes=[pltpu.VMEM((te, D_MODEL), F32)]),
        out_shape=jax.ShapeDtypeStruct(xs.shape, F32),
        compiler_params=_params("arbitrary"),
        name="moe_experts",
    )(tile_g, tile_a, tile_b, xs, wr, br, wg, wu, wd, wg, wu, wd, g, b)


def _slot_kernel(route_ref, start_ref, pos_ref):
    rf = lax.broadcasted_iota(jnp.int32, (LANES, 1), 0).astype(F32)
    for j in range(route_ref.shape[0]):
        cls, rank = route_ref[j, 0:1, :], route_ref[j, 1:2, :]
        start = jnp.sum(jnp.where(rf == cls, start_ref[...], 0.0), axis=0, keepdims=True)
        pos_ref[j] = (start + rank).astype(jnp.int32)


def _slots(route, starts):
    n, _, tm = route.shape
    nb = min(n, 16)
    slots = pl.pallas_call(
        _slot_kernel,
        grid=(n // nb,),
        in_specs=[pl.BlockSpec((nb, SUBLANES, tm), lambda i: (i, 0, 0)), _full_spec(starts.shape)],
        out_specs=pl.BlockSpec((nb, 1, tm), lambda i: (i, 0, 0)),
        out_shape=jax.ShapeDtypeStruct((n, 1, tm), jnp.int32),
        compiler_params=_params("parallel"),
        name="moe_slots",
    )(route, starts)
    move = min(TM_MOVE, n * tm)
    return slots.reshape(n * tm // move, 1, move)


def _moe_ln(x2, route, counts, wr, br, wg, wu, wd, g, b):
    T = x2.shape[0]
    te = TE_MOE
    n_tiles = (T + N_CLASSES * (te - 1) + te - 1) // te
    cnt = counts[:N_CLASSES, 0].astype(jnp.int32)
    padded = (cnt + te - 1) // te * te
    ends = jnp.cumsum(padded)
    starts = jnp.pad((ends - padded).astype(F32), (0, LANES - N_CLASSES))[:, None]
    pos3 = _slots(route, starts)
    tile_start = jnp.arange(n_tiles, dtype=jnp.int32) * te
    tile_cls = jnp.minimum(jnp.sum(tile_start[:, None] >= ends[None, :], axis=1), N_CLASSES - 1).astype(jnp.int32)
    pair = jnp.asarray(_PAIRS, jnp.int32)[tile_cls % len(_PAIRS)]
    tile_g = tile_cls // len(_PAIRS)
    tile_a = tile_g * EXPERTS_PER_GROUP + pair[:, 0]
    tile_b = tile_g * EXPERTS_PER_GROUP + pair[:, 1]
    xs = _dispatch(x2, pos3, n_tiles * te)
    ys = _experts(xs, tile_g, tile_a, tile_b, wr, br, wg, wu, wd, g, b)
    return _collect(ys, pos3)


def _router_weights(w_grp, b_grp, w_sub, b_sub):
    pad = LANES - N_GROUPS - N_EXPERTS
    wr = jnp.concatenate([w_grp, w_sub, jnp.zeros((D_MODEL, pad), w_grp.dtype)], axis=1).astype(BF16)
    br = jnp.concatenate([b_grp, b_sub, jnp.zeros((pad,), b_grp.dtype)])[None].astype(F32)
    return wr, br


def kernel(x, positions, ev_w_in, ev_g_q, ev_g_kv, ev_w_uq, ev_w_ukv, ev_w_o, od_w_in, od_b_f, od_w_o,
           moe_w_grp, moe_b_grp, moe_w_sub, moe_b_sub, moe_w_gate, moe_w_up, moe_w_down,
           ln1_g, ln1_b, ln2_g, ln2_b):
    B, S, D = x.shape
    T = B * S
    ksel = min(TOPK_MAX, S // 4)
    x2 = x.reshape(T, D)
    tabs = _rope_tables(positions.reshape(T, 1).astype(F32))
    row = lambda v: v[None].astype(F32)
    for layer in range(DEPTH):
        j = layer // 2
        wr, br = _router_weights(moe_w_grp[layer], moe_b_grp[layer], moe_w_sub[layer], moe_b_sub[layer])
        ln1 = (row(ln1_g[layer]), row(ln1_b[layer]), wr, br)
        if layer % 2 == 0:
            w1, wuq, wukv = _even_weights(ev_w_in[j], ev_w_uq[j], ev_w_ukv[j])
            qa, kv, kr, qb, kb2, va, qi, ki2, wi = _even_proj(
                x2, w1, row(ev_g_q[j]), row(ev_g_kv[j]), wuq, wukv, tabs)
            o_a = _flash((qa, kv, kr), fox=False, B=B, S=S, scale=(D_NOPE + D_ROPE) ** -0.5)
            wit = jnp.pad(wi[:, :H_IDX].T, ((0, SUBLANES - H_IDX), (0, 0)))
            o_b = _dsa(qi, qb, wit, ki2, kb2, va, B=B, S=S, ksel=ksel)
            wo = ev_w_o[j].astype(BF16)
            n_a = H_MLA * D_V_MLA
            x2, route, counts = _out_ln((o_a, o_b), (wo[:n_a], wo[n_a:]), x2, *ln1)
        else:
            n = H_FOX * D_FOX
            w = jnp.pad(od_w_in[j], ((0, 0), (0, LANES - H_FOX))).astype(BF16)
            bf = jnp.pad(od_b_f[j], (0, LANES - H_FOX))[None].astype(F32)
            q, k, v, lf = _odd_proj(x2, w, bf)
            c = _cumsum_seq(lf, S)
            c_rows = c.reshape(B, S, LANES)[:, :, :H_FOX].swapaxes(1, 2)
            o = _flash((q, k, v, c, c_rows), fox=True, B=B, S=S, scale=D_FOX ** -0.5)
            x2, route, counts = _out_ln((o,), (od_w_o[j].astype(BF16),), x2, *ln1)
        x2 = _moe_ln(x2, route, counts, wr, br, moe_w_gate[layer].astype(BF16), moe_w_up[layer].astype(BF16),
                     moe_w_down[layer].astype(BF16), row(ln2_g[layer]), row(ln2_b[layer]))
    return x2.reshape(B, S, D)
```
